```python
import jax, jax.numpy as jnp
from jax import lax
import numpy as np

D_MODEL = 1024
BATCH = 2
SEQ = 8192
DEPTH = 2

A_HEADS = 8
A_HEAD_DIM = 64
A_LATENT = 128
IDX_HEADS = 8
IDX_DIM = 32
TOPK_MAX = 256
Q_BLOCK = 128
R_HEADS = 4
R_DK = 128
R_DV = 128
R_CHUNK = 128
CONV_WIDTH = 31
P_HEADS = 8
P_NKEYS = 128
P_EXPERTS = P_NKEYS * P_NKEYS
P_QDIM = 256
P_TOPK = 16
P_TOKEN_BLOCK = 128
EPS = 1e-6
IN_SIZES = (A_HEADS * A_HEAD_DIM,
            A_LATENT,
            IDX_HEADS * IDX_DIM,
            IDX_DIM,
            IDX_HEADS,
            R_HEADS * R_DK,
            R_HEADS * R_DK,
            R_HEADS * R_DV,
            R_HEADS * R_DV)
W_IN_COLS = sum(IN_SIZES)
MIX_WIDTH = A_HEADS * A_HEAD_DIM + R_HEADS * R_DV

kernel_name = 'hybrid_dsa_retention_conformer_peer'


def rms_norm(x, g):
    xf = x.astype(jnp.float32)
    y = xf * lax.rsqrt(jnp.mean(xf * xf, axis=-1, keepdims=True) + EPS)
    return (y * g.astype(jnp.float32)).astype(x.dtype)


def layer_norm(x, g, b):
    xf = x.astype(jnp.float32)
    mu = jnp.mean(xf, axis=-1, keepdims=True)
    var = jnp.mean(jnp.square(xf - mu), axis=-1, keepdims=True)
    y = (xf - mu) * lax.rsqrt(var + EPS)
    return (y * g.astype(jnp.float32) + b.astype(jnp.float32)).astype(x.dtype)


def dsa_sparse_attention(q_a, c_kv, q_idx, k_idx, w_idx, w_uk, w_uv):
    B, L, _ = q_a.shape
    topk = min(TOPK_MAX, L // 4)
    n_blk = L // Q_BLOCK
    q_a = q_a.reshape(B, L, A_HEADS, A_HEAD_DIM)
    q_lat = jnp.einsum('bshd,hcd->bshc', q_a, w_uk)
    q_idx = q_idx.reshape(B, L, IDX_HEADS, IDX_DIM)
    w_idx = w_idx * (IDX_HEADS ** -0.5)
    to_blocks = lambda t: t.reshape((B, n_blk, Q_BLOCK) + t.shape[2:]).swapaxes(0, 1)
    key_pos = jnp.arange(L)
    scale = A_HEAD_DIM ** -0.5

    def block(args):
        ql, qi, wi, blk = args
        q_pos = blk * Q_BLOCK + jnp.arange(Q_BLOCK)
        rel = jax.nn.relu(jnp.einsum('bqhd,bsd->bqhs', qi, k_idx))
        score = jnp.einsum('bqhs,bqh->bqs', rel, wi)
        causal = key_pos[None, :] <= q_pos[:, None]
        score = jnp.where(causal[None], score, -jnp.inf)
        _, idx = lax.top_k(score, topk)
        valid = idx <= q_pos[None, :, None]
        c_sel = jax.vmap(lambda c, i: c[i])(c_kv, idx)
        logits = jnp.einsum('bqhc,bqkc->bqhk', ql, c_sel).astype(jnp.float32) * scale
        logits = jnp.where(valid[:, :, None, :], logits, -jnp.inf)
        p = jax.nn.softmax(logits, axis=-1).astype(c_sel.dtype)
        return jnp.einsum('bqhk,bqkc->bqhc', p, c_sel)

    o_lat = lax.map(block, (to_blocks(q_lat), to_blocks(q_idx), to_blocks(w_idx),
                            jnp.arange(n_blk)))
    o_lat = o_lat.swapaxes(0, 1).reshape(B, L, A_HEADS, A_LATENT)
    out = jnp.einsum('bshc,hcd->bshd', o_lat, w_uv)
    return out.reshape(B, L, A_HEADS * A_HEAD_DIM)


def rotary(x):
    L, d = x.shape[1], x.shape[-1]
    inv_freq = 1.0 / (10000.0 ** jnp.linspace(0.0, 1.0, d // 2, dtype=jnp.float32))
    ang = jnp.arange(L, dtype=jnp.float32)[:, None] * inv_freq[None, :]
    cos = jnp.cos(ang)[None, :, None, :]
    sin = jnp.sin(ang)[None, :, None, :]
    x1 = x[..., 0::2]
    x2 = x[..., 1::2]
    return jnp.stack([x1 * cos - x2 * sin, x1 * sin + x2 * cos], axis=-1).reshape(x.shape)


def chunkwise_retention(q, k, v):
    B, L, _ = q.shape
    C = R_CHUNK
    N = L // C
    f32 = jnp.float32
    q = rotary(q.astype(f32).reshape(B, L, R_HEADS, R_DK))
    k = rotary(k.astype(f32).reshape(B, L, R_HEADS, R_DK)) * (R_DK ** -0.5)
    v = v.astype(f32).reshape(B, L, R_HEADS, R_DV)
    log_g = jnp.log1p(-jnp.exp2(-5.0 - jnp.arange(R_HEADS, dtype=f32)))
    pos = jnp.arange(C, dtype=f32)
    diff = pos[:, None] - pos[None, :]
    decay = jnp.where(diff >= 0, jnp.exp(jnp.maximum(diff, 0.0)[None] * log_g[:, None, None]), 0.0)
    xi = jnp.exp((pos + 1.0)[None, :] * log_g[:, None])
    zeta = jnp.exp((C - 1.0 - pos)[None, :] * log_g[:, None])
    g_chunk = jnp.exp(C * log_g)
    chunks = lambda t: t.reshape(B, N, C, R_HEADS, t.shape[-1]).transpose(1, 0, 3, 2, 4)
    qc, kc, vc = chunks(q), chunks(k), chunks(v)
    inner = jnp.einsum('nbhce,nbhse->nbhcs', qc, kc) * decay
    inner = jnp.einsum('nbhcs,nbhse->nbhce', inner, vc)

    def step(state, inp):
        qn, kn, vn = inp
        cross = jnp.einsum('bhcd,bhde->bhce', qn, state) * xi[None, :, :, None]
        state = state * g_chunk[None, :, None, None] + jnp.einsum(
            'bhcd,bhce->bhde', kn * zeta[None, :, :, None], vn)
        return state, cross

    state0 = jnp.zeros((B, R_HEADS, R_DK, R_DV), f32)
    _, cross = lax.scan(step, state0, (qc, kc, vc))
    out = (inner + cross).transpose(1, 0, 3, 2, 4).reshape(B, L, R_HEADS, R_DV)
    mu = jnp.mean(out, axis=-1, keepdims=True)
    var = jnp.mean(jnp.square(out - mu), axis=-1, keepdims=True)
    return (out - mu) * lax.rsqrt(var + EPS)


def attention_retention_mixer(h, w_in, kv_norm, w_uk, w_uv, w_o):
    B, L, _ = h.shape
    proj = h @ w_in
    points = np.cumsum(IN_SIZES)[:-1].tolist()
    qa, ckv, qi, ki, wi, rq, rk, rv, rg = jnp.split(proj, points, axis=-1)
    ckv = rms_norm(ckv, kv_norm)
    a_out = dsa_sparse_attention(qa, ckv, qi, ki, wi, w_uk, w_uv)
    ret = chunkwise_retention(rq, rk, rv).reshape(B, L, R_HEADS * R_DV).astype(h.dtype)
    b_out = jax.nn.silu(rg) * ret
    return jnp.concatenate([a_out, b_out], axis=-1) @ w_o


def conformer_conv_module(h, w_pw1, b_pw1, w_dw, b_dw, ln_g, ln_b, w_pw2, b_pw2):
    D = h.shape[-1]
    a = h @ w_pw1 + b_pw1
    a = a[..., :D] * jax.nn.sigmoid(a[..., D:])
    y = lax.conv_general_dilated(a, w_dw[:, None, :], window_strides=(1,),
                                 padding=((CONV_WIDTH - 1, 0),),
                                 dimension_numbers=('NWC', 'WIO', 'NWC'),
                                 feature_group_count=D) + b_dw
    y = jax.nn.silu(layer_norm(y, ln_g, ln_b))
    return y @ w_pw2 + b_pw2


def peer_ffn(h, w_q, sub_keys, u, v):
    B, L, D = h.shape
    T = B * L
    xf = h.reshape(T, D)
    q = (xf @ w_q).reshape(T, P_HEADS, 2, P_QDIM // 2)
    s1 = jnp.einsum('thd,kd->thk', q[:, :, 0], sub_keys[0])
    s2 = jnp.einsum('thd,kd->thk', q[:, :, 1], sub_keys[1])
    v1, i1 = lax.top_k(s1, P_TOPK)
    v2, i2 = lax.top_k(s2, P_TOPK)
    cand = (v1[..., :, None] + v2[..., None, :]).reshape(T, P_HEADS, P_TOPK * P_TOPK)
    sv, ci = lax.top_k(cand, P_TOPK)
    e1 = jnp.take_along_axis(i1, ci // P_TOPK, axis=-1)
    e2 = jnp.take_along_axis(i2, ci % P_TOPK, axis=-1)
    eidx = (e1 * P_NKEYS + e2).reshape(T, P_HEADS * P_TOPK)
    gate = jax.nn.softmax(sv.astype(jnp.float32), axis=-1).astype(h.dtype).reshape(T, P_HEADS * P_TOPK)
    n_blk = T // P_TOKEN_BLOCK

    def block(args):
        xb, ib, gb = args
        hb = jnp.einsum('td,tkd->tk', xb, u[ib])
        act = jax.nn.gelu(hb, approximate=False) * gb
        return jnp.einsum('tk,tkd->td', act, v[ib])

    out = lax.map(block, (xf.reshape(n_blk, P_TOKEN_BLOCK, D),
                          eidx.reshape(n_blk, P_TOKEN_BLOCK, -1),
                          gate.reshape(n_blk, P_TOKEN_BLOCK, -1)))
    return out.reshape(B, L, D)


def setup_inputs(seed: int = 0) -> dict:
    key = jax.random.key(seed)
    ks = iter(jax.random.split(key, 32))
    n_even = (DEPTH + 1) // 2
    n_odd = DEPTH // 2
    D = D_MODEL

    def nrm(shape, scale):
        return jax.random.normal(next(ks), shape, jnp.float32) * scale

    def gain(shape):
        return 1.0 + nrm(shape, 0.01)

    return {
        'x': nrm((BATCH, SEQ, D), 1.0),
        'mix_norm_e': gain((n_even, D)),
        'w_in': nrm((n_even, D, W_IN_COLS), D ** -0.5),
        'kv_norm': gain((n_even, A_LATENT)),
        'w_uk': nrm((n_even, A_HEADS, A_LATENT, A_HEAD_DIM), A_HEAD_DIM ** -0.5),
        'w_uv': nrm((n_even, A_HEADS, A_LATENT, A_HEAD_DIM), A_LATENT ** -0.5),
        'w_o': nrm((n_even, MIX_WIDTH, D), MIX_WIDTH ** -0.5),
        'mix_norm_o': gain((n_odd, D)),
        'conv_w1': nrm((n_odd, D, 2 * D), D ** -0.5),
        'conv_b1': nrm((n_odd, 2 * D), 0.01),
        'conv_dw': nrm((n_odd, CONV_WIDTH, D), CONV_WIDTH ** -0.5),
        'conv_dw_b': nrm((n_odd, D), 0.01),
        'conv_ln_g': gain((n_odd, D)),
        'conv_ln_b': nrm((n_odd, D), 0.01),
        'conv_w2': nrm((n_odd, D, D), D ** -0.5),
        'conv_b2': nrm((n_odd, D), 0.01),
        'ffn_norm': gain((DEPTH, D)),
        'peer_wq': nrm((DEPTH, D, P_HEADS * P_QDIM), D ** -0.5),
        'peer_keys': nrm((DEPTH, 2, P_NKEYS, P_QDIM // 2), (P_QDIM // 2) ** -0.5),
        'peer_u': nrm((DEPTH, P_EXPERTS, D), D ** -0.5),
        'peer_v': nrm((DEPTH, P_EXPERTS, D), D ** -0.5),
        'final_norm': gain((D,)),
    }


def reference(x, mix_norm_e, w_in, kv_norm, w_uk, w_uv, w_o, mix_norm_o, conv_w1, conv_b1,
              conv_dw, conv_dw_b, conv_ln_g, conv_ln_b, conv_w2, conv_b2, ffn_norm,
              peer_wq, peer_keys, peer_u, peer_v, final_norm):
    for layer in range(DEPTH):
        j = layer // 2
        if layer % 2 == 0:
            h = rms_norm(x, mix_norm_e[j])
            x = x + attention_retention_mixer(h, w_in[j], kv_norm[j], w_uk[j], w_uv[j], w_o[j])
        else:
            h = rms_norm(x, mix_norm_o[j])
            x = x + conformer_conv_module(h, conv_w1[j], conv_b1[j], conv_dw[j], conv_dw_b[j],
                                          conv_ln_g[j], conv_ln_b[j], conv_w2[j], conv_b2[j])
        h = rms_norm(x, ffn_norm[layer])
        x = x + peer_ffn(h, peer_wq[layer], peer_keys[layer], peer_u[layer], peer_v[layer])
    return rms_norm(x, final_norm)
```

```python
import functools

import jax
import jax.numpy as jnp
import numpy as np
from jax import lax
from jax.experimental import pallas as pl
from jax.experimental.pallas import tpu as pltpu

F32 = jnp.float32
BF16 = jnp.bfloat16
I32 = jnp.int32

EPS = 1e-6
A_HEADS, A_HEAD_DIM, A_LATENT = 8, 64, 128
IDX_HEADS, IDX_DIM, TOPK_MAX = 8, 32, 256
R_HEADS, R_DK, R_DV, R_CHUNK = 4, 128, 128, 128
CONV_WIDTH = 31
P_HEADS, P_NKEYS, P_QDIM, P_TOPK = 8, 128, 256, 16

LANES = 128
SUBLANES = 8
VMEM_LIMIT = 56 * 1024 * 1024
INT_MIN = -(2 ** 31)
NEG_BIG = -1e30

NT_DIMS = (((1,), (1,)), ((), ()))


def _cparams(sem):
    return pltpu.CompilerParams(dimension_semantics=sem, vmem_limit_bytes=VMEM_LIMIT)


def _rms(x, g):
    return x * lax.rsqrt(jnp.mean(x * x, axis=-1, keepdims=True) + EPS) * g


def _split(x):
    hi = x.astype(BF16)
    return hi, (x - hi.astype(F32)).astype(BF16)


def _dot3(a_hi, a_lo, b_hi, b_lo):
    d = functools.partial(jnp.dot, preferred_element_type=F32)
    return d(a_hi, b_hi) + (d(a_hi, b_lo) + d(a_lo, b_hi))


def _in_proj_kernel(x_ref, g_ref, w_ref, whi_ref, wlo_ref, o_ref, oi_ref):
    h = _rms(x_ref[...], g_ref[...])
    hi, lo = _split(h)
    o_ref[...] = jnp.dot(hi, w_ref[...], preferred_element_type=F32)
    oi_ref[...] = _dot3(hi, lo, whi_ref[...], wlo_ref[...])


def _in_proj(x, g, w_main, w_idx, tm=512):
    T, D = x.shape
    N, NI = w_main.shape[1], w_idx.shape[1]
    whi, wlo = _split(w_idx)
    return pl.pallas_call(
        _in_proj_kernel,
        grid=(T // tm,),
        in_specs=[pl.BlockSpec((tm, D), lambda i: (i, 0)),
                  pl.BlockSpec((1, D), lambda i: (0, 0)),
                  pl.BlockSpec((D, N), lambda i: (0, 0)),
                  pl.BlockSpec((D, NI), lambda i: (0, 0)),
                  pl.BlockSpec((D, NI), lambda i: (0, 0))],
        out_specs=[pl.BlockSpec((tm, N), lambda i: (i, 0)),
                   pl.BlockSpec((tm, NI), lambda i: (i, 0))],
        out_shape=[jax.ShapeDtypeStruct((T, N), F32), jax.ShapeDtypeStruct((T, NI), F32)],
        compiler_params=_cparams(("parallel",)),
    )(x, g.reshape(1, D), w_main.astype(BF16), whi, wlo)


def _norm_glu_kernel(x_ref, g_ref, wa_ref, wg_ref, ba_ref, bg_ref, o_ref, h_scr):
    @pl.when(pl.program_id(1) == 0)
    def _():
        h_scr[...] = _rms(x_ref[...], g_ref[...]).astype(BF16)

    h = h_scr[...]
    a = jnp.dot(h, wa_ref[...], preferred_element_type=F32) + ba_ref[...]
    gate = jnp.dot(h, wg_ref[...], preferred_element_type=F32) + bg_ref[...]
    o_ref[...] = a * jax.nn.sigmoid(gate)


def _norm_glu(x, g, w1, b1, tm=512, tn=512):
    T, D = x.shape
    nj = D // tn
    b1 = b1.reshape(1, 2 * D)
    return pl.pallas_call(
        _norm_glu_kernel,
        grid=(T // tm, nj),
        in_specs=[pl.BlockSpec((tm, D), lambda i, j: (i, 0)),
                  pl.BlockSpec((1, D), lambda i, j: (0, 0)),
                  pl.BlockSpec((D, tn), lambda i, j: (0, j)),
                  pl.BlockSpec((D, tn), lambda i, j: (0, j + nj)),
                  pl.BlockSpec((1, tn), lambda i, j: (0, j)),
                  pl.BlockSpec((1, tn), lambda i, j: (0, j + nj))],
        out_specs=pl.BlockSpec((tm, tn), lambda i, j: (i, j)),
        out_shape=jax.ShapeDtypeStruct((T, D), F32),
        scratch_shapes=[pltpu.VMEM((tm, D), BF16)],
        compiler_params=_cparams(("parallel", "arbitrary")),
    )(x, g.reshape(1, D), w1, w1, b1, b1)


def _mm_res_kernel(*refs, n_lhs, has_bias):
    lhs = refs[:n_lhs]
    ws = refs[n_lhs:2 * n_lhs]
    rest = refs[2 * n_lhs:]
    if has_bias:
        b_ref, res_ref, o_ref = rest
        acc = res_ref[...] + b_ref[...]
    else:
        res_ref, o_ref = rest
        acc = res_ref[...]
    for a_ref, w_ref in zip(lhs, ws):
        acc = acc + jnp.dot(a_ref[...], w_ref[...], preferred_element_type=F32)
    o_ref[...] = acc


def _mm_res(lhs, ws, res, bias=None, tm=512):
    T, N = res.shape
    n = len(lhs)
    in_specs = [pl.BlockSpec((tm, a.shape[1]), lambda i: (i, 0)) for a in lhs]
    in_specs += [pl.BlockSpec(w.shape, lambda i: (0, 0)) for w in ws]
    args = list(lhs) + list(ws)
    if bias is not None:
        in_specs.append(pl.BlockSpec((1, N), lambda i: (0, 0)))
        args.append(bias.reshape(1, N))
    in_specs.append(pl.BlockSpec((tm, N), lambda i: (i, 0)))
    args.append(res)
    return pl.pallas_call(
        functools.partial(_mm_res_kernel, n_lhs=n, has_bias=bias is not None),
        grid=(T // tm,),
        in_specs=in_specs,
        out_specs=pl.BlockSpec((tm, N), lambda i: (i, 0)),
        out_shape=jax.ShapeDtypeStruct((T, N), F32),
        compiler_params=_cparams(("parallel",)),
    )(*args)


def _dsa_kernel(qa_ref, qi_ref, misc_ref, ckv_ref, kit_ref, kvn_ref, wuk_ref, wuv_ref, o_ref,
                c_s, key_s, qst_s, qlat_s, wb_s, lg_s, bias_s, p_s, m_s, l_s, acc_s,
                *, Q, KC, topk):
    qb = pl.program_id(1)
    H = A_HEADS
    NL = KC // LANES

    @pl.when(qb == 0)
    def _():
        c_s[...] = _rms(ckv_ref[0], kvn_ref[...]).astype(BF16)

    qi = qi_ref[0]
    wi = misc_ref[0][:, IDX_DIM:IDX_DIM + IDX_HEADS] * (IDX_HEADS ** -0.5)
    for h in range(IDX_HEADS):
        hi, lo = _split(qi[:, h * IDX_DIM:(h + 1) * IDX_DIM])
        qst_s[h * Q:(h + 1) * Q, :] = jnp.concatenate([hi, lo, hi], axis=1)
        wb_s[h] = jnp.broadcast_to(wi[:, h:h + 1], (Q, LANES))
    ql = jnp.dot(qa_ref[0].astype(BF16), wuk_ref[...], preferred_element_type=F32) * (A_HEAD_DIM ** -0.5)
    for h in range(H):
        qlat_s[h * Q:(h + 1) * Q, :] = ql[:, h * A_LATENT:(h + 1) * A_LATENT].astype(BF16)

    n_chunks = ((qb + 1) * Q + KC - 1) // KC
    q_pos = qb * Q + lax.broadcasted_iota(I32, (Q, LANES), 0)
    lane = lax.broadcasted_iota(I32, (Q, LANES), 1)

    def score_chunk(c, carry):
        off = pl.multiple_of(c * KC, KC)
        lg_s[...] = jnp.dot(qst_s[...], kit_ref[0, :, pl.ds(off, KC)], preferred_element_type=F32)
        for l in range(NL):
            a = None
            for h in range(IDX_HEADS):
                t = jnp.maximum(lg_s[h * Q:(h + 1) * Q, l * LANES:(l + 1) * LANES], 0.0) * wb_s[h]
                a = t if a is None else a + t
            bits = pltpu.bitcast(a, I32)
            key = bits ^ ((bits >> 31) & 0x7FFFFFFF)
            key = jnp.where(off + l * LANES + lane <= q_pos, key, INT_MIN)
            key_s[:, pl.ds(pl.multiple_of(off + l * LANES, LANES), LANES)] = key
        return carry

    lax.fori_loop(0, n_chunks, score_chunk, 0)

    def bit_body(i, tau):
        cand = tau | lax.shift_left(jnp.int32(1), 31 - i)
        cand_s = cand ^ INT_MIN

        def cnt_body(c, cnt):
            off = pl.multiple_of(c * KC, KC)
            for l in range(NL):
                blk = key_s[:, pl.ds(pl.multiple_of(off + l * LANES, LANES), LANES)]
                cnt = cnt + jnp.where(blk >= cand_s, 1, 0)
            return cnt

        cnt = lax.fori_loop(0, n_chunks, cnt_body, jnp.zeros((Q, LANES), I32))
        tot = jnp.sum(cnt.astype(F32), axis=1, keepdims=True)
        return jnp.where(tot >= topk, cand, tau)

    tau = lax.fori_loop(0, 32, bit_body, jnp.zeros((Q, LANES), I32))
    thr = jnp.maximum(tau ^ INT_MIN, INT_MIN + 1)

    m_s[...] = jnp.full(m_s.shape, NEG_BIG, F32)
    l_s[...] = jnp.zeros(l_s.shape, F32)
    acc_s[...] = jnp.zeros(acc_s.shape, F32)

    def att_chunk(c, carry):
        off = pl.multiple_of(c * KC, KC)
        ck = c_s[pl.ds(off, KC), :]
        lg_s[...] = lax.dot_general(qlat_s[...], ck, NT_DIMS, preferred_element_type=F32)
        for l in range(NL):
            blk = key_s[:, pl.ds(pl.multiple_of(off + l * LANES, LANES), LANES)]
            bias_s[:, l * LANES:(l + 1) * LANES] = jnp.where(blk >= thr, 0.0, NEG_BIG)
        for h in range(H):
            rows = slice(h * Q, (h + 1) * Q)
            m_prev = m_s[rows, :]
            lgs = [lg_s[rows, l * LANES:(l + 1) * LANES] + bias_s[:, l * LANES:(l + 1) * LANES]
                   for l in range(NL)]
            mx = lgs[0]
            for l in range(1, NL):
                mx = jnp.maximum(mx, lgs[l])
            m_new = jnp.maximum(m_prev, jnp.max(mx, axis=1, keepdims=True))
            alpha = jnp.exp(m_prev - m_new)
            psum = None
            for l in range(NL):
                p = jnp.exp(lgs[l] - m_new)
                p_s[rows, l * LANES:(l + 1) * LANES] = p.astype(BF16)
                psum = p if psum is None else psum + p
            l_s[rows, :] = alpha * l_s[rows, :] + jnp.sum(psum, axis=1, keepdims=True)
            m_s[rows, :] = m_new
            acc_s[rows, :] = acc_s[rows, :] * alpha
        acc_s[...] += jnp.dot(p_s[...], ck, preferred_element_type=F32)
        return carry

    lax.fori_loop(0, n_chunks, att_chunk, 0)

    o = acc_s[...] / l_s[...]
    o_cat = jnp.concatenate([o[h * Q:(h + 1) * Q, :] for h in range(H)], axis=1).astype(BF16)
    o_ref[0] = jnp.dot(o_cat, wuv_ref[...], preferred_element_type=F32).astype(BF16)


def _dsa(proj3, idx3, kv_norm, wuk_bd, wuv_bd, Q=128, KC=512):
    B, L, _ = proj3.shape
    H = A_HEADS
    topk = min(TOPK_MAX, L // 4)
    k_hi, k_lo = _split(jnp.swapaxes(idx3[:, :, 256:256 + IDX_DIM], 1, 2))
    kit = jnp.concatenate([k_hi, k_hi, k_lo], axis=1)
    kern = functools.partial(_dsa_kernel, Q=Q, KC=KC, topk=topk)
    return pl.pallas_call(
        kern,
        grid=(B, L // Q),
        in_specs=[pl.BlockSpec((1, Q, 512), lambda b, q: (b, q, 4)),
                  pl.BlockSpec((1, Q, 256), lambda b, q: (b, q, 0)),
                  pl.BlockSpec((1, Q, 128), lambda b, q: (b, q, 2)),
                  pl.BlockSpec((1, L, 128), lambda b, q: (b, 0, 20)),
                  pl.BlockSpec((1, 3 * IDX_DIM, L), lambda b, q: (b, 0, 0)),
                  pl.BlockSpec((1, A_LATENT), lambda b, q: (0, 0)),
                  pl.BlockSpec(wuk_bd.shape, lambda b, q: (0, 0)),
                  pl.BlockSpec(wuv_bd.shape, lambda b, q: (0, 0))],
        out_specs=pl.BlockSpec((1, Q, H * A_HEAD_DIM), lambda b, q: (b, q, 0)),
        out_shape=jax.ShapeDtypeStruct((B, L, H * A_HEAD_DIM), BF16),
        scratch_shapes=[pltpu.VMEM((L, A_LATENT), BF16),
                        pltpu.VMEM((Q, L), I32),
                        pltpu.VMEM((IDX_HEADS * Q, 3 * IDX_DIM), BF16),
                        pltpu.VMEM((H * Q, A_LATENT), BF16),
                        pltpu.VMEM((IDX_HEADS, Q, LANES), F32),
                        pltpu.VMEM((H * Q, KC), F32),
                        pltpu.VMEM((Q, KC), F32),
                        pltpu.VMEM((H * Q, KC), BF16),
                        pltpu.VMEM((H * Q, LANES), F32),
                        pltpu.VMEM((H * Q, LANES), F32),
                        pltpu.VMEM((H * Q, A_LATENT), F32)],
        compiler_params=_cparams(("arbitrary", "arbitrary")),
    )(proj3, idx3, idx3, proj3, kit, kv_norm.reshape(1, A_LATENT), wuk_bd, wuv_bd)


def _ret_kernel(q_ref, k_ref, v_ref, g_ref, cos_ref, sin_ref, dec_ref, xi_ref, zeta_ref, gch_ref,
                o_ref, st_s, *, B):
    @pl.when(pl.program_id(0) == 0)
    def _():
        st_s[...] = jnp.zeros(st_s.shape, F32)

    C = R_CHUNK
    cos = cos_ref[...]
    sin = sin_ref[...]
    even = (lax.broadcasted_iota(I32, (C, R_DK), 1) & 1) == 0

    def rot(x):
        partner = jnp.where(even, pltpu.roll(x, R_DK - 1, 1), pltpu.roll(x, 1, 1))
        return x * cos + partner * sin

    for b in range(B):
        for h in range(R_HEADS):
            sl = slice(h * R_DK, (h + 1) * R_DK)
            q = rot(q_ref[b, :, sl])
            k = rot(k_ref[b, :, sl]) * (R_DK ** -0.5)
            vb = v_ref[b, :, sl].astype(BF16)
            qb = q.astype(BF16)
            s = lax.dot_general(qb, k.astype(BF16), NT_DIMS, preferred_element_type=F32) * dec_ref[h]
            inner = jnp.dot(s.astype(BF16), vb, preferred_element_type=F32)
            st = st_s[b, h]
            cross = jnp.dot(qb, st.astype(BF16), preferred_element_type=F32) * xi_ref[h]
            kz = (k * zeta_ref[h]).astype(BF16)
            st_s[b, h] = st * gch_ref[h] + jnp.dot(kz.T, vb, preferred_element_type=F32)
            out = inner + cross
            mu = jnp.mean(out, axis=-1, keepdims=True)
            d = out - mu
            var = jnp.mean(d * d, axis=-1, keepdims=True)
            y = d * lax.rsqrt(var + EPS)
            gate = g_ref[b, :, sl]
            o_ref[b, :, sl] = (gate * jax.nn.sigmoid(gate) * y).astype(BF16)


def _retention(proj3):
    B, L, _ = proj3.shape
    C, H = R_CHUNK, R_HEADS
    N = L // C
    d2 = R_DK // 2
    inv_freq = 1.0 / (10000.0 ** jnp.linspace(0.0, 1.0, d2, dtype=F32))
    ang = jnp.arange(L, dtype=F32)[:, None] * inv_freq[None, :]
    cos = jnp.repeat(jnp.cos(ang), 2, axis=1)
    sin = jnp.stack([-jnp.sin(ang), jnp.sin(ang)], axis=-1).reshape(L, R_DK)
    log_g = jnp.log1p(-jnp.exp2(-5.0 - jnp.arange(H, dtype=F32)))
    pos = jnp.arange(C, dtype=F32)
    diff = pos[:, None] - pos[None, :]
    decay = jnp.where(diff >= 0, jnp.exp(jnp.maximum(diff, 0.0)[None] * log_g[:, None, None]), 0.0)
    xi = jnp.broadcast_to(jnp.exp((pos + 1.0)[None, :] * log_g[:, None])[:, :, None], (H, C, R_DV))
    zeta = jnp.broadcast_to(jnp.exp((C - 1.0 - pos)[None, :] * log_g[:, None])[:, :, None], (H, C, R_DK))
    gch = jnp.broadcast_to(jnp.exp(C * log_g)[:, None, None], (H, 1, R_DV))
    W = H * R_DK
    colspec = lambda j: pl.BlockSpec((B, C, W), lambda n: (0, n, j))
    cst = lambda shape: pl.BlockSpec(shape, lambda n: (0,) * len(shape))
    return pl.pallas_call(
        functools.partial(_ret_kernel, B=B),
        grid=(N,),
        in_specs=[colspec(0), colspec(1), colspec(2), colspec(3),
                  pl.BlockSpec((C, R_DK), lambda n: (n, 0)),
                  pl.BlockSpec((C, R_DK), lambda n: (n, 0)),
                  cst((H, C, C)), cst((H, C, R_DV)), cst((H, C, R_DK)), cst((H, 1, R_DV))],
        out_specs=pl.BlockSpec((B, C, W), lambda n: (0, n, 0)),
        out_shape=jax.ShapeDtypeStruct((B, L, W), BF16),
        scratch_shapes=[pltpu.VMEM((B, H, R_DK, R_DV), F32)],
        compiler_params=_cparams(("arbitrary",)),
    )(proj3, proj3, proj3, proj3, cos, sin, decay, xi, zeta, gch)


HALO = 32


def _conv_kernel(cur_ref, prev_ref, w_ref, b_ref, lg_ref, lb_ref, o_ref, ext_s, *, tl):
    i = pl.program_id(1)
    ext_s[HALO:, :] = cur_ref[0]

    @pl.when(i == 0)
    def _():
        ext_s[:HALO, :] = jnp.zeros((HALO, ext_s.shape[1]), F32)

    @pl.when(i > 0)
    def _():
        ext_s[:HALO, :] = prev_ref[0]

    base = HALO - (CONV_WIDTH - 1)
    y = None
    for j in range(CONV_WIDTH):
        t = ext_s[base + j:base + j + tl, :] * w_ref[j:j + 1, :]
        y = t if y is None else y + t
    y = y + b_ref[...]
    mu = jnp.mean(y, axis=-1, keepdims=True)
    d = y - mu
    var = jnp.mean(d * d, axis=-1, keepdims=True)
    z = d * lax.rsqrt(var + EPS) * lg_ref[...] + lb_ref[...]
    o_ref[0] = (z * jax.nn.sigmoid(z)).astype(BF16)


def _conv_ln(a3, w_dw, b_dw, ln_g, ln_b, tl=256):
    B, L, D = a3.shape
    r = tl // HALO
    row = lambda v: v.reshape(1, D)
    return pl.pallas_call(
        functools.partial(_conv_kernel, tl=tl),
        grid=(B, L // tl),
        in_specs=[pl.BlockSpec((1, tl, D), lambda b, i: (b, i, 0)),
                  pl.BlockSpec((1, HALO, D), lambda b, i: (b, jnp.maximum(i * r - 1, 0), 0)),
                  pl.BlockSpec((CONV_WIDTH, D), lambda b, i: (0, 0)),
                  pl.BlockSpec((1, D), lambda b, i: (0, 0)),
                  pl.BlockSpec((1, D), lambda b, i: (0, 0)),
                  pl.BlockSpec((1, D), lambda b, i: (0, 0))],
        out_specs=pl.BlockSpec((1, tl, D), lambda b, i: (b, i, 0)),
        out_shape=jax.ShapeDtypeStruct((B, L, D), BF16),
        scratch_shapes=[pltpu.VMEM((tl + HALO, D), F32)],
        compiler_params=_cparams(("parallel", "arbitrary")),
    )(a3, a3, w_dw, row(b_dw), row(ln_g), row(ln_b))


def _peer_score_kernel(x_ref, g_ref, whi_ref, wlo_ref, k3_ref, h_ref, st_ref, hi_s, lo_s):
    @pl.when(pl.program_id(1) == 0)
    def _():
        hi, lo = _split(_rms(x_ref[...], g_ref[...]))
        hi_s[...] = hi
        lo_s[...] = lo
        h_ref[...] = hi

    q = _dot3(hi_s[...], lo_s[...], whi_ref[...], wlo_ref[...])
    q_hi, q_lo = _split(q)
    half = P_QDIM // 2
    for s in range(2):
        cols = slice(s * half, (s + 1) * half)
        q3 = jnp.concatenate([q_hi[:, cols], q_lo[:, cols], q_hi[:, cols]], axis=1)
        st_ref[0, s] = lax.dot_general(k3_ref[s], q3, NT_DIMS, preferred_element_type=F32)


def _peer_scores(x, g, wq, keys, tm=512):
    T, D = x.shape
    whi, wlo = _split(wq)
    k_hi, k_lo = _split(keys)
    k3 = jnp.concatenate([k_hi, k_hi, k_lo], axis=-1)
    return pl.pallas_call(
        _peer_score_kernel,
        grid=(T // tm, P_HEADS),
        in_specs=[pl.BlockSpec((tm, D), lambda i, j: (i, 0)),
                  pl.BlockSpec((1, D), lambda i, j: (0, 0)),
                  pl.BlockSpec((D, P_QDIM), lambda i, j: (0, j)),
                  pl.BlockSpec((D, P_QDIM), lambda i, j: (0, j)),
                  pl.BlockSpec(k3.shape, lambda i, j: (0, 0, 0))],
        out_specs=[pl.BlockSpec((tm, D), lambda i, j: (i, 0)),
                   pl.BlockSpec((1, 2, P_NKEYS, tm), lambda i, j: (j, 0, 0, i))],
        out_shape=[jax.ShapeDtypeStruct((T, D), BF16),
                   jax.ShapeDtypeStruct((P_HEADS, 2, P_NKEYS, T), F32)],
        scratch_shapes=[pltpu.VMEM((tm, D), BF16), pltpu.VMEM((tm, D), BF16)],
        compiler_params=_cparams(("parallel", "arbitrary")),
    )(x, g.reshape(1, D), whi, wlo, k3)


def _take_top(s, n):
    rows = []
    for _ in range(n):
        mx = jnp.max(s, axis=0, keepdims=True)
        rows.append(mx)
        s = jnp.where(s == mx, -jnp.inf, s)
    return rows, s


def _peer_topk_kernel(st_ref, o_ref):
    def head(h, carry):
        r1, _ = _take_top(st_ref[h, 0], P_TOPK)
        r2, _ = _take_top(st_ref[h, 1], P_TOPK)
        v2 = jnp.concatenate(r2, axis=0)
        cand = jnp.concatenate([r + v2 for r in r1], axis=0)
        top, rest = _take_top(cand, P_TOPK)
        nxt = jnp.max(rest, axis=0, keepdims=True)
        kth = top[-1]
        thr = jnp.where(nxt > -jnp.inf, 0.5 * (kth + nxt), kth)
        z = None
        for r in top:
            e = jnp.exp(r - top[0])
            z = e if z is None else z + e
        o_ref[0, pl.ds(h, 1), :] = thr
        o_ref[1, pl.ds(h, 1), :] = r1[0]
        o_ref[2, pl.ds(h, 1), :] = r2[0]
        o_ref[3, pl.ds(h, 1), :] = 1.0 / z
        return carry

    lax.fori_loop(0, P_HEADS, head, 0)


def _peer_topk(st, tt=256):
    T = st.shape[-1]
    return pl.pallas_call(
        _peer_topk_kernel,
        grid=(T // tt,),
        in_specs=[pl.BlockSpec((P_HEADS, 2, P_NKEYS, tt), lambda i: (0, 0, 0, i))],
        out_specs=pl.BlockSpec((4, P_HEADS, tt), lambda i: (0, 0, i)),
        out_shape=jax.ShapeDtypeStruct((4, P_HEADS, T), F32),
        compiler_params=_cparams(("parallel",)),
    )(st)


def _peer_dense_kernel(h_ref, u_ref, vt_ref, st_ref, stat_ref, res_ref, fg_ref, o_ref,
                       thr_s, coef_s, e2_s, hb_s, act_s, acc_s, *, te, tt, final):
    j = pl.program_id(1)
    NK = P_NKEYS

    @pl.when(j == 0)
    def _():
        for h in range(P_HEADS):
            s1 = st_ref[h, 0]
            thr_s[h] = stat_ref[0, h:h + 1, :] - s1
            coef_s[h] = jnp.exp(s1 - stat_ref[1, h:h + 1, :]) * stat_ref[3, h:h + 1, :]
            e2_s[h] = jnp.exp(st_ref[h, 1] - stat_ref[2, h:h + 1, :])
        acc_s[...] = jnp.zeros(acc_s.shape, F32)

    hb_s[...] = lax.dot_general(u_ref[...], h_ref[...], NT_DIMS, preferred_element_type=F32)
    a0 = pl.multiple_of(j * SUBLANES, SUBLANES)
    for al in range(te // NK):
        for c in range(tt // LANES):
            cols = slice(c * LANES, (c + 1) * LANES)
            g = None
            for h in range(P_HEADS):
                thr_row = thr_s[h, pl.ds(a0, SUBLANES), cols][al:al + 1]
                coef_row = coef_s[h, pl.ds(a0, SUBLANES), cols][al:al + 1]
                t = jnp.where(st_ref[h, 1, :, cols] >= thr_row, e2_s[h, :, cols], 0.0) * coef_row
                g = t if g is None else g + t
            hb = hb_s[al * NK:(al + 1) * NK, cols]
            gelu = 0.5 * hb * (1.0 + lax.erf(hb * (2.0 ** -0.5)))
            act_s[al * NK:(al + 1) * NK, cols] = (gelu * g).astype(BF16)
    acc_s[...] += jnp.dot(vt_ref[...], act_s[...], preferred_element_type=F32)

    @pl.when(j == pl.num_programs(1) - 1)
    def _():
        y = res_ref[...] + acc_s[...].T
        if final:
            y = _rms(y, fg_ref[...])
        o_ref[...] = y


def _peer_dense(hn, u, vt, st, stats, res, fgain, final, tt=512):
    T, D = res.shape
    E = u.shape[0]
    te = SUBLANES * P_NKEYS
    kern = functools.partial(_peer_dense_kernel, te=te, tt=tt, final=final)
    return pl.pallas_call(
        kern,
        grid=(T // tt, E // te),
        in_specs=[pl.BlockSpec((tt, D), lambda i, j: (i, 0)),
                  pl.BlockSpec((te, D), lambda i, j: (j, 0)),
                  pl.BlockSpec((D, te), lambda i, j: (0, j)),
                  pl.BlockSpec((P_HEADS, 2, P_NKEYS, tt), lambda i, j: (0, 0, 0, i)),
                  pl.BlockSpec((4, P_HEADS, tt), lambda i, j: (0, 0, i)),
                  pl.BlockSpec((tt, D), lambda i, j: (i, 0)),
                  pl.BlockSpec((1, D), lambda i, j: (0, 0))],
        out_specs=pl.BlockSpec((tt, D), lambda i, j: (i, 0)),
        out_shape=jax.ShapeDtypeStruct((T, D), F32),
        scratch_shapes=[pltpu.VMEM((P_HEADS, P_NKEYS, tt), F32),
                        pltpu.VMEM((P_HEADS, P_NKEYS, tt), F32),
                        pltpu.VMEM((P_HEADS, P_NKEYS, tt), F32),
                        pltpu.VMEM((te, tt), F32),
                        pltpu.VMEM((te, tt), BF16),
                        pltpu.VMEM((D, tt), F32)],
        compiler_params=_cparams(("parallel", "arbitrary")),
    )(hn, u, vt, st, stats, res, fgain.reshape(1, D))


def _peer(x, g, wq, keys, u, v, fgain, final):
    hn, st = _peer_scores(x, g, wq, keys)
    stats = _peer_topk(st)
    return _peer_dense(hn, u.astype(BF16), v.T.astype(BF16), st, stats, x, fgain, final)


def _block_diag(blocks):
    H, r, c = blocks.shape
    eye = jnp.eye(H, dtype=blocks.dtype)
    return (eye[:, None, :, None] * blocks[:, :, None, :]).reshape(H * r, H * c)


def _in_proj_weights(w_in):
    sizes = (A_HEADS * A_HEAD_DIM, A_LATENT, IDX_HEADS * IDX_DIM, IDX_DIM, IDX_HEADS,
             R_HEADS * R_DK, R_HEADS * R_DK, R_HEADS * R_DV, R_HEADS * R_DV)
    pts = np.cumsum(sizes)[:-1].tolist()
    qa, ckv, qi, ki, wi, rq, rk, rv, rg = jnp.split(w_in, pts, axis=1)
    pad = jnp.zeros((w_in.shape[0], LANES - IDX_DIM - IDX_HEADS), w_in.dtype)
    return (jnp.concatenate([rq, rk, rv, rg, qa, ckv], axis=1),
            jnp.concatenate([qi, ki, wi, pad], axis=1))


def kernel(x, mix_norm_e, w_in, kv_norm, w_uk, w_uv, w_o, mix_norm_o, conv_w1, conv_b1, conv_dw, conv_dw_b, conv_ln_g, conv_ln_b, conv_w2, conv_b2, ffn_norm, peer_wq, peer_keys, peer_u, peer_v, final_norm):
    B, L, D = x.shape
    T = B * L
    depth = ffn_norm.shape[0]
    xf = x.reshape(T, D)
    for layer in range(depth):
        j = layer // 2
        if layer % 2 == 0:
            proj, idx = _in_proj(xf, mix_norm_e[j], *_in_proj_weights(w_in[j]))
            proj3 = proj.reshape(B, L, proj.shape[1])
            idx3 = idx.reshape(B, L, idx.shape[1])
            wuk_bd = _block_diag(jnp.swapaxes(w_uk[j], 1, 2)).astype(BF16)
            wuv_bd = _block_diag(w_uv[j]).astype(BF16)
            a_out = _dsa(proj3, idx3, kv_norm[j], wuk_bd, wuv_bd)
            b_out = _retention(proj3)
            wo = w_o[j].astype(BF16)
            na = A_HEADS * A_HEAD_DIM
            xf = _mm_res([a_out.reshape(T, na), b_out.reshape(T, -1)], [wo[:na], wo[na:]], xf)
        else:
            a = _norm_glu(xf, mix_norm_o[j], conv_w1[j].astype(BF16), conv_b1[j])
            y = _conv_ln(a.reshape(B, L, D), conv_dw[j], conv_dw_b[j], conv_ln_g[j], conv_ln_b[j])
            xf = _mm_res([y.reshape(T, D)], [conv_w2[j].astype(BF16)], xf, bias=conv_b2[j])
        xf = _peer(xf, ffn_norm[layer], peer_wq[layer], peer_keys[layer], peer_u[layer], peer_v[layer],
                   final_norm, final=(layer == depth - 1))
    return xf.reshape(B, L, D)
```

```python
import functools

import jax
import jax.numpy as jnp
import numpy as np
from jax import lax
from jax.experimental import pallas as pl
from jax.experimental.pallas import tpu as pltpu

F32 = jnp.float32
BF16 = jnp.bfloat16
I32 = jnp.int32

EPS = 1e-6
A_HEADS, A_HEAD_DIM, A_LATENT = 8, 64, 128
IDX_HEADS, IDX_DIM, TOPK_MAX = 8, 32, 256
R_HEADS, R_DK, R_DV, R_CHUNK = 4, 128, 128, 128
CONV_WIDTH = 31
P_HEADS, P_NKEYS, P_QDIM, P_TOPK = 8, 128, 256, 16

LANES = 128
SUBLANES = 8
VMEM_LIMIT = 56 * 1024 * 1024
INT_MIN = -(2 ** 31)
NEG_BIG = -1e30

NT_DIMS = (((1,), (1,)), ((), ()))


def _cparams(sem):
    return pltpu.CompilerParams(dimension_semantics=sem, vmem_limit_bytes=VMEM_LIMIT)


def _aligned(x, m):
    if isinstance(x, int):
        assert x % m == 0
        return x
    return pl.multiple_of(x, m)


def _rms(x, g):
    return x * lax.rsqrt(jnp.mean(x * x, axis=-1, keepdims=True) + EPS) * g


def _split(x):
    hi = x.astype(BF16)
    return hi, (x - hi.astype(F32)).astype(BF16)


def _dot3(a_hi, a_lo, b_hi, b_lo):
    d = functools.partial(jnp.dot, preferred_element_type=F32)
    return d(a_hi, b_hi) + (d(a_hi, b_lo) + d(a_lo, b_hi))


def _in_proj_kernel(x_ref, g_ref, w_ref, whi_ref, wlo_ref, o_ref, oi_ref):
    h = _rms(x_ref[...], g_ref[...])
    hi, lo = _split(h)
    o_ref[...] = jnp.dot(hi, w_ref[...], preferred_element_type=F32)
    oi_ref[...] = _dot3(hi, lo, whi_ref[...], wlo_ref[...])


def _in_proj(x, g, w_main, w_idx, tm=512):
    T, D = x.shape
    N, NI = w_main.shape[1], w_idx.shape[1]
    whi, wlo = _split(w_idx)
    return pl.pallas_call(
        _in_proj_kernel,
        grid=(T // tm,),
        in_specs=[pl.BlockSpec((tm, D), lambda i: (i, 0)),
                  pl.BlockSpec((1, D), lambda i: (0, 0)),
                  pl.BlockSpec((D, N), lambda i: (0, 0)),
                  pl.BlockSpec((D, NI), lambda i: (0, 0)),
                  pl.BlockSpec((D, NI), lambda i: (0, 0))],
        out_specs=[pl.BlockSpec((tm, N), lambda i: (i, 0)),
                   pl.BlockSpec((tm, NI), lambda i: (i, 0))],
        out_shape=[jax.ShapeDtypeStruct((T, N), F32), jax.ShapeDtypeStruct((T, NI), F32)],
        compiler_params=_cparams(("parallel",)),
        name="in_proj",
    )(x, g.reshape(1, D), w_main.astype(BF16), whi, wlo)


def _norm_glu_kernel(x_ref, g_ref, wa_ref, wg_ref, ba_ref, bg_ref, o_ref, h_scr):
    @pl.when(pl.program_id(1) == 0)
    def _():
        h_scr[...] = _rms(x_ref[...], g_ref[...]).astype(BF16)

    h = h_scr[...]
    a = jnp.dot(h, wa_ref[...], preferred_element_type=F32) + ba_ref[...]
    gate = jnp.dot(h, wg_ref[...], preferred_element_type=F32) + bg_ref[...]
    o_ref[...] = a * jax.nn.sigmoid(gate)


def _norm_glu(x, g, w1, b1, tm=512, tn=512):
    T, D = x.shape
    nj = D // tn
    b1 = b1.reshape(1, 2 * D)
    return pl.pallas_call(
        _norm_glu_kernel,
        grid=(T // tm, nj),
        in_specs=[pl.BlockSpec((tm, D), lambda i, j: (i, 0)),
                  pl.BlockSpec((1, D), lambda i, j: (0, 0)),
                  pl.BlockSpec((D, tn), lambda i, j: (0, j)),
                  pl.BlockSpec((D, tn), lambda i, j: (0, j + nj)),
                  pl.BlockSpec((1, tn), lambda i, j: (0, j)),
                  pl.BlockSpec((1, tn), lambda i, j: (0, j + nj))],
        out_specs=pl.BlockSpec((tm, tn), lambda i, j: (i, j)),
        out_shape=jax.ShapeDtypeStruct((T, D), F32),
        scratch_shapes=[pltpu.VMEM((tm, D), BF16)],
        compiler_params=_cparams(("parallel", "arbitrary")),
        name="norm_glu",
    )(x, g.reshape(1, D), w1, w1, b1, b1)


def _mm_res_kernel(*refs, n_lhs, has_bias):
    lhs = refs[:n_lhs]
    ws = refs[n_lhs:2 * n_lhs]
    rest = refs[2 * n_lhs:]
    if has_bias:
        b_ref, res_ref, o_ref = rest
        acc = res_ref[...] + b_ref[...]
    else:
        res_ref, o_ref = rest
        acc = res_ref[...]
    for a_ref, w_ref in zip(lhs, ws):
        acc = acc + jnp.dot(a_ref[...], w_ref[...], preferred_element_type=F32)
    o_ref[...] = acc


def _mm_res(lhs, ws, res, bias=None, tm=512):
    T, N = res.shape
    n = len(lhs)
    in_specs = [pl.BlockSpec((tm, a.shape[1]), lambda i: (i, 0)) for a in lhs]
    in_specs += [pl.BlockSpec(w.shape, lambda i: (0, 0)) for w in ws]
    args = list(lhs) + list(ws)
    if bias is not None:
        in_specs.append(pl.BlockSpec((1, N), lambda i: (0, 0)))
        args.append(bias.reshape(1, N))
    in_specs.append(pl.BlockSpec((tm, N), lambda i: (i, 0)))
    args.append(res)
    return pl.pallas_call(
        functools.partial(_mm_res_kernel, n_lhs=n, has_bias=bias is not None),
        grid=(T // tm,),
        in_specs=in_specs,
        out_specs=pl.BlockSpec((tm, N), lambda i: (i, 0)),
        out_shape=jax.ShapeDtypeStruct((T, N), F32),
        compiler_params=_cparams(("parallel",)),
        name="mm_res",
    )(*args)


def _dsa_kernel(qa_ref, qi_ref, misc_ref, ckv_ref, kit_ref, kvn_ref, wuk_ref, wuv_ref, o_ref,
                c_s, key_s, qst_s, qlat_s, wb_s, lg_s, bias_s, cmax_s, p_s, m_s, l_s, acc_s,
                *, Q, KC, topk, col_bits):
    qb = pl.program_id(1)
    H = A_HEADS
    NL = KC // LANES

    @pl.when(qb == 0)
    def _():
        c_s[...] = _rms(ckv_ref[0], kvn_ref[...]).astype(BF16)

    qi = qi_ref[0]
    wi = misc_ref[0][:, IDX_DIM:IDX_DIM + IDX_HEADS] * (IDX_HEADS ** -0.5)
    for h in range(IDX_HEADS):
        hi, lo = _split(qi[:, h * IDX_DIM:(h + 1) * IDX_DIM])
        qst_s[h * Q:(h + 1) * Q, :] = jnp.concatenate([hi, lo, hi], axis=1)
        wb_s[h] = jnp.broadcast_to(wi[:, h:h + 1], (Q, LANES))
    ql = jnp.dot(qa_ref[0].astype(BF16), wuk_ref[...], preferred_element_type=F32) * (A_HEAD_DIM ** -0.5)
    for h in range(H):
        qlat_s[h * Q:(h + 1) * Q, :] = ql[:, h * A_LATENT:(h + 1) * A_LATENT].astype(BF16)

    n_chunks = ((qb + 1) * Q + KC - 1) // KC
    q_pos = qb * Q + lax.broadcasted_iota(I32, (Q, LANES), 0)
    lane = lax.broadcasted_iota(I32, (Q, LANES), 1)

    RG = 4 * SUBLANES
    lane_rg = lax.broadcasted_iota(I32, (RG, LANES), 1)
    row_rg = lax.broadcasted_iota(I32, (RG, LANES), 0)

    def score_chunk(c, carry):
        off = pl.multiple_of(c * KC, KC)
        lg_s[...] = jnp.dot(qst_s[...], kit_ref[0, :, pl.ds(off, KC)], preferred_element_type=F32)
        for r0 in range(0, Q, RG):
            a = None
            for h in range(IDX_HEADS):
                w = jnp.concatenate([wb_s[h, r0:r0 + RG, :]] * NL, axis=1)
                t = jnp.maximum(lg_s[h * Q + r0:h * Q + r0 + RG, :], 0.0) * w
                a = t if a is None else a + t
            bits = pltpu.bitcast(a, I32)
            sgn = bits >> 31
            key = (bits ^ (sgn & 0x7FFFFFFF)) - sgn
            for l in range(NL):
                causal = off + l * LANES + lane_rg <= qb * Q + r0 + row_rg
                key_s[c * NL + l, r0:r0 + RG, :] = jnp.where(causal, key[:, l * LANES:(l + 1) * LANES], INT_MIN)
        return carry

    lax.fori_loop(0, n_chunks, score_chunk, 0)

    def bit_body(i, tau):
        cand = tau | lax.shift_left(jnp.int32(1), 31 - i)
        cand_s = cand ^ INT_MIN

        def cnt_body(c, cnt):
            for l in range(NL):
                cnt = cnt + jnp.where(key_s[c * NL + l] >= cand_s, 1, 0)
            return cnt

        cnt = lax.fori_loop(0, n_chunks, cnt_body, jnp.zeros((Q, LANES), I32))
        tot = jnp.sum(cnt.astype(F32), axis=1, keepdims=True)
        return jnp.where(tot >= topk, cand, tau)

    tau = lax.fori_loop(0, 32, bit_body, jnp.zeros((Q, LANES), I32))
    thr = jnp.maximum(tau ^ INT_MIN, INT_MIN + 1)

    def tie_cnt(c, carry):
        ge, gt = carry
        for l in range(NL):
            blk = key_s[c * NL + l]
            ge = ge + jnp.where(blk >= thr, 1, 0)
            gt = gt + jnp.where(blk > thr, 1, 0)
        return ge, gt

    zero = jnp.zeros((Q, LANES), I32)
    ge, gt = lax.fori_loop(0, n_chunks, tie_cnt, (zero, zero))
    n_ge = jnp.sum(ge.astype(F32), axis=1, keepdims=True)
    need = topk - jnp.sum(gt.astype(F32), axis=1, keepdims=True)
    excess = n_ge > topk
    cmax_s[...] = jnp.full((Q, LANES), 2 ** 31 - 1, I32)

    @pl.when(jnp.max(jnp.where(excess, 1.0, 0.0)) > 0.0)
    def _():
        def col_bit(i, y):
            cand = y | lax.shift_left(jnp.int32(1), col_bits - 1 - i)

            def body(c, cnt):
                for l in range(NL):
                    hit = (key_s[c * NL + l] == thr) & (c * KC + l * LANES + lane < cand)
                    cnt = cnt + jnp.where(hit, 1, 0)
                return cnt

            cnt = lax.fori_loop(0, n_chunks, body, zero)
            tot = jnp.sum(cnt.astype(F32), axis=1, keepdims=True)
            return jnp.where(tot < need, cand, y)

        y = lax.fori_loop(0, col_bits, col_bit, zero)
        cmax_s[...] = jnp.where(excess, y, 2 ** 31 - 1)

    cmax = cmax_s[...]
    m_s[...] = jnp.full(m_s.shape, NEG_BIG, F32)
    l_s[...] = jnp.zeros(l_s.shape, F32)
    acc_s[...] = jnp.zeros(acc_s.shape, F32)

    def att_chunk(c, carry):
        off = pl.multiple_of(c * KC, KC)
        ck = c_s[pl.ds(off, KC), :]
        lg_s[...] = lax.dot_general(qlat_s[...], ck, NT_DIMS, preferred_element_type=F32)
        for l in range(NL):
            blk = key_s[c * NL + l]
            sel = (blk > thr) | ((blk == thr) & (off + l * LANES + lane <= cmax))
            bias_s[l] = jnp.where(sel, 0.0, NEG_BIG)

        for h in range(H):
            for r0 in range(0, Q, RG):
                rows = slice(h * Q + r0, h * Q + r0 + RG)
                m_prev = m_s[rows, :]
                lg = lg_s[rows, :]
                lgs = [lg[:, l * LANES:(l + 1) * LANES] + bias_s[l, r0:r0 + RG, :] for l in range(NL)]
                mx = lgs[0]
                for l in range(1, NL):
                    mx = jnp.maximum(mx, lgs[l])
                m_new = jnp.maximum(m_prev, jnp.max(mx, axis=1, keepdims=True))
                alpha = jnp.exp(m_prev - m_new)
                ps = [jnp.exp(x - m_new) for x in lgs]
                p_s[rows, :] = jnp.concatenate(ps, axis=1).astype(BF16)
                psum = ps[0]
                for l in range(1, NL):
                    psum = psum + ps[l]
                l_s[rows, :] = alpha * l_s[rows, :] + psum
                m_s[rows, :] = m_new
                acc_s[rows, :] = acc_s[rows, :] * alpha
        acc_s[...] += jnp.dot(p_s[...], ck, preferred_element_type=F32)
        return carry

    lax.fori_loop(0, n_chunks, att_chunk, 0)

    o = acc_s[...] / jnp.sum(l_s[...], axis=1, keepdims=True)
    o_cat = jnp.concatenate([o[h * Q:(h + 1) * Q, :] for h in range(H)], axis=1).astype(BF16)
    o_ref[0] = jnp.dot(o_cat, wuv_ref[...], preferred_element_type=F32).astype(BF16)


def _dsa(proj3, idx3, kv_norm, wuk_bd, wuv_bd, Q=128, KC=512):
    B, L, _ = proj3.shape
    H = A_HEADS
    topk = min(TOPK_MAX, L // 4)
    k_hi, k_lo = _split(jnp.swapaxes(idx3[:, :, 256:256 + IDX_DIM], 1, 2))
    kit = jnp.concatenate([k_hi, k_hi, k_lo], axis=1)
    kern = functools.partial(_dsa_kernel, Q=Q, KC=KC, topk=topk, col_bits=max(1, (L - 1).bit_length()))
    return pl.pallas_call(
        kern,
        grid=(B, L // Q),
        in_specs=[pl.BlockSpec((1, Q, 512), lambda b, q: (b, q, 4)),
                  pl.BlockSpec((1, Q, 256), lambda b, q: (b, q, 0)),
                  pl.BlockSpec((1, Q, 128), lambda b, q: (b, q, 2)),
                  pl.BlockSpec((1, L, 128), lambda b, q: (b, 0, 20)),
                  pl.BlockSpec((1, 3 * IDX_DIM, L), lambda b, q: (b, 0, 0)),
                  pl.BlockSpec((1, A_LATENT), lambda b, q: (0, 0)),
                  pl.BlockSpec(wuk_bd.shape, lambda b, q: (0, 0)),
                  pl.BlockSpec(wuv_bd.shape, lambda b, q: (0, 0))],
        out_specs=pl.BlockSpec((1, Q, H * A_HEAD_DIM), lambda b, q: (b, q, 0)),
        out_shape=jax.ShapeDtypeStruct((B, L, H * A_HEAD_DIM), BF16),
        scratch_shapes=[pltpu.VMEM((L, A_LATENT), BF16),
                        pltpu.VMEM((L // LANES, Q, LANES), I32),
                        pltpu.VMEM((IDX_HEADS * Q, 3 * IDX_DIM), BF16),
                        pltpu.VMEM((H * Q, A_LATENT), BF16),
                        pltpu.VMEM((IDX_HEADS, Q, LANES), F32),
                        pltpu.VMEM((H * Q, KC), F32),
                        pltpu.VMEM((KC // LANES, Q, LANES), F32),
                        pltpu.VMEM((Q, LANES), I32),
                        pltpu.VMEM((H * Q, KC), BF16),
                        pltpu.VMEM((H * Q, LANES), F32),
                        pltpu.VMEM((H * Q, LANES), F32),
                        pltpu.VMEM((H * Q, A_LATENT), F32)],
        compiler_params=_cparams(("arbitrary", "arbitrary")),
        name="dsa",
    )(proj3, idx3, idx3, proj3, kit, kv_norm.reshape(1, A_LATENT), wuk_bd, wuv_bd)


def _ret_kernel(q_ref, k_ref, v_ref, g_ref, cos_ref, sin_ref, dec_ref, xi_ref, zeta_ref, gch_ref,
                o_ref, st_s, *, B):
    @pl.when(pl.program_id(0) == 0)
    def _():
        st_s[...] = jnp.zeros(st_s.shape, F32)

    C = R_CHUNK
    cos = cos_ref[...]
    sin = sin_ref[...]
    even = (lax.broadcasted_iota(I32, (C, R_DK), 1) & 1) == 0

    def rot(x):
        partner = jnp.where(even, pltpu.roll(x, R_DK - 1, 1), pltpu.roll(x, 1, 1))
        return x * cos + partner * sin

    for b in range(B):
        for h in range(R_HEADS):
            sl = slice(h * R_DK, (h + 1) * R_DK)
            q = rot(q_ref[b, :, sl])
            k = rot(k_ref[b, :, sl]) * (R_DK ** -0.5)
            vb = v_ref[b, :, sl].astype(BF16)
            qb = q.astype(BF16)
            s = lax.dot_general(qb, k.astype(BF16), NT_DIMS, preferred_element_type=F32) * dec_ref[h]
            inner = jnp.dot(s.astype(BF16), vb, preferred_element_type=F32)
            st = st_s[b, h]
            cross = jnp.dot(qb, st.astype(BF16), preferred_element_type=F32) * xi_ref[h]
            kz = (k * zeta_ref[h]).astype(BF16)
            st_s[b, h] = st * gch_ref[h] + jnp.dot(kz.T, vb, preferred_element_type=F32)
            out = inner + cross
            mu = jnp.mean(out, axis=-1, keepdims=True)
            d = out - mu
            var = jnp.mean(d * d, axis=-1, keepdims=True)
            y = d * lax.rsqrt(var + EPS)
            gate = g_ref[b, :, sl]
            o_ref[b, :, sl] = (gate * jax.nn.sigmoid(gate) * y).astype(BF16)


def _retention(proj3):
    B, L, _ = proj3.shape
    C, H = R_CHUNK, R_HEADS
    N = L // C
    d2 = R_DK // 2
    inv_freq = 1.0 / (10000.0 ** jnp.linspace(0.0, 1.0, d2, dtype=F32))
    ang = jnp.arange(L, dtype=F32)[:, None] * inv_freq[None, :]
    cos = jnp.repeat(jnp.cos(ang), 2, axis=1)
    sin = jnp.stack([-jnp.sin(ang), jnp.sin(ang)], axis=-1).reshape(L, R_DK)
    log_g = jnp.log1p(-jnp.exp2(-5.0 - jnp.arange(H, dtype=F32)))
    pos = jnp.arange(C, dtype=F32)
    diff = pos[:, None] - pos[None, :]
    decay = jnp.where(diff >= 0, jnp.exp(jnp.maximum(diff, 0.0)[None] * log_g[:, None, None]), 0.0)
    xi = jnp.broadcast_to(jnp.exp((pos + 1.0)[None, :] * log_g[:, None])[:, :, None], (H, C, R_DV))
    zeta = jnp.broadcast_to(jnp.exp((C - 1.0 - pos)[None, :] * log_g[:, None])[:, :, None], (H, C, R_DK))
    gch = jnp.broadcast_to(jnp.exp(C * log_g)[:, None, None], (H, 1, R_DV))
    W = H * R_DK
    colspec = lambda j: pl.BlockSpec((B, C, W), lambda n: (0, n, j))
    cst = lambda shape: pl.BlockSpec(shape, lambda n: (0,) * len(shape))
    return pl.pallas_call(
        functools.partial(_ret_kernel, B=B),
        grid=(N,),
        in_specs=[colspec(0), colspec(1), colspec(2), colspec(3),
                  pl.BlockSpec((C, R_DK), lambda n: (n, 0)),
                  pl.BlockSpec((C, R_DK), lambda n: (n, 0)),
                  cst((H, C, C)), cst((H, C, R_DV)), cst((H, C, R_DK)), cst((H, 1, R_DV))],
        out_specs=pl.BlockSpec((B, C, W), lambda n: (0, n, 0)),
        out_shape=jax.ShapeDtypeStruct((B, L, W), BF16),
        scratch_shapes=[pltpu.VMEM((B, H, R_DK, R_DV), F32)],
        compiler_params=_cparams(("arbitrary",)),
        name="retention",
    )(proj3, proj3, proj3, proj3, cos, sin, decay, xi, zeta, gch)


HALO = 32


def _conv_kernel(cur_ref, prev_ref, w_ref, b_ref, lg_ref, lb_ref, o_ref, ext_s, *, tl):
    i = pl.program_id(1)
    ext_s[HALO:, :] = cur_ref[0]

    @pl.when(i == 0)
    def _():
        ext_s[:HALO, :] = jnp.zeros((HALO, ext_s.shape[1]), F32)

    @pl.when(i > 0)
    def _():
        ext_s[:HALO, :] = prev_ref[0]

    base = HALO - (CONV_WIDTH - 1)
    y = None
    for j in range(CONV_WIDTH):
        t = ext_s[base + j:base + j + tl, :] * w_ref[j:j + 1, :]
        y = t if y is None else y + t
    y = y + b_ref[...]
    mu = jnp.mean(y, axis=-1, keepdims=True)
    d = y - mu
    var = jnp.mean(d * d, axis=-1, keepdims=True)
    z = d * lax.rsqrt(var + EPS) * lg_ref[...] + lb_ref[...]
    o_ref[0] = (z * jax.nn.sigmoid(z)).astype(BF16)


def _conv_ln(a3, w_dw, b_dw, ln_g, ln_b, tl=256):
    B, L, D = a3.shape
    r = tl // HALO
    row = lambda v: v.reshape(1, D)
    return pl.pallas_call(
        functools.partial(_conv_kernel, tl=tl),
        grid=(B, L // tl),
        in_specs=[pl.BlockSpec((1, tl, D), lambda b, i: (b, i, 0)),
                  pl.BlockSpec((1, HALO, D), lambda b, i: (b, jnp.maximum(i * r - 1, 0), 0)),
                  pl.BlockSpec((CONV_WIDTH, D), lambda b, i: (0, 0)),
                  pl.BlockSpec((1, D), lambda b, i: (0, 0)),
                  pl.BlockSpec((1, D), lambda b, i: (0, 0)),
                  pl.BlockSpec((1, D), lambda b, i: (0, 0))],
        out_specs=pl.BlockSpec((1, tl, D), lambda b, i: (b, i, 0)),
        out_shape=jax.ShapeDtypeStruct((B, L, D), BF16),
        scratch_shapes=[pltpu.VMEM((tl + HALO, D), F32)],
        compiler_params=_cparams(("parallel", "arbitrary")),
        name="conv_ln",
    )(a3, a3, w_dw, row(b_dw), row(ln_g), row(ln_b))


def _peer_score_kernel(x_ref, g_ref, whi_ref, wlo_ref, k3_ref, h_ref, st_ref, hi_s, lo_s):
    @pl.when(pl.program_id(1) == 0)
    def _():
        hi, lo = _split(_rms(x_ref[...], g_ref[...]))
        hi_s[...] = hi
        lo_s[...] = lo
        h_ref[...] = hi

    q = _dot3(hi_s[...], lo_s[...], whi_ref[...], wlo_ref[...])
    q_hi, q_lo = _split(q)
    half = P_QDIM // 2
    for s in range(2):
        cols = slice(s * half, (s + 1) * half)
        q3 = jnp.concatenate([q_hi[:, cols], q_lo[:, cols], q_hi[:, cols]], axis=1)
        st_ref[0, s] = lax.dot_general(k3_ref[s], q3, NT_DIMS, preferred_element_type=F32)


def _peer_scores(x, g, wq, keys, tm=512):
    T, D = x.shape
    whi, wlo = _split(wq)
    k_hi, k_lo = _split(keys)
    k3 = jnp.concatenate([k_hi, k_hi, k_lo], axis=-1)
    return pl.pallas_call(
        _peer_score_kernel,
        grid=(T // tm, P_HEADS),
        in_specs=[pl.BlockSpec((tm, D), lambda i, j: (i, 0)),
                  pl.BlockSpec((1, D), lambda i, j: (0, 0)),
                  pl.BlockSpec((D, P_QDIM), lambda i, j: (0, j)),
                  pl.BlockSpec((D, P_QDIM), lambda i, j: (0, j)),
                  pl.BlockSpec(k3.shape, lambda i, j: (0, 0, 0))],
        out_specs=[pl.BlockSpec((tm, D), lambda i, j: (i, 0)),
                   pl.BlockSpec((1, 2, P_NKEYS, tm), lambda i, j: (j, 0, 0, i))],
        out_shape=[jax.ShapeDtypeStruct((T, D), BF16),
                   jax.ShapeDtypeStruct((P_HEADS, 2, P_NKEYS, T), F32)],
        scratch_shapes=[pltpu.VMEM((tm, D), BF16), pltpu.VMEM((tm, D), BF16)],
        compiler_params=_cparams(("parallel", "arbitrary")),
        name="peer_scores",
    )(x, g.reshape(1, D), whi, wlo, k3)


def _take_top(s, n):
    rows = []
    for _ in range(n):
        mx = jnp.max(s, axis=0, keepdims=True)
        rows.append(mx)
        s = jnp.where(s == mx, -jnp.inf, s)
    return rows, s


def _peer_topk_kernel(st_ref, o_ref):
    n = P_TOPK + 1
    tt = st_ref.shape[-1]

    def ninf(rows):
        return jnp.full((rows, tt), -jnp.inf, F32)

    def head(h, carry):
        r1, _ = _take_top(st_ref[h, 0], n)
        r2, _ = _take_top(st_ref[h, 1], n)
        blocks = [r1[0] + jnp.concatenate(r2 + [ninf(3 * SUBLANES - n)], axis=0)]
        v2 = jnp.concatenate(r2[:SUBLANES], axis=0)
        rank = lax.broadcasted_iota(I32, (SUBLANES, tt), 0)
        for i in range(1, SUBLANES):
            blocks.append(jnp.where(rank < n // (i + 1), r1[i] + v2, -jnp.inf))
        blocks.append(jnp.concatenate(r1[SUBLANES:] + [ninf(3 * SUBLANES - n)], axis=0) + r2[0])
        cand = jnp.concatenate(blocks, axis=0)
        top, rest = _take_top(cand, P_TOPK)
        nxt = jnp.max(rest, axis=0, keepdims=True)
        kth = top[-1]
        thr = jnp.where(nxt > -jnp.inf, 0.5 * (kth + nxt), kth)
        z = None
        for r in top:
            e = jnp.exp(r - top[0])
            z = e if z is None else z + e
        o_ref[0, pl.ds(h, 1), :] = thr
        o_ref[1, pl.ds(h, 1), :] = r1[0]
        o_ref[2, pl.ds(h, 1), :] = r2[0]
        o_ref[3, pl.ds(h, 1), :] = 1.0 / z
        return carry

    lax.fori_loop(0, P_HEADS, head, 0)


def _peer_topk(st, tt=256):
    T = st.shape[-1]
    return pl.pallas_call(
        _peer_topk_kernel,
        grid=(T // tt,),
        in_specs=[pl.BlockSpec((P_HEADS, 2, P_NKEYS, tt), lambda i: (0, 0, 0, i))],
        out_specs=pl.BlockSpec((4, P_HEADS, tt), lambda i: (0, 0, i)),
        out_shape=jax.ShapeDtypeStruct((4, P_HEADS, T), F32),
        compiler_params=_cparams(("parallel",)),
        name="peer_topk",
    )(st)


ROWS_PER_PART = 2
TILE_ROWS = 32


def _peer_dense_kernel(h_ref, u_ref, vt_ref, st_ref, stat_ref, res_ref, fg_ref, o_ref,
                       thr_s, coef_s, e2_s, bthr_s, bcoef_s, hb_s, g_s, act_s, acc_s, *, te, tt, final):
    j = pl.program_id(1)
    NK = P_NKEYS

    @pl.when(j == 0)
    def _():
        for h in range(P_HEADS):
            s1 = st_ref[h, 0]
            thr_s[h] = stat_ref[0, h:h + 1, :] - s1
            coef_s[h] = jnp.exp(s1 - stat_ref[1, h:h + 1, :]) * stat_ref[3, h:h + 1, :]
            e2_s[h] = jnp.exp(st_ref[h, 1] - stat_ref[2, h:h + 1, :])
        acc_s[...] = jnp.zeros(acc_s.shape, F32)

    a0 = pl.multiple_of(j * SUBLANES, SUBLANES)
    for h in range(P_HEADS):
        thr8 = thr_s[h, pl.ds(a0, SUBLANES), :]
        coef8 = coef_s[h, pl.ds(a0, SUBLANES), :]
        for al in range(SUBLANES):
            bthr_s[h, al] = jnp.broadcast_to(thr8[al:al + 1], (SUBLANES, tt))
            bcoef_s[h, al] = jnp.broadcast_to(coef8[al:al + 1], (SUBLANES, tt))

    reps = TILE_ROWS // SUBLANES
    n_part = SUBLANES // ROWS_PER_PART
    kw = h_ref.shape[1] // n_part

    def part(q, carry, first=False):
        ks = pl.ds(_aligned(q * kw, kw), kw)
        d = lax.dot_general(u_ref[:, ks], h_ref[:, ks], NT_DIMS, preferred_element_type=F32)
        hb_s[...] = d if first else hb_s[...] + d
        for r in range(ROWS_PER_PART):
            al = q * ROWS_PER_PART + r
            for b0 in range(0, NK, TILE_ROWS):
                sub = slice(b0, b0 + TILE_ROWS)
                g = None
                for h in range(P_HEADS):
                    thr = jnp.concatenate([bthr_s[h, al]] * reps, axis=0)
                    coef = jnp.concatenate([bcoef_s[h, al]] * reps, axis=0)
                    t = jnp.where(st_ref[h, 1, sub, :] >= thr, e2_s[h, sub, :], 0.0) * coef
                    g = t if g is None else g + t
                g_s[pl.ds(_aligned(al * NK + b0, TILE_ROWS), TILE_ROWS), :] = g
        return carry

    part(0, 0, first=True)
    lax.fori_loop(1, n_part, part, 0)

    for r in range(SUBLANES):
        rows = slice(r * NK, (r + 1) * NK)
        hb = hb_s[rows, :]
        gelu = 0.5 * hb * (1.0 + lax.erf(hb * (2.0 ** -0.5)))
        act_s[rows, :] = (gelu * g_s[rows, :]).astype(BF16)
    acc_s[...] += jnp.dot(vt_ref[...], act_s[...], preferred_element_type=F32)

    @pl.when(j == pl.num_programs(1) - 1)
    def _():
        y = res_ref[...] + acc_s[...].T
        if final:
            y = _rms(y, fg_ref[...])
        o_ref[...] = y


def _peer_dense(hn, u, vt, st, stats, res, fgain, final, tt=512):
    T, D = res.shape
    E = u.shape[0]
    te = SUBLANES * P_NKEYS
    kern = functools.partial(_peer_dense_kernel, te=te, tt=tt, final=final)
    return pl.pallas_call(
        kern,
        grid=(T // tt, E // te),
        in_specs=[pl.BlockSpec((tt, D), lambda i, j: (i, 0)),
                  pl.BlockSpec((te, D), lambda i, j: (j, 0)),
                  pl.BlockSpec((D, te), lambda i, j: (0, j)),
                  pl.BlockSpec((P_HEADS, 2, P_NKEYS, tt), lambda i, j: (0, 0, 0, i)),
                  pl.BlockSpec((4, P_HEADS, tt), lambda i, j: (0, 0, i)),
                  pl.BlockSpec((tt, D), lambda i, j: (i, 0)),
                  pl.BlockSpec((1, D), lambda i, j: (0, 0))],
        out_specs=pl.BlockSpec((tt, D), lambda i, j: (i, 0)),
        out_shape=jax.ShapeDtypeStruct((T, D), F32),
        scratch_shapes=[pltpu.VMEM((P_HEADS, P_NKEYS, tt), F32),
                        pltpu.VMEM((P_HEADS, P_NKEYS, tt), F32),
                        pltpu.VMEM((P_HEADS, P_NKEYS, tt), F32),
                        pltpu.VMEM((P_HEADS, SUBLANES, SUBLANES, tt), F32),
                        pltpu.VMEM((P_HEADS, SUBLANES, SUBLANES, tt), F32),
                        pltpu.VMEM((te, tt), F32),
                        pltpu.VMEM((te, tt), F32),
                        pltpu.VMEM((te, tt), BF16),
                        pltpu.VMEM((D, tt), F32)],
        compiler_params=_cparams(("parallel", "arbitrary")),
        name="peer_dense",
    )(hn, u, vt, st, stats, res, fgain.reshape(1, D))


def _peer(x, g, wq, keys, u, v, fgain, final):
    hn, st = _peer_scores(x, g, wq, keys)
    stats = _peer_topk(st)
    return _peer_dense(hn, u.astype(BF16), v.T.astype(BF16), st, stats, x, fgain, final)


def _block_diag(blocks):
    H, r, c = blocks.shape
    eye = jnp.eye(H, dtype=blocks.dtype)
    return (eye[:, None, :, None] * blocks[:, :, None, :]).reshape(H * r, H * c)


def _in_proj_weights(w_in):
    sizes = (A_HEADS * A_HEAD_DIM, A_LATENT, IDX_HEADS * IDX_DIM, IDX_DIM, IDX_HEADS,
             R_HEADS * R_DK, R_HEADS * R_DK, R_HEADS * R_DV, R_HEADS * R_DV)
    pts = np.cumsum(sizes)[:-1].tolist()
    qa, ckv, qi, ki, wi, rq, rk, rv, rg = jnp.split(w_in, pts, axis=1)
    pad = jnp.zeros((w_in.shape[0], LANES - IDX_DIM - IDX_HEADS), w_in.dtype)
    return (jnp.concatenate([rq, rk, rv, rg, qa, ckv], axis=1),
            jnp.concatenate([qi, ki, wi, pad], axis=1))


def kernel(x, mix_norm_e, w_in, kv_norm, w_uk, w_uv, w_o, mix_norm_o, conv_w1, conv_b1, conv_dw, conv_dw_b, conv_ln_g, conv_ln_b, conv_w2, conv_b2, ffn_norm, peer_wq, peer_keys, peer_u, peer_v, final_norm):
    B, L, D = x.shape
    T = B * L
    depth = ffn_norm.shape[0]
    xf = x.reshape(T, D)
    for layer in range(depth):
        j = layer // 2
        if layer % 2 == 0:
            proj, idx = _in_proj(xf, mix_norm_e[j], *_in_proj_weights(w_in[j]))
            proj3 = proj.reshape(B, L, proj.shape[1])
            idx3 = idx.reshape(B, L, idx.shape[1])
            wuk_bd = _block_diag(jnp.swapaxes(w_uk[j], 1, 2)).astype(BF16)
            wuv_bd = _block_diag(w_uv[j]).astype(BF16)
            a_out = _dsa(proj3, idx3, kv_norm[j], wuk_bd, wuv_bd)
            b_out = _retention(proj3)
            wo = w_o[j].astype(BF16)
            na = A_HEADS * A_HEAD_DIM
            xf = _mm_res([a_out.reshape(T, na), b_out.reshape(T, -1)], [wo[:na], wo[na:]], xf)
        else:
            a = _norm_glu(xf, mix_norm_o[j], conv_w1[j].astype(BF16), conv_b1[j])
            y = _conv_ln(a.reshape(B, L, D), conv_dw[j], conv_dw_b[j], conv_ln_g[j], conv_ln_b[j])
            xf = _mm_res([y.reshape(T, D)], [conv_w2[j].astype(BF16)], xf, bias=conv_b2[j])
        xf = _peer(xf, ffn_norm[layer], peer_wq[layer], peer_keys[layer], peer_u[layer], peer_v[layer],
                   final_norm, final=(layer == depth - 1))
    return xf.reshape(B, L, D)
```

```python
import functools

import jax
import jax.numpy as jnp
import numpy as np
from jax import lax
from jax.experimental import pallas as pl
from jax.experimental.pallas import tpu as pltpu

F32 = jnp.float32
BF16 = jnp.bfloat16
I32 = jnp.int32

EPS = 1e-6
A_HEADS, A_HEAD_DIM, A_LATENT = 8, 64, 128
IDX_HEADS, IDX_DIM, TOPK_MAX = 8, 32, 256
R_HEADS, R_DK, R_DV, R_CHUNK = 4, 128, 128, 128
CONV_WIDTH = 31
P_HEADS, P_NKEYS, P_QDIM, P_TOPK = 8, 128, 256, 16

LANES = 128
SUBLANES = 8
VMEM_LIMIT = 56 * 1024 * 1024
INT_MIN = -(2 ** 31)
NEG_BIG = -1e30
LOG2_E = 1.4426950408889634

NT_DIMS = (((1,), (1,)), ((), ()))


def _cparams(sem):
    return pltpu.CompilerParams(dimension_semantics=sem, vmem_limit_bytes=VMEM_LIMIT)


def _aligned(x, m):
    if isinstance(x, int):
        assert x % m == 0
        return x
    return pl.multiple_of(x, m)


def _rms(x, g):
    return x * lax.rsqrt(jnp.mean(x * x, axis=-1, keepdims=True) + EPS) * g


def _split(x):
    hi = x.astype(BF16)
    return hi, (x - hi.astype(F32)).astype(BF16)


def _dot3(a_hi, a_lo, b_hi, b_lo):
    d = functools.partial(jnp.dot, preferred_element_type=F32)
    return d(a_hi, b_hi) + (d(a_hi, b_lo) + d(a_lo, b_hi))


def _in_proj_kernel(x_ref, g_ref, w_ref, whi_ref, wlo_ref, o_ref, oi_ref):
    h = _rms(x_ref[...], g_ref[...])
    hi, lo = _split(h)
    o_ref[...] = jnp.dot(hi, w_ref[...], preferred_element_type=F32)
    oi_ref[...] = _dot3(hi, lo, whi_ref[...], wlo_ref[...])


def _in_proj(x, g, w_main, w_idx, tm=512):
    T, D = x.shape
    N, NI = w_main.shape[1], w_idx.shape[1]
    whi, wlo = _split(w_idx)
    return pl.pallas_call(
        _in_proj_kernel,
        grid=(T // tm,),
        in_specs=[pl.BlockSpec((tm, D), lambda i: (i, 0)),
                  pl.BlockSpec((1, D), lambda i: (0, 0)),
                  pl.BlockSpec((D, N), lambda i: (0, 0)),
                  pl.BlockSpec((D, NI), lambda i: (0, 0)),
                  pl.BlockSpec((D, NI), lambda i: (0, 0))],
        out_specs=[pl.BlockSpec((tm, N), lambda i: (i, 0)),
                   pl.BlockSpec((tm, NI), lambda i: (i, 0))],
        out_shape=[jax.ShapeDtypeStruct((T, N), F32), jax.ShapeDtypeStruct((T, NI), F32)],
        compiler_params=_cparams(("parallel",)),
        name="in_proj",
    )(x, g.reshape(1, D), w_main.astype(BF16), whi, wlo)


def _norm_glu_kernel(x_ref, g_ref, wa_ref, wg_ref, ba_ref, bg_ref, o_ref, h_scr):
    @pl.when(pl.program_id(1) == 0)
    def _():
        h_scr[...] = _rms(x_ref[...], g_ref[...]).astype(BF16)

    h = h_scr[...]
    a = jnp.dot(h, wa_ref[...], preferred_element_type=F32) + ba_ref[...]
    gate = jnp.dot(h, wg_ref[...], preferred_element_type=F32) + bg_ref[...]
    o_ref[...] = a * jax.nn.sigmoid(gate)


def _norm_glu(x, g, w1, b1, tm=512, tn=512):
    T, D = x.shape
    nj = D // tn
    b1 = b1.reshape(1, 2 * D)
    return pl.pallas_call(
        _norm_glu_kernel,
        grid=(T // tm, nj),
        in_specs=[pl.BlockSpec((tm, D), lambda i, j: (i, 0)),
                  pl.BlockSpec((1, D), lambda i, j: (0, 0)),
                  pl.BlockSpec((D, tn), lambda i, j: (0, j)),
                  pl.BlockSpec((D, tn), lambda i, j: (0, j + nj)),
                  pl.BlockSpec((1, tn), lambda i, j: (0, j)),
                  pl.BlockSpec((1, tn), lambda i, j: (0, j + nj))],
        out_specs=pl.BlockSpec((tm, tn), lambda i, j: (i, j)),
        out_shape=jax.ShapeDtypeStruct((T, D), F32),
        scratch_shapes=[pltpu.VMEM((tm, D), BF16)],
        compiler_params=_cparams(("parallel", "arbitrary")),
        name="norm_glu",
    )(x, g.reshape(1, D), w1, w1, b1, b1)


def _mm_res_kernel(*refs, n_lhs, has_bias):
    lhs = refs[:n_lhs]
    ws = refs[n_lhs:2 * n_lhs]
    rest = refs[2 * n_lhs:]
    if has_bias:
        b_ref, res_ref, o_ref = rest
        acc = res_ref[...] + b_ref[...]
    else:
        res_ref, o_ref = rest
        acc = res_ref[...]
    for a_ref, w_ref in zip(lhs, ws):
        acc = acc + jnp.dot(a_ref[...], w_ref[...], preferred_element_type=F32)
    o_ref[...] = acc


def _mm_res(lhs, ws, res, bias=None, tm=512):
    T, N = res.shape
    n = len(lhs)
    in_specs = [pl.BlockSpec((tm, a.shape[1]), lambda i: (i, 0)) for a in lhs]
    in_specs += [pl.BlockSpec(w.shape, lambda i: (0, 0)) for w in ws]
    args = list(lhs) + list(ws)
    if bias is not None:
        in_specs.append(pl.BlockSpec((1, N), lambda i: (0, 0)))
        args.append(bias.reshape(1, N))
    in_specs.append(pl.BlockSpec((tm, N), lambda i: (i, 0)))
    args.append(res)
    return pl.pallas_call(
        functools.partial(_mm_res_kernel, n_lhs=n, has_bias=bias is not None),
        grid=(T // tm,),
        in_specs=in_specs,
        out_specs=pl.BlockSpec((tm, N), lambda i: (i, 0)),
        out_shape=jax.ShapeDtypeStruct((T, N), F32),
        compiler_params=_cparams(("parallel",)),
        name="mm_res",
    )(*args)


def _dsa_kernel(qa_ref, qi_ref, misc_ref, ckv_ref, kit_ref, kvn_ref, wuk_ref, wuv_ref, o_ref,
                c_s, key_s, qst_s, qlat_s, wb_s, lg_s, bias_s, cmax_s, p_s, m_s, l_s, acc_s,
                *, Q, KC, topk, col_bits):
    qb = pl.program_id(1)
    H = A_HEADS
    NL = KC // LANES

    @pl.when(qb == 0)
    def _():
        c_s[...] = _rms(ckv_ref[0], kvn_ref[...]).astype(BF16)

    qi = qi_ref[0]
    wi = misc_ref[0][:, IDX_DIM:IDX_DIM + IDX_HEADS] * (IDX_HEADS ** -0.5)
    for h in range(IDX_HEADS):
        hi, lo = _split(qi[:, h * IDX_DIM:(h + 1) * IDX_DIM])
        qst_s[h * Q:(h + 1) * Q, :] = jnp.concatenate([hi, lo, hi], axis=1)
        wb_s[h] = jnp.broadcast_to(wi[:, h:h + 1], (Q, LANES))
    ql = jnp.dot(qa_ref[0].astype(BF16), wuk_ref[...], preferred_element_type=F32) * (A_HEAD_DIM ** -0.5 * LOG2_E)
    for h in range(H):
        qlat_s[h * Q:(h + 1) * Q, :] = ql[:, h * A_LATENT:(h + 1) * A_LATENT].astype(BF16)

    n_chunks = ((qb + 1) * Q + KC - 1) // KC
    q_pos = qb * Q + lax.broadcasted_iota(I32, (Q, LANES), 0)
    lane = lax.broadcasted_iota(I32, (Q, LANES), 1)

    RG = 4 * SUBLANES
    lane_rg = lax.broadcasted_iota(I32, (RG, LANES), 1)
    row_rg = lax.broadcasted_iota(I32, (RG, LANES), 0)

    def score_chunk(c, carry):
        off = pl.multiple_of(c * KC, KC)
        lg_s[...] = jnp.dot(qst_s[...], kit_ref[0, :, pl.ds(off, KC)], preferred_element_type=F32)
        for r0 in range(0, Q, RG):
            a = None
            for h in range(IDX_HEADS):
                w = jnp.concatenate([wb_s[h, r0:r0 + RG, :]] * NL, axis=1)
                t = jnp.maximum(lg_s[h * Q + r0:h * Q + r0 + RG, :], 0.0) * w
                a = t if a is None else a + t
            bits = pltpu.bitcast(a, I32)
            sgn = bits >> 31
            key = (bits ^ (sgn & 0x7FFFFFFF)) - sgn
            for l in range(NL):
                causal = off + l * LANES + lane_rg <= qb * Q + r0 + row_rg
                key_s[c * NL + l, r0:r0 + RG, :] = jnp.where(causal, key[:, l * LANES:(l + 1) * LANES], INT_MIN)
        return carry

    lax.fori_loop(0, n_chunks, score_chunk, 0)

    def bit_body(i, tau):
        cand = tau | lax.shift_left(jnp.int32(1), 31 - i)
        cand_s = cand ^ INT_MIN

        def cnt_body(c, cnt):
            for l in range(NL):
                cnt = cnt + jnp.where(key_s[c * NL + l] >= cand_s, 1, 0)
            return cnt

        cnt = lax.fori_loop(0, n_chunks, cnt_body, jnp.zeros((Q, LANES), I32))
        tot = jnp.sum(cnt.astype(F32), axis=1, keepdims=True)
        return jnp.where(tot >= topk, cand, tau)

    tau = lax.fori_loop(0, 32, bit_body, jnp.zeros((Q, LANES), I32))
    thr = jnp.maximum(tau ^ INT_MIN, INT_MIN + 1)

    def tie_cnt(c, carry):
        ge, gt = carry
        for l in range(NL):
            blk = key_s[c * NL + l]
            ge = ge + jnp.where(blk >= thr, 1, 0)
            gt = gt + jnp.where(blk > thr, 1, 0)
        return ge, gt

    zero = jnp.zeros((Q, LANES), I32)
    ge, gt = lax.fori_loop(0, n_chunks, tie_cnt, (zero, zero))
    n_ge = jnp.sum(ge.astype(F32), axis=1, keepdims=True)
    need = topk - jnp.sum(gt.astype(F32), axis=1, keepdims=True)
    excess = n_ge > topk
    cmax_s[...] = jnp.full((Q, LANES), 2 ** 31 - 1, I32)

    @pl.when(jnp.max(jnp.where(excess, 1.0, 0.0)) > 0.0)
    def _():
        def col_bit(i, y):
            cand = y | lax.shift_left(jnp.int32(1), col_bits - 1 - i)

            def body(c, cnt):
                for l in range(NL):
                    hit = (key_s[c * NL + l] == thr) & (c * KC + l * LANES + lane < cand)
                    cnt = cnt + jnp.where(hit, 1, 0)
                return cnt

            cnt = lax.fori_loop(0, n_chunks, body, zero)
            tot = jnp.sum(cnt.astype(F32), axis=1, keepdims=True)
            return jnp.where(tot < need, cand, y)

        y = lax.fori_loop(0, col_bits, col_bit, zero)
        cmax_s[...] = jnp.where(excess, y, 2 ** 31 - 1)

    cmax = cmax_s[...]
    m_s[...] = jnp.full(m_s.shape, NEG_BIG, F32)
    l_s[...] = jnp.zeros(l_s.shape, F32)
    acc_s[...] = jnp.zeros(acc_s.shape, F32)

    def att_chunk(c, carry):
        off = pl.multiple_of(c * KC, KC)
        ck = c_s[pl.ds(off, KC), :]
        lg_s[...] = lax.dot_general(qlat_s[...], ck, NT_DIMS, preferred_element_type=F32)
        for l in range(NL):
            blk = key_s[c * NL + l]
            sel = (blk > thr) | ((blk == thr) & (off + l * LANES + lane <= cmax))
            bias_s[l] = jnp.where(sel, 0.0, NEG_BIG)

        for h in range(H):
            for r0 in range(0, Q, RG):
                rows = slice(h * Q + r0, h * Q + r0 + RG)
                m_prev = m_s[rows, :]
                lg = lg_s[rows, :]
                lgs = [lg[:, l * LANES:(l + 1) * LANES] + bias_s[l, r0:r0 + RG, :] for l in range(NL)]
                mx = lgs[0]
                for l in range(1, NL):
                    mx = jnp.maximum(mx, lgs[l])
                m_new = jnp.maximum(m_prev, jnp.max(mx, axis=1, keepdims=True))
                alpha = jnp.exp2(m_prev - m_new)
                ps = [jnp.exp2(x - m_new) for x in lgs]
                p_s[rows, :] = jnp.concatenate(ps, axis=1).astype(BF16)
                psum = ps[0]
                for l in range(1, NL):
                    psum = psum + ps[l]
                l_s[rows, :] = alpha * l_s[rows, :] + psum
                m_s[rows, :] = m_new
                acc_s[rows, :] = acc_s[rows, :] * alpha
        acc_s[...] += jnp.dot(p_s[...], ck, preferred_element_type=F32)
        return carry

    lax.fori_loop(0, n_chunks, att_chunk, 0)

    o = acc_s[...] / jnp.sum(l_s[...], axis=1, keepdims=True)
    o_cat = jnp.concatenate([o[h * Q:(h + 1) * Q, :] for h in range(H)], axis=1).astype(BF16)
    o_ref[0] = jnp.dot(o_cat, wuv_ref[...], preferred_element_type=F32).astype(BF16)


def _dsa(proj3, idx3, kv_norm, wuk_bd, wuv_bd, Q=128, KC=512):
    B, L, _ = proj3.shape
    H = A_HEADS
    topk = min(TOPK_MAX, L // 4)
    k_hi, k_lo = _split(jnp.swapaxes(idx3[:, :, 256:256 + IDX_DIM], 1, 2))
    kit = jnp.concatenate([k_hi, k_hi, k_lo], axis=1)
    kern = functools.partial(_dsa_kernel, Q=Q, KC=KC, topk=topk, col_bits=max(1, (L - 1).bit_length()))
    return pl.pallas_call(
        kern,
        grid=(B, L // Q),
        in_specs=[pl.BlockSpec((1, Q, 512), lambda b, q: (b, q, 4)),
                  pl.BlockSpec((1, Q, 256), lambda b, q: (b, q, 0)),
                  pl.BlockSpec((1, Q, 128), lambda b, q: (b, q, 2)),
                  pl.BlockSpec((1, L, 128), lambda b, q: (b, 0, 20)),
                  pl.BlockSpec((1, 3 * IDX_DIM, L), lambda b, q: (b, 0, 0)),
                  pl.BlockSpec((1, A_LATENT), lambda b, q: (0, 0)),
                  pl.BlockSpec(wuk_bd.shape, lambda b, q: (0, 0)),
                  pl.BlockSpec(wuv_bd.shape, lambda b, q: (0, 0))],
        out_specs=pl.BlockSpec((1, Q, H * A_HEAD_DIM), lambda b, q: (b, q, 0)),
        out_shape=jax.ShapeDtypeStruct((B, L, H * A_HEAD_DIM), BF16),
        scratch_shapes=[pltpu.VMEM((L, A_LATENT), BF16),
                        pltpu.VMEM((L // LANES, Q, LANES), I32),
                        pltpu.VMEM((IDX_HEADS * Q, 3 * IDX_DIM), BF16),
                        pltpu.VMEM((H * Q, A_LATENT), BF16),
                        pltpu.VMEM((IDX_HEADS, Q, LANES), F32),
                        pltpu.VMEM((H * Q, KC), F32),
                        pltpu.VMEM((KC // LANES, Q, LANES), F32),
                        pltpu.VMEM((Q, LANES), I32),
                        pltpu.VMEM((H * Q, KC), BF16),
                        pltpu.VMEM((H * Q, LANES), F32),
                        pltpu.VMEM((H * Q, LANES), F32),
                        pltpu.VMEM((H * Q, A_LATENT), F32)],
        compiler_params=_cparams(("arbitrary", "arbitrary")),
        name="dsa",
    )(proj3, idx3, idx3, proj3, kit, kv_norm.reshape(1, A_LATENT), wuk_bd, wuv_bd)


def _ret_kernel(q_ref, k_ref, v_ref, g_ref, cos_ref, sin_ref, dec_ref, xi_ref, zeta_ref, gch_ref,
                o_ref, st_s, *, B):
    @pl.when(pl.program_id(0) == 0)
    def _():
        st_s[...] = jnp.zeros(st_s.shape, F32)

    C = R_CHUNK
    cos = cos_ref[...]
    sin = sin_ref[...]
    even = (lax.broadcasted_iota(I32, (C, R_DK), 1) & 1) == 0

    def rot(x):
        partner = jnp.where(even, pltpu.roll(x, R_DK - 1, 1), pltpu.roll(x, 1, 1))
        return x * cos + partner * sin

    for b in range(B):
        for h in range(R_HEADS):
            sl = slice(h * R_DK, (h + 1) * R_DK)
            q = rot(q_ref[b, :, sl])
            k = rot(k_ref[b, :, sl]) * (R_DK ** -0.5)
            vb = v_ref[b, :, sl].astype(BF16)
            qb = q.astype(BF16)
            s = lax.dot_general(qb, k.astype(BF16), NT_DIMS, preferred_element_type=F32) * dec_ref[h]
            inner = jnp.dot(s.astype(BF16), vb, preferred_element_type=F32)
            st = st_s[b, h]
            cross = jnp.dot(qb, st.astype(BF16), preferred_element_type=F32) * xi_ref[h]
            kz = (k * zeta_ref[h]).astype(BF16)
            st_s[b, h] = st * gch_ref[h] + jnp.dot(kz.T, vb, preferred_element_type=F32)
            out = inner + cross
            mu = jnp.mean(out, axis=-1, keepdims=True)
            d = out - mu
            var = jnp.mean(d * d, axis=-1, keepdims=True)
            y = d * lax.rsqrt(var + EPS)
            gate = g_ref[b, :, sl]
            o_ref[b, :, sl] = (gate * jax.nn.sigmoid(gate) * y).astype(BF16)


def _retention(proj3):
    B, L, _ = proj3.shape
    C, H = R_CHUNK, R_HEADS
    N = L // C
    d2 = R_DK // 2
    inv_freq = 1.0 / (10000.0 ** jnp.linspace(0.0, 1.0, d2, dtype=F32))
    ang = jnp.arange(L, dtype=F32)[:, None] * inv_freq[None, :]
    cos = jnp.repeat(jnp.cos(ang), 2, axis=1)
    sin = jnp.stack([-jnp.sin(ang), jnp.sin(ang)], axis=-1).reshape(L, R_DK)
    log_g = jnp.log1p(-jnp.exp2(-5.0 - jnp.arange(H, dtype=F32)))
    pos = jnp.arange(C, dtype=F32)
    diff = pos[:, None] - pos[None, :]
    decay = jnp.where(diff >= 0, jnp.exp(jnp.maximum(diff, 0.0)[None] * log_g[:, None, None]), 0.0)
    xi = jnp.broadcast_to(jnp.exp((pos + 1.0)[None, :] * log_g[:, None])[:, :, None], (H, C, R_DV))
    zeta = jnp.broadcast_to(jnp.exp((C - 1.0 - pos)[None, :] * log_g[:, None])[:, :, None], (H, C, R_DK))
    gch = jnp.broadcast_to(jnp.exp(C * log_g)[:, None, None], (H, 1, R_DV))
    W = H * R_DK
    colspec = lambda j: pl.BlockSpec((B, C, W), lambda n: (0, n, j))
    cst = lambda shape: pl.BlockSpec(shape, lambda n: (0,) * len(shape))
    return pl.pallas_call(
        functools.partial(_ret_kernel, B=B),
        grid=(N,),
        in_specs=[colspec(0), colspec(1), colspec(2), colspec(3),
                  pl.BlockSpec((C, R_DK), lambda n: (n, 0)),
                  pl.BlockSpec((C, R_DK), lambda n: (n, 0)),
                  cst((H, C, C)), cst((H, C, R_DV)), cst((H, C, R_DK)), cst((H, 1, R_DV))],
        out_specs=pl.BlockSpec((B, C, W), lambda n: (0, n, 0)),
        out_shape=jax.ShapeDtypeStruct((B, L, W), BF16),
        scratch_shapes=[pltpu.VMEM((B, H, R_DK, R_DV), F32)],
        compiler_params=_cparams(("arbitrary",)),
        name="retention",
    )(proj3, proj3, proj3, proj3, cos, sin, decay, xi, zeta, gch)


HALO = 32


def _conv_kernel(cur_ref, prev_ref, w_ref, b_ref, lg_ref, lb_ref, o_ref, ext_s, *, tl):
    i = pl.program_id(1)
    ext_s[HALO:, :] = cur_ref[0]

    @pl.when(i == 0)
    def _():
        ext_s[:HALO, :] = jnp.zeros((HALO, ext_s.shape[1]), F32)

    @pl.when(i > 0)
    def _():
        ext_s[:HALO, :] = prev_ref[0]

    base = HALO - (CONV_WIDTH - 1)
    y = None
    for j in range(CONV_WIDTH):
        t = ext_s[base + j:base + j + tl, :] * w_ref[j:j + 1, :]
        y = t if y is None else y + t
    y = y + b_ref[...]
    mu = jnp.mean(y, axis=-1, keepdims=True)
    d = y - mu
    var = jnp.mean(d * d, axis=-1, keepdims=True)
    z = d * lax.rsqrt(var + EPS) * lg_ref[...] + lb_ref[...]
    o_ref[0] = (z * jax.nn.sigmoid(z)).astype(BF16)


def _conv_ln(a3, w_dw, b_dw, ln_g, ln_b, tl=256):
    B, L, D = a3.shape
    r = tl // HALO
    row = lambda v: v.reshape(1, D)
    return pl.pallas_call(
        functools.partial(_conv_kernel, tl=tl),
        grid=(B, L // tl),
        in_specs=[pl.BlockSpec((1, tl, D), lambda b, i: (b, i, 0)),
                  pl.BlockSpec((1, HALO, D), lambda b, i: (b, jnp.maximum(i * r - 1, 0), 0)),
                  pl.BlockSpec((CONV_WIDTH, D), lambda b, i: (0, 0)),
                  pl.BlockSpec((1, D), lambda b, i: (0, 0)),
                  pl.BlockSpec((1, D), lambda b, i: (0, 0)),
                  pl.BlockSpec((1, D), lambda b, i: (0, 0))],
        out_specs=pl.BlockSpec((1, tl, D), lambda b, i: (b, i, 0)),
        out_shape=jax.ShapeDtypeStruct((B, L, D), BF16),
        scratch_shapes=[pltpu.VMEM((tl + HALO, D), F32)],
        compiler_params=_cparams(("parallel", "arbitrary")),
        name="conv_ln",
    )(a3, a3, w_dw, row(b_dw), row(ln_g), row(ln_b))


def _peer_score_kernel(x_ref, g_ref, whi_ref, wlo_ref, k3_ref, h_ref, st_ref, hi_s, lo_s):
    @pl.when(pl.program_id(1) == 0)
    def _():
        hi, lo = _split(_rms(x_ref[...], g_ref[...]))
        hi_s[...] = hi
        lo_s[...] = lo
        h_ref[...] = hi

    q = _dot3(hi_s[...], lo_s[...], whi_ref[...], wlo_ref[...])
    q_hi, q_lo = _split(q)
    half = P_QDIM // 2
    for s in range(q.shape[1] // half):
        cols = slice(s * half, (s + 1) * half)
        q3 = jnp.concatenate([q_hi[:, cols], q_lo[:, cols], q_hi[:, cols]], axis=1)
        st_ref[s // 2, s % 2] = lax.dot_general(k3_ref[s % 2], q3, NT_DIMS, preferred_element_type=F32)


def _peer_scores(x, g, wq, keys, tm=512, hps=2):
    T, D = x.shape
    whi, wlo = _split(wq)
    k_hi, k_lo = _split(keys)
    k3 = jnp.concatenate([k_hi, k_hi, k_lo], axis=-1)
    return pl.pallas_call(
        _peer_score_kernel,
        grid=(T // tm, P_HEADS // hps),
        in_specs=[pl.BlockSpec((tm, D), lambda i, j: (i, 0)),
                  pl.BlockSpec((1, D), lambda i, j: (0, 0)),
                  pl.BlockSpec((D, hps * P_QDIM), lambda i, j: (0, j)),
                  pl.BlockSpec((D, hps * P_QDIM), lambda i, j: (0, j)),
                  pl.BlockSpec(k3.shape, lambda i, j: (0, 0, 0))],
        out_specs=[pl.BlockSpec((tm, D), lambda i, j: (i, 0)),
                   pl.BlockSpec((hps, 2, P_NKEYS, tm), lambda i, j: (j, 0, 0, i))],
        out_shape=[jax.ShapeDtypeStruct((T, D), BF16),
                   jax.ShapeDtypeStruct((P_HEADS, 2, P_NKEYS, T), F32)],
        scratch_shapes=[pltpu.VMEM((tm, D), BF16), pltpu.VMEM((tm, D), BF16)],
        compiler_params=_cparams(("parallel", "arbitrary")),
        name="peer_scores",
    )(x, g.reshape(1, D), whi, wlo, k3)


def _take_top(s, n):
    rows = []
    for _ in range(n):
        mx = jnp.max(s, axis=0, keepdims=True)
        rows.append(mx)
        s = jnp.where(s == mx, -jnp.inf, s)
    return rows, s


def _take_top_ranked(s, n):
    rows = []
    rank = jnp.full(s.shape, float(n), F32)
    for r in range(n):
        mx = jnp.max(s, axis=0, keepdims=True)
        rows.append(mx)
        hit = s == mx
        rank = jnp.where(hit, float(r), rank)
        s = jnp.where(hit, -jnp.inf, s)
    return rows, rank


def _peer_topk_kernel(st_ref, rank2_ref, n1_ref, e2_ref, coef_ref):
    n = P_TOPK
    tt = st_ref.shape[-1]

    def head(h, carry):
        s1 = st_ref[h, 0]
        s2 = st_ref[h, 1]
        r1, rank1 = _take_top_ranked(s1, n)
        r2, rank2 = _take_top_ranked(s2, n)
        v2 = jnp.concatenate(r2, axis=0)
        blocks = [r1[0] + v2]
        rows8 = lax.broadcasted_iota(I32, (SUBLANES, tt), 0)
        for i in range(1, SUBLANES):
            blocks.append(jnp.where(rows8 < n // (i + 1), r1[i] + v2[:SUBLANES], -jnp.inf))
        blocks.append(jnp.concatenate(r1[SUBLANES:], axis=0) + r2[0])
        top, _ = _take_top(jnp.concatenate(blocks, axis=0), n)
        kth = top[-1]
        z = None
        for r in top:
            e = jnp.exp(r - top[0])
            z = e if z is None else z + e
        hits = [jnp.where(blk >= kth, 1.0, 0.0) for blk in blocks]
        per_rank = [jnp.sum(hb, axis=0, keepdims=True) for hb in hits[:SUBLANES]]
        per_rank += [hits[SUBLANES][k:k + 1] for k in range(n - SUBLANES)]
        n1 = jnp.zeros(s1.shape, F32)
        for i in range(n):
            n1 = jnp.where(rank1 == float(i), per_rank[i], n1)
        rank2_ref[h] = rank2.astype(BF16)
        n1_ref[h] = n1
        e2_ref[h] = jnp.exp(s2 - r2[0]).astype(BF16)
        coef_ref[h] = jnp.exp(s1 - r1[0]) * (1.0 / z)
        return carry

    lax.fori_loop(0, P_HEADS, head, 0)


def _peer_topk(st, tt=256):
    T = st.shape[-1]
    spec = pl.BlockSpec((P_HEADS, P_NKEYS, tt), lambda i: (0, 0, i))
    shape = lambda dt: jax.ShapeDtypeStruct((P_HEADS, P_NKEYS, T), dt)
    return pl.pallas_call(
        _peer_topk_kernel,
        grid=(T // tt,),
        in_specs=[pl.BlockSpec((P_HEADS, 2, P_NKEYS, tt), lambda i: (0, 0, 0, i))],
        out_specs=[spec, spec, spec, spec],
        out_shape=[shape(BF16), shape(F32), shape(BF16), shape(F32)],
        compiler_params=_cparams(("parallel",)),
        name="peer_topk",
    )(st)


TILE_ROWS = 32
PACKED_ROWS = 16


def _peer_dense_kernel(h_ref, u_ref, vt_ref, rank_ref, e2_ref, n1_ref, coef_ref, res_ref, fg_ref, o_ref,
                       bn_s, bcoef_s, hb_s, act0_s, act1_s, acc_s, *, te, tt, final):
    j = pl.program_id(1)
    NK = P_NKEYS

    n_tiles = pl.num_programs(1) - 1

    @pl.when(j == 0)
    def _():
        acc_s[...] = jnp.zeros(acc_s.shape, F32)
        act0_s[...] = jnp.zeros(act0_s.shape, BF16)

    def step(act_prev, act_next):
        acc_s[...] += jnp.dot(vt_ref[...], act_prev[...], preferred_element_type=F32)
        hb_s[...] = lax.dot_general(u_ref[...], h_ref[...], NT_DIMS, preferred_element_type=F32)

        a0 = pl.multiple_of(jnp.minimum(j, n_tiles - 1) * SUBLANES, SUBLANES)
        reps = TILE_ROWS // PACKED_ROWS
        for h in range(P_HEADS):
            n8 = n1_ref[h, pl.ds(a0, SUBLANES), :]
            coef8 = coef_ref[h, pl.ds(a0, SUBLANES), :]
            for al in range(SUBLANES):
                bn_s[h, al] = jnp.broadcast_to(n8[al:al + 1], (PACKED_ROWS, tt)).astype(BF16)
                bcoef_s[h, al] = jnp.broadcast_to(coef8[al:al + 1], (PACKED_ROWS, tt)).astype(BF16)
        for al in range(SUBLANES):
            for b0 in range(0, NK, TILE_ROWS):
                sub = slice(b0, b0 + TILE_ROWS)
                g = None
                for h in range(P_HEADS):
                    pairs = jnp.concatenate([bn_s[h, al]] * reps, axis=0)
                    coef = jnp.concatenate([bcoef_s[h, al]] * reps, axis=0)
                    e2 = e2_ref[h, sub, :]
                    t = jnp.where(rank_ref[h, sub, :] < pairs, e2, jnp.zeros_like(e2)) * coef
                    g = t if g is None else g + t
                rows = slice(al * NK + b0, al * NK + b0 + TILE_ROWS)
                hb = hb_s[rows, :]
                gelu = 0.5 * hb * (1.0 + lax.erf(hb * (2.0 ** -0.5)))
                act_next[rows, :] = gelu.astype(BF16) * g

    @pl.when(j % 2 == 0)
    def _():
        step(act0_s, act1_s)

    @pl.when(j % 2 == 1)
    def _():
        step(act1_s, act0_s)

    @pl.when(j == n_tiles)
    def _():
        y = res_ref[...] + acc_s[...].T
        if final:
            y = _rms(y, fg_ref[...])
        o_ref[...] = y


def _peer_dense(hn, u, vt, sel, res, fgain, final, tt=512):
    T, D = res.shape
    E = u.shape[0]
    te = SUBLANES * P_NKEYS
    n_tiles = E // te
    kern = functools.partial(_peer_dense_kernel, te=te, tt=tt, final=final)
    per_token = pl.BlockSpec((P_HEADS, P_NKEYS, tt), lambda i, j: (0, 0, i))
    return pl.pallas_call(
        kern,
        grid=(T // tt, n_tiles + 1),
        in_specs=[pl.BlockSpec((tt, D), lambda i, j: (i, 0)),
                  pl.BlockSpec((te, D), lambda i, j: (jnp.minimum(j, n_tiles - 1), 0)),
                  pl.BlockSpec((D, te), lambda i, j: (0, jnp.maximum(j - 1, 0))),
                  per_token, per_token, per_token, per_token,
                  pl.BlockSpec((tt, D), lambda i, j: (i, 0)),
                  pl.BlockSpec((1, D), lambda i, j: (0, 0))],
        out_specs=pl.BlockSpec((tt, D), lambda i, j: (i, 0)),
        out_shape=jax.ShapeDtypeStruct((T, D), F32),
        scratch_shapes=[pltpu.VMEM((P_HEADS, SUBLANES, PACKED_ROWS, tt), BF16),
                        pltpu.VMEM((P_HEADS, SUBLANES, PACKED_ROWS, tt), BF16),
                        pltpu.VMEM((te, tt), F32),
                        pltpu.VMEM((te, tt), BF16),
                        pltpu.VMEM((te, tt), BF16),
                        pltpu.VMEM((D, tt), F32)],
        compiler_params=_cparams(("parallel", "arbitrary")),
        name="peer_dense",
    )(hn, u, vt, *sel, res, fgain.reshape(1, D))


def _peer(x, g, wq, keys, u, v, fgain, final):
    hn, st = _peer_scores(x, g, wq, keys)
    rank2, n1, e2, coef = _peer_topk(st)
    return _peer_dense(hn, u.astype(BF16), v.T.astype(BF16), (rank2, e2, n1, coef), x, fgain, final)


def _block_diag(blocks):
    H, r, c = blocks.shape
    eye = jnp.eye(H, dtype=blocks.dtype)
    return (eye[:, None, :, None] * blocks[:, :, None, :]).reshape(H * r, H * c)


def _in_proj_weights(w_in):
    sizes = (A_HEADS * A_HEAD_DIM, A_LATENT, IDX_HEADS * IDX_DIM, IDX_DIM, IDX_HEADS,
             R_HEADS * R_DK, R_HEADS * R_DK, R_HEADS * R_DV, R_HEADS * R_DV)
    pts = np.cumsum(sizes)[:-1].tolist()
    qa, ckv, qi, ki, wi, rq, rk, rv, rg = jnp.split(w_in, pts, axis=1)
    pad = jnp.zeros((w_in.shape[0], LANES - IDX_DIM - IDX_HEADS), w_in.dtype)
    return (jnp.concatenate([rq, rk, rv, rg, qa, ckv], axis=1),
            jnp.concatenate([qi, ki, wi, pad], axis=1))


def kernel(x, mix_norm_e, w_in, kv_norm, w_uk, w_uv, w_o, mix_norm_o, conv_w1, conv_b1, conv_dw, conv_dw_b, conv_ln_g, conv_ln_b, conv_w2, conv_b2, ffn_norm, peer_wq, peer_keys, peer_u, peer_v, final_norm):
    B, L, D = x.shape
    T = B * L
    depth = ffn_norm.shape[0]
    xf = x.reshape(T, D)
    for layer in range(depth):
        j = layer // 2
        if layer % 2 == 0:
            proj, idx = _in_proj(xf, mix_norm_e[j], *_in_proj_weights(w_in[j]))
            proj3 = proj.reshape(B, L, proj.shape[1])
            idx3 = idx.reshape(B, L, idx.shape[1])
            wuk_bd = _block_diag(jnp.swapaxes(w_uk[j], 1, 2)).astype(BF16)
            wuv_bd = _block_diag(w_uv[j]).astype(BF16)
            a_out = _dsa(proj3, idx3, kv_norm[j], wuk_bd, wuv_bd)
            b_out = _retention(proj3)
            wo = w_o[j].astype(BF16)
            na = A_HEADS * A_HEAD_DIM
            xf = _mm_res([a_out.reshape(T, na), b_out.reshape(T, -1)], [wo[:na], wo[na:]], xf)
        else:
            a = _norm_glu(xf, mix_norm_o[j], conv_w1[j].astype(BF16), conv_b1[j])
            y = _conv_ln(a.reshape(B, L, D), conv_dw[j], conv_dw_b[j], conv_ln_g[j], conv_ln_b[j])
            xf = _mm_res([y.reshape(T, D)], [conv_w2[j].astype(BF16)], xf, bias=conv_b2[j])
        xf = _peer(xf, ffn_norm[layer], peer_wq[layer], peer_keys[layer], peer_u[layer], peer_v[layer],
                   final_norm, final=(layer == depth - 1))
    return xf.reshape(B, L, D)
```

```python
import functools

import jax
import jax.numpy as jnp
import numpy as np
from jax import lax
from jax.experimental import pallas as pl
from jax.experimental.pallas import tpu as pltpu

F32 = jnp.float32
BF16 = jnp.bfloat16
I32 = jnp.int32

EPS = 1e-6
A_HEADS, A_HEAD_DIM, A_LATENT = 8, 64, 128
IDX_HEADS, IDX_DIM, TOPK_MAX = 8, 32, 256
R_HEADS, R_DK, R_DV, R_CHUNK = 4, 128, 128, 128
CONV_WIDTH = 31
P_HEADS, P_NKEYS, P_QDIM, P_TOPK = 8, 128, 256, 16

LANES = 128
SUBLANES = 8
VMEM_LIMIT = 56 * 1024 * 1024
INT_MIN = -(2 ** 31)
NEG_BIG = -1e30
LOG2_E = 1.4426950408889634

NT_DIMS = (((1,), (1,)), ((), ()))


def _cparams(sem, flags=None):
    return pltpu.CompilerParams(dimension_semantics=sem, vmem_limit_bytes=VMEM_LIMIT, flags=flags)


def _aligned(x, m):
    if isinstance(x, int):
        assert x % m == 0
        return x
    return pl.multiple_of(x, m)


def _rms(x, g):
    return x * lax.rsqrt(jnp.mean(x * x, axis=-1, keepdims=True) + EPS) * g


def _split(x):
    hi = x.astype(BF16)
    return hi, (x - hi.astype(F32)).astype(BF16)


def _dot3(a_hi, a_lo, b_hi, b_lo):
    d = functools.partial(jnp.dot, preferred_element_type=F32)
    return d(a_hi, b_hi) + (d(a_hi, b_lo) + d(a_lo, b_hi))


def _in_proj_kernel(x_ref, g_ref, w_ref, whi_ref, wlo_ref, o_ref, oi_ref):
    h = _rms(x_ref[...], g_ref[...])
    hi, lo = _split(h)
    o_ref[...] = jnp.dot(hi, w_ref[...], preferred_element_type=F32)
    oi_ref[...] = _dot3(hi, lo, whi_ref[...], wlo_ref[...])


def _in_proj(x, g, w_main, w_idx, tm=512):
    T, D = x.shape
    N, NI = w_main.shape[1], w_idx.shape[1]
    whi, wlo = _split(w_idx)
    return pl.pallas_call(
        _in_proj_kernel,
        grid=(T // tm,),
        in_specs=[pl.BlockSpec((tm, D), lambda i: (i, 0)),
                  pl.BlockSpec((1, D), lambda i: (0, 0)),
                  pl.BlockSpec((D, N), lambda i: (0, 0)),
                  pl.BlockSpec((D, NI), lambda i: (0, 0)),
                  pl.BlockSpec((D, NI), lambda i: (0, 0))],
        out_specs=[pl.BlockSpec((tm, N), lambda i: (i, 0)),
                   pl.BlockSpec((tm, NI), lambda i: (i, 0))],
        out_shape=[jax.ShapeDtypeStruct((T, N), F32), jax.ShapeDtypeStruct((T, NI), F32)],
        compiler_params=_cparams(("parallel",)),
        name="in_proj",
    )(x, g.reshape(1, D), w_main.astype(BF16), whi, wlo)


def _norm_glu_kernel(x_ref, g_ref, wa_ref, wg_ref, ba_ref, bg_ref, o_ref, h_scr):
    @pl.when(pl.program_id(1) == 0)
    def _():
        h_scr[...] = _rms(x_ref[...], g_ref[...]).astype(BF16)

    h = h_scr[...]
    a = jnp.dot(h, wa_ref[...], preferred_element_type=F32) + ba_ref[...]
    gate = jnp.dot(h, wg_ref[...], preferred_element_type=F32) + bg_ref[...]
    o_ref[...] = a * jax.nn.sigmoid(gate)


def _norm_glu(x, g, w1, b1, tm=512, tn=512):
    T, D = x.shape
    nj = D // tn
    b1 = b1.reshape(1, 2 * D)
    return pl.pallas_call(
        _norm_glu_kernel,
        grid=(T // tm, nj),
        in_specs=[pl.BlockSpec((tm, D), lambda i, j: (i, 0)),
                  pl.BlockSpec((1, D), lambda i, j: (0, 0)),
                  pl.BlockSpec((D, tn), lambda i, j: (0, j)),
                  pl.BlockSpec((D, tn), lambda i, j: (0, j + nj)),
                  pl.BlockSpec((1, tn), lambda i, j: (0, j)),
                  pl.BlockSpec((1, tn), lambda i, j: (0, j + nj))],
        out_specs=pl.BlockSpec((tm, tn), lambda i, j: (i, j)),
        out_shape=jax.ShapeDtypeStruct((T, D), F32),
        scratch_shapes=[pltpu.VMEM((tm, D), BF16)],
        compiler_params=_cparams(("parallel", "arbitrary")),
        name="norm_glu",
    )(x, g.reshape(1, D), w1, w1, b1, b1)


def _mm_res_kernel(*refs, n_lhs, has_bias):
    lhs = refs[:n_lhs]
    ws = refs[n_lhs:2 * n_lhs]
    rest = refs[2 * n_lhs:]
    if has_bias:
        b_ref, res_ref, o_ref = rest
        acc = res_ref[...] + b_ref[...]
    else:
        res_ref, o_ref = rest
        acc = res_ref[...]
    for a_ref, w_ref in zip(lhs, ws):
        acc = acc + jnp.dot(a_ref[...], w_ref[...], preferred_element_type=F32)
    o_ref[...] = acc


def _mm_res(lhs, ws, res, bias=None, tm=512):
    T, N = res.shape
    n = len(lhs)
    in_specs = [pl.BlockSpec((tm, a.shape[1]), lambda i: (i, 0)) for a in lhs]
    in_specs += [pl.BlockSpec(w.shape, lambda i: (0, 0)) for w in ws]
    args = list(lhs) + list(ws)
    if bias is not None:
        in_specs.append(pl.BlockSpec((1, N), lambda i: (0, 0)))
        args.append(bias.reshape(1, N))
    in_specs.append(pl.BlockSpec((tm, N), lambda i: (i, 0)))
    args.append(res)
    return pl.pallas_call(
        functools.partial(_mm_res_kernel, n_lhs=n, has_bias=bias is not None),
        grid=(T // tm,),
        in_specs=in_specs,
        out_specs=pl.BlockSpec((tm, N), lambda i: (i, 0)),
        out_shape=jax.ShapeDtypeStruct((T, N), F32),
        compiler_params=_cparams(("parallel",)),
        name="mm_res",
    )(*args)


def _dsa_kernel(qa_ref, qi_ref, misc_ref, ckv_ref, kit_ref, kvn_ref, wuk_ref, wuv_ref, o_ref,
                c_s, key_s, qst_s, qlat_s, wb_s, lg_s, bias_s, cmax_s, p_s, m_s, l_s, acc_s,
                *, Q, KC, topk, col_bits):
    qb = pl.program_id(1)
    H = A_HEADS
    NL = KC // LANES

    @pl.when(qb == 0)
    def _():
        c_s[...] = _rms(ckv_ref[0], kvn_ref[...]).astype(BF16)

    qi = qi_ref[0]
    wi = misc_ref[0][:, IDX_DIM:IDX_DIM + IDX_HEADS] * (IDX_HEADS ** -0.5)
    for h in range(IDX_HEADS):
        hi, lo = _split(qi[:, h * IDX_DIM:(h + 1) * IDX_DIM])
        qst_s[h * Q:(h + 1) * Q, :] = jnp.concatenate([hi, lo, hi], axis=1)
        wb_s[h] = jnp.broadcast_to(wi[:, h:h + 1], (Q, LANES))
    ql = jnp.dot(qa_ref[0].astype(BF16), wuk_ref[...], preferred_element_type=F32) * (A_HEAD_DIM ** -0.5 * LOG2_E)
    for h in range(H):
        qlat_s[h * Q:(h + 1) * Q, :] = ql[:, h * A_LATENT:(h + 1) * A_LATENT].astype(BF16)

    n_chunks = ((qb + 1) * Q + KC - 1) // KC
    q_pos = qb * Q + lax.broadcasted_iota(I32, (Q, LANES), 0)
    lane = lax.broadcasted_iota(I32, (Q, LANES), 1)

    RG = 4 * SUBLANES
    lane_rg = lax.broadcasted_iota(I32, (RG, LANES), 1)
    row_rg = lax.broadcasted_iota(I32, (RG, LANES), 0)

    def score_chunk(c, carry):
        off = pl.multiple_of(c * KC, KC)
        lg_s[...] = jnp.dot(qst_s[...], kit_ref[0, :, pl.ds(off, KC)], preferred_element_type=F32)
        for r0 in range(0, Q, RG):
            a = None
            for h in range(IDX_HEADS):
                w = jnp.concatenate([wb_s[h, r0:r0 + RG, :]] * NL, axis=1)
                t = jnp.maximum(lg_s[h * Q + r0:h * Q + r0 + RG, :], 0.0) * w
                a = t if a is None else a + t
            bits = pltpu.bitcast(a, I32)
            sgn = bits >> 31
            key = (bits ^ (sgn & 0x7FFFFFFF)) - sgn
            for l in range(NL):
                causal = off + l * LANES + lane_rg <= qb * Q + r0 + row_rg
                key_s[c * NL + l, r0:r0 + RG, :] = jnp.where(causal, key[:, l * LANES:(l + 1) * LANES], INT_MIN)
        return carry

    lax.fori_loop(0, n_chunks, score_chunk, 0)

    SR = min(Q, 16 * SUBLANES)
    lane_sr = lax.broadcasted_iota(I32, (SR, LANES), 1)

    def row_count(pred):
        cnts = []
        for r0 in range(0, Q, SR):
            def body(c, cnt, r0=r0):
                for l in range(NL):
                    blk = key_s[c * NL + l, r0:r0 + SR, :]
                    cnt = cnt + jnp.where(pred(blk, r0, c * KC + l * LANES), 1, 0)
                return cnt

            cnts.append(lax.fori_loop(0, n_chunks, body, jnp.zeros((SR, LANES), I32)))
        return jnp.sum(jnp.concatenate(cnts, axis=0).astype(F32), axis=1, keepdims=True)

    def bit_body(i, tau):
        cand = tau | lax.shift_left(jnp.int32(1), 31 - i)
        cand_s = cand ^ INT_MIN
        tot = row_count(lambda blk, r0, col0: blk >= cand_s[r0:r0 + SR])
        return jnp.where(tot >= topk, cand, tau)

    zero = jnp.zeros((Q, LANES), I32)
    tau = lax.fori_loop(0, 32, bit_body, zero)
    thr = jnp.maximum(tau ^ INT_MIN, INT_MIN + 1)

    n_ge = row_count(lambda blk, r0, col0: blk >= thr[r0:r0 + SR])
    need = topk - row_count(lambda blk, r0, col0: blk > thr[r0:r0 + SR])
    excess = n_ge > topk
    cmax_s[...] = jnp.full((Q, LANES), 2 ** 31 - 1, I32)

    @pl.when(jnp.max(jnp.where(excess, 1.0, 0.0)) > 0.0)
    def _():
        def col_bit(i, y):
            cand = y | lax.shift_left(jnp.int32(1), col_bits - 1 - i)
            tot = row_count(lambda blk, r0, col0:
                            (blk == thr[r0:r0 + SR]) & (col0 + lane_sr < cand[r0:r0 + SR]))
            return jnp.where(tot < need, cand, y)

        y = lax.fori_loop(0, col_bits, col_bit, zero)
        cmax_s[...] = jnp.where(excess, y, 2 ** 31 - 1)

    cmax = cmax_s[...]
    m_s[...] = jnp.full(m_s.shape, NEG_BIG, F32)
    l_s[...] = jnp.zeros(l_s.shape, F32)
    acc_s[...] = jnp.zeros(acc_s.shape, F32)

    def att_chunk(c, carry):
        off = pl.multiple_of(c * KC, KC)
        ck = c_s[pl.ds(off, KC), :]
        lg_s[...] = lax.dot_general(qlat_s[...], ck, NT_DIMS, preferred_element_type=F32)
        for l in range(NL):
            blk = key_s[c * NL + l]
            sel = (blk > thr) | ((blk == thr) & (off + l * LANES + lane <= cmax))
            bias_s[l] = jnp.where(sel, 0.0, NEG_BIG)

        for h in range(H):
            for r0 in range(0, Q, RG):
                rows = slice(h * Q + r0, h * Q + r0 + RG)
                m_prev = m_s[rows, :]
                lg = lg_s[rows, :]
                lgs = [lg[:, l * LANES:(l + 1) * LANES] + bias_s[l, r0:r0 + RG, :] for l in range(NL)]
                mx = lgs[0]
                for l in range(1, NL):
                    mx = jnp.maximum(mx, lgs[l])
                m_new = jnp.maximum(m_prev, jnp.max(mx, axis=1, keepdims=True))
                alpha = jnp.exp2(m_prev - m_new)
                ps = [jnp.exp2(x - m_new) for x in lgs]
                p_s[rows, :] = jnp.concatenate(ps, axis=1).astype(BF16)
                psum = ps[0]
                for l in range(1, NL):
                    psum = psum + ps[l]
                l_s[rows, :] = alpha * l_s[rows, :] + psum
                m_s[rows, :] = m_new
                acc_s[rows, :] = acc_s[rows, :] * alpha
        acc_s[...] += jnp.dot(p_s[...], ck, preferred_element_type=F32)
        return carry

    lax.fori_loop(0, n_chunks, att_chunk, 0)

    o = acc_s[...] / jnp.sum(l_s[...], axis=1, keepdims=True)
    o_cat = jnp.concatenate([o[h * Q:(h + 1) * Q, :] for h in range(H)], axis=1).astype(BF16)
    o_ref[0] = jnp.dot(o_cat, wuv_ref[...], preferred_element_type=F32).astype(BF16)


def _dsa(proj3, idx3, kv_norm, wuk_bd, wuv_bd, Q=256, KC=512):
    B, L, _ = proj3.shape
    H = A_HEADS
    topk = min(TOPK_MAX, L // 4)
    k_hi, k_lo = _split(jnp.swapaxes(idx3[:, :, 256:256 + IDX_DIM], 1, 2))
    kit = jnp.concatenate([k_hi, k_hi, k_lo], axis=1)
    kern = functools.partial(_dsa_kernel, Q=Q, KC=KC, topk=topk, col_bits=max(1, (L - 1).bit_length()))
    return pl.pallas_call(
        kern,
        grid=(B, L // Q),
        in_specs=[pl.BlockSpec((1, Q, 512), lambda b, q: (b, q, 4)),
                  pl.BlockSpec((1, Q, 256), lambda b, q: (b, q, 0)),
                  pl.BlockSpec((1, Q, 128), lambda b, q: (b, q, 2)),
                  pl.BlockSpec((1, L, 128), lambda b, q: (b, 0, 20)),
                  pl.BlockSpec((1, 3 * IDX_DIM, L), lambda b, q: (b, 0, 0)),
                  pl.BlockSpec((1, A_LATENT), lambda b, q: (0, 0)),
                  pl.BlockSpec(wuk_bd.shape, lambda b, q: (0, 0)),
                  pl.BlockSpec(wuv_bd.shape, lambda b, q: (0, 0))],
        out_specs=pl.BlockSpec((1, Q, H * A_HEAD_DIM), lambda b, q: (b, q, 0)),
        out_shape=jax.ShapeDtypeStruct((B, L, H * A_HEAD_DIM), BF16),
        scratch_shapes=[pltpu.VMEM((L, A_LATENT), BF16),
                        pltpu.VMEM((L // LANES, Q, LANES), I32),
                        pltpu.VMEM((IDX_HEADS * Q, 3 * IDX_DIM), BF16),
                        pltpu.VMEM((H * Q, A_LATENT), BF16),
                        pltpu.VMEM((IDX_HEADS, Q, LANES), F32),
                        pltpu.VMEM((H * Q, KC), F32),
                        pltpu.VMEM((KC // LANES, Q, LANES), F32),
                        pltpu.VMEM((Q, LANES), I32),
                        pltpu.VMEM((H * Q, KC), BF16),
                        pltpu.VMEM((H * Q, LANES), F32),
                        pltpu.VMEM((H * Q, LANES), F32),
                        pltpu.VMEM((H * Q, A_LATENT), F32)],
        compiler_params=_cparams(("arbitrary", "arbitrary")),
        name="dsa",
    )(proj3, idx3, idx3, proj3, kit, kv_norm.reshape(1, A_LATENT), wuk_bd, wuv_bd)


def _ret_kernel(q_ref, k_ref, v_ref, g_ref, cos_ref, sin_ref, dec_ref, xi_ref, zeta_ref, gch_ref,
                o_ref, st_s, *, B):
    @pl.when(pl.program_id(0) == 0)
    def _():
        st_s[...] = jnp.zeros(st_s.shape, F32)

    C = R_CHUNK
    cos = cos_ref[...]
    sin = sin_ref[...]
    even = (lax.broadcasted_iota(I32, (C, R_DK), 1) & 1) == 0

    def rot(x):
        partner = jnp.where(even, pltpu.roll(x, R_DK - 1, 1), pltpu.roll(x, 1, 1))
        return x * cos + partner * sin

    for b in range(B):
        for h in range(R_HEADS):
            sl = slice(h * R_DK, (h + 1) * R_DK)
            q = rot(q_ref[b, :, sl])
            k = rot(k_ref[b, :, sl]) * (R_DK ** -0.5)
            vb = v_ref[b, :, sl].astype(BF16)
            qb = q.astype(BF16)
            s = lax.dot_general(qb, k.astype(BF16), NT_DIMS, preferred_element_type=F32) * dec_ref[h]
            inner = jnp.dot(s.astype(BF16), vb, preferred_element_type=F32)
            st = st_s[b, h]
            cross = jnp.dot(qb, st.astype(BF16), preferred_element_type=F32) * xi_ref[h]
            kz = (k * zeta_ref[h]).astype(BF16)
            st_s[b, h] = st * gch_ref[h] + jnp.dot(kz.T, vb, preferred_element_type=F32)
            out = inner + cross
            mu = jnp.mean(out, axis=-1, keepdims=True)
            d = out - mu
            var = jnp.mean(d * d, axis=-1, keepdims=True)
            y = d * lax.rsqrt(var + EPS)
            gate = g_ref[b, :, sl]
            o_ref[b, :, sl] = (gate * jax.nn.sigmoid(gate) * y).astype(BF16)


def _retention(proj3):
    B, L, _ = proj3.shape
    C, H = R_CHUNK, R_HEADS
    N = L // C
    d2 = R_DK // 2
    inv_freq = 1.0 / (10000.0 ** jnp.linspace(0.0, 1.0, d2, dtype=F32))
    ang = jnp.arange(L, dtype=F32)[:, None] * inv_freq[None, :]
    cos = jnp.repeat(jnp.cos(ang), 2, axis=1)
    sin = jnp.stack([-jnp.sin(ang), jnp.sin(ang)], axis=-1).reshape(L, R_DK)
    log_g = jnp.log1p(-jnp.exp2(-5.0 - jnp.arange(H, dtype=F32)))
    pos = jnp.arange(C, dtype=F32)
    diff = pos[:, None] - pos[None, :]
    decay = jnp.where(diff >= 0, jnp.exp(jnp.maximum(diff, 0.0)[None] * log_g[:, None, None]), 0.0)
    xi = jnp.broadcast_to(jnp.exp((pos + 1.0)[None, :] * log_g[:, None])[:, :, None], (H, C, R_DV))
    zeta = jnp.broadcast_to(jnp.exp((C - 1.0 - pos)[None, :] * log_g[:, None])[:, :, None], (H, C, R_DK))
    gch = jnp.broadcast_to(jnp.exp(C * log_g)[:, None, None], (H, 1, R_DV))
    W = H * R_DK
    colspec = lambda j: pl.BlockSpec((B, C, W), lambda n: (0, n, j))
    cst = lambda shape: pl.BlockSpec(shape, lambda n: (0,) * len(shape))
    return pl.pallas_call(
        functools.partial(_ret_kernel, B=B),
        grid=(N,),
        in_specs=[colspec(0), colspec(1), colspec(2), colspec(3),
                  pl.BlockSpec((C, R_DK), lambda n: (n, 0)),
                  pl.BlockSpec((C, R_DK), lambda n: (n, 0)),
                  cst((H, C, C)), cst((H, C, R_DV)), cst((H, C, R_DK)), cst((H, 1, R_DV))],
        out_specs=pl.BlockSpec((B, C, W), lambda n: (0, n, 0)),
        out_shape=jax.ShapeDtypeStruct((B, L, W), BF16),
        scratch_shapes=[pltpu.VMEM((B, H, R_DK, R_DV), F32)],
        compiler_params=_cparams(("arbitrary",)),
        name="retention",
    )(proj3, proj3, proj3, proj3, cos, sin, decay, xi, zeta, gch)


HALO = 32


def _conv_kernel(cur_ref, prev_ref, w_ref, b_ref, lg_ref, lb_ref, o_ref, ext_s, *, tl):
    i = pl.program_id(1)
    ext_s[HALO:, :] = cur_ref[0]

    @pl.when(i == 0)
    def _():
        ext_s[:HALO, :] = jnp.zeros((HALO, ext_s.shape[1]), F32)

    @pl.when(i > 0)
    def _():
        ext_s[:HALO, :] = prev_ref[0]

    base = HALO - (CONV_WIDTH - 1)
    y = None
    for j in range(CONV_WIDTH):
        t = ext_s[base + j:base + j + tl, :] * w_ref[j:j + 1, :]
        y = t if y is None else y + t
    y = y + b_ref[...]
    mu = jnp.mean(y, axis=-1, keepdims=True)
    d = y - mu
    var = jnp.mean(d * d, axis=-1, keepdims=True)
    z = d * lax.rsqrt(var + EPS) * lg_ref[...] + lb_ref[...]
    o_ref[0] = (z * jax.nn.sigmoid(z)).astype(BF16)


def _conv_ln(a3, w_dw, b_dw, ln_g, ln_b, tl=256):
    B, L, D = a3.shape
    r = tl // HALO
    row = lambda v: v.reshape(1, D)
    return pl.pallas_call(
        functools.partial(_conv_kernel, tl=tl),
        grid=(B, L // tl),
        in_specs=[pl.BlockSpec((1, tl, D), lambda b, i: (b, i, 0)),
                  pl.BlockSpec((1, HALO, D), lambda b, i: (b, jnp.maximum(i * r - 1, 0), 0)),
                  pl.BlockSpec((CONV_WIDTH, D), lambda b, i: (0, 0)),
                  pl.BlockSpec((1, D), lambda b, i: (0, 0)),
                  pl.BlockSpec((1, D), lambda b, i: (0, 0)),
                  pl.BlockSpec((1, D), lambda b, i: (0, 0))],
        out_specs=pl.BlockSpec((1, tl, D), lambda b, i: (b, i, 0)),
        out_shape=jax.ShapeDtypeStruct((B, L, D), BF16),
        scratch_shapes=[pltpu.VMEM((tl + HALO, D), F32)],
        compiler_params=_cparams(("parallel", "arbitrary")),
        name="conv_ln",
    )(a3, a3, w_dw, row(b_dw), row(ln_g), row(ln_b))


def _peer_score_kernel(x_ref, g_ref, whi_ref, wlo_ref, k3_ref, h_ref, st_ref, hi_s, lo_s):
    @pl.when(pl.program_id(1) == 0)
    def _():
        hi, lo = _split(_rms(x_ref[...], g_ref[...]))
        hi_s[...] = hi
        lo_s[...] = lo
        h_ref[...] = hi

    q = _dot3(hi_s[...], lo_s[...], whi_ref[...], wlo_ref[...])
    q_hi, q_lo = _split(q)
    half = P_QDIM // 2
    for s in range(q.shape[1] // half):
        cols = slice(s * half, (s + 1) * half)
        q3 = jnp.concatenate([q_hi[:, cols], q_lo[:, cols], q_hi[:, cols]], axis=1)
        st_ref[s // 2, s % 2] = lax.dot_general(k3_ref[s % 2], q3, NT_DIMS, preferred_element_type=F32)


def _peer_scores(x, g, wq, keys, tm=512, hps=2):
    T, D = x.shape
    whi, wlo = _split(wq)
    k_hi, k_lo = _split(keys)
    k3 = jnp.concatenate([k_hi, k_hi, k_lo], axis=-1)
    return pl.pallas_call(
        _peer_score_kernel,
        grid=(T // tm, P_HEADS // hps),
        in_specs=[pl.BlockSpec((tm, D), lambda i, j: (i, 0)),
                  pl.BlockSpec((1, D), lambda i, j: (0, 0)),
                  pl.BlockSpec((D, hps * P_QDIM), lambda i, j: (0, j)),
                  pl.BlockSpec((D, hps * P_QDIM), lambda i, j: (0, j)),
                  pl.BlockSpec(k3.shape, lambda i, j: (0, 0, 0))],
        out_specs=[pl.BlockSpec((tm, D), lambda i, j: (i, 0)),
                   pl.BlockSpec((hps, 2, P_NKEYS, tm), lambda i, j: (j, 0, 0, i))],
        out_shape=[jax.ShapeDtypeStruct((T, D), BF16),
                   jax.ShapeDtypeStruct((P_HEADS, 2, P_NKEYS, T), F32)],
        scratch_shapes=[pltpu.VMEM((tm, D), BF16), pltpu.VMEM((tm, D), BF16)],
        compiler_params=_cparams(("parallel", "arbitrary")),
        name="peer_scores",
    )(x, g.reshape(1, D), whi, wlo, k3)


def _take_top(s, n):
    rows = []
    for _ in range(n):
        mx = jnp.max(s, axis=0, keepdims=True)
        rows.append(mx)
        s = jnp.where(s == mx, -jnp.inf, s)
    return rows, s


def _take_top_ranked(s, n):
    rows = []
    rank = jnp.full(s.shape, float(n), F32)
    for r in range(n):
        mx = jnp.max(s, axis=0, keepdims=True)
        rows.append(mx)
        hit = s == mx
        rank = jnp.where(hit, float(r), rank)
        s = jnp.where(hit, -jnp.inf, s)
    return rows, rank


def _peer_topk_kernel(st_ref, rank2_ref, n1_ref, e2_ref, coef_ref):
    n = P_TOPK
    tt = st_ref.shape[-1]

    def head(h, carry):
        s1 = st_ref[h, 0]
        s2 = st_ref[h, 1]
        r1, rank1 = _take_top_ranked(s1, n)
        r2, rank2 = _take_top_ranked(s2, n)
        v2 = jnp.concatenate(r2, axis=0)
        blocks = [r1[0] + v2]
        rows8 = lax.broadcasted_iota(I32, (SUBLANES, tt), 0)
        for i in range(1, SUBLANES):
            blocks.append(jnp.where(rows8 < n // (i + 1), r1[i] + v2[:SUBLANES], -jnp.inf))
        blocks.append(jnp.concatenate(r1[SUBLANES:], axis=0) + r2[0])
        top, _ = _take_top(jnp.concatenate(blocks, axis=0), n)
        kth = top[-1]
        z = None
        for r in top:
            e = jnp.exp(r - top[0])
            z = e if z is None else z + e
        hits = [jnp.where(blk >= kth, 1.0, 0.0) for blk in blocks]
        per_rank = [jnp.sum(hb, axis=0, keepdims=True) for hb in hits[:SUBLANES]]
        per_rank += [hits[SUBLANES][k:k + 1] for k in range(n - SUBLANES)]
        n1 = jnp.zeros(s1.shape, F32)
        for i in range(n):
            n1 = jnp.where(rank1 == float(i), per_rank[i], n1)
        rank2_ref[h] = rank2.astype(BF16)
        n1_ref[h] = n1
        e2_ref[h] = jnp.exp(s2 - r2[0]).astype(BF16)
        coef_ref[h] = jnp.exp(s1 - r1[0]) * (1.0 / z)
        return carry

    lax.fori_loop(0, P_HEADS, head, 0)


def _peer_topk(st, tt=256):
    T = st.shape[-1]
    spec = pl.BlockSpec((P_HEADS, P_NKEYS, tt), lambda i: (0, 0, i))
    shape = lambda dt: jax.ShapeDtypeStruct((P_HEADS, P_NKEYS, T), dt)
    return pl.pallas_call(
        _peer_topk_kernel,
        grid=(T // tt,),
        in_specs=[pl.BlockSpec((P_HEADS, 2, P_NKEYS, tt), lambda i: (0, 0, 0, i))],
        out_specs=[spec, spec, spec, spec],
        out_shape=[shape(BF16), shape(F32), shape(BF16), shape(F32)],
        compiler_params=_cparams(("parallel",)),
        name="peer_topk",
    )(st)


TILE_ROWS = 32
PACKED_ROWS = 16


def _peer_dense_kernel(h_ref, u_ref, vt_ref, rank_ref, e2_ref, n1_ref, coef_ref, res_ref, fg_ref, o_ref,
                       bn_s, bcoef_s, hb_s, g_s, act0_s, act1_s, acc_s, *, te, tt, final):
    j = pl.program_id(1)
    NK = P_NKEYS

    n_tiles = pl.num_programs(1) - 1

    @pl.when(j == 0)
    def _():
        acc_s[...] = jnp.zeros(acc_s.shape, F32)
        act0_s[...] = jnp.zeros(act0_s.shape, BF16)

    def step(act_prev, act_next):
        hw = tt // 2
        half = lambda c: slice(c * hw, (c + 1) * hw)

        def mm_down(c):
            acc_s[:, half(c)] += jnp.dot(vt_ref[...], act_prev[:, half(c)], preferred_element_type=F32)

        def mm_up(c):
            hb_s[:, half(c)] = lax.dot_general(u_ref[...], h_ref[half(c), :], NT_DIMS,
                                               preferred_element_type=F32)

        a0 = pl.multiple_of(jnp.minimum(j, n_tiles - 1) * SUBLANES, SUBLANES)
        reps = TILE_ROWS // PACKED_ROWS
        for h in range(P_HEADS):
            n8 = n1_ref[h, pl.ds(a0, SUBLANES), :]
            coef8 = coef_ref[h, pl.ds(a0, SUBLANES), :]
            for al in range(SUBLANES):
                bn_s[h, al] = jnp.broadcast_to(n8[al:al + 1], (PACKED_ROWS, tt)).astype(BF16)
                bcoef_s[h, al] = jnp.broadcast_to(coef8[al:al + 1], (PACKED_ROWS, tt)).astype(BF16)

        def weights(als):
            for al in als:
                for b0 in range(0, NK, TILE_ROWS):
                    sub = slice(b0, b0 + TILE_ROWS)
                    g = None
                    for h in range(P_HEADS):
                        pairs = jnp.concatenate([bn_s[h, al]] * reps, axis=0)
                        coef = jnp.concatenate([bcoef_s[h, al]] * reps, axis=0)
                        e2 = e2_ref[h, sub, :]
                        t = jnp.where(rank_ref[h, sub, :] < pairs, e2, jnp.zeros_like(e2)) * coef
                        g = t if g is None else g + t
                    g_s[al * NK + b0:al * NK + b0 + TILE_ROWS, :] = g

        def activate(c):
            for r0 in range(0, te, NK):
                hb = hb_s[r0:r0 + NK, half(c)]
                gelu = 0.5 * hb * (1.0 + lax.erf(hb * (2.0 ** -0.5)))
                act_next[r0:r0 + NK, half(c)] = gelu.astype(BF16) * g_s[r0:r0 + NK, half(c)]

        weights(range(0, SUBLANES // 2))
        mm_up(0)
        mm_down(0)
        weights(range(SUBLANES // 2, SUBLANES))
        mm_up(1)
        activate(0)
        mm_down(1)
        activate(1)

    @pl.when(j % 2 == 0)
    def _():
        step(act0_s, act1_s)

    @pl.when(j % 2 == 1)
    def _():
        step(act1_s, act0_s)

    @pl.when(j == n_tiles)
    def _():
        y = res_ref[...] + acc_s[...].T
        if final:
            y = _rms(y, fg_ref[...])
        o_ref[...] = y


def _peer_dense(hn, u, vt, sel, res, fgain, final, tt=512):
    T, D = res.shape
    E = u.shape[0]
    te = SUBLANES * P_NKEYS
    n_tiles = E // te
    kern = functools.partial(_peer_dense_kernel, te=te, tt=tt, final=final)
    per_token = pl.BlockSpec((P_HEADS, P_NKEYS, tt), lambda i, j: (0, 0, i))
    return pl.pallas_call(
        kern,
        grid=(T // tt, n_tiles + 1),
        in_specs=[pl.BlockSpec((tt, D), lambda i, j: (i, 0)),
                  pl.BlockSpec((te, D), lambda i, j: (jnp.minimum(j, n_tiles - 1), 0)),
                  pl.BlockSpec((D, te), lambda i, j: (0, jnp.maximum(j - 1, 0))),
                  per_token, per_token, per_token, per_token,
                  pl.BlockSpec((tt, D), lambda i, j: (i, 0)),
                  pl.BlockSpec((1, D), lambda i, j: (0, 0))],
        out_specs=pl.BlockSpec((tt, D), lambda i, j: (i, 0)),
        out_shape=jax.ShapeDtypeStruct((T, D), F32),
        scratch_shapes=[pltpu.VMEM((P_HEADS, SUBLANES, PACKED_ROWS, tt), BF16),
                        pltpu.VMEM((P_HEADS, SUBLANES, PACKED_ROWS, tt), BF16),
                        pltpu.VMEM((te, tt), F32),
                        pltpu.VMEM((te, tt), BF16),
                        pltpu.VMEM((te, tt), BF16),
                        pltpu.VMEM((te, tt), BF16),
                        pltpu.VMEM((D, tt), F32)],
        compiler_params=_cparams(("parallel", "arbitrary")),
        name="peer_dense",
    )(hn, u, vt, *sel, res, fgain.reshape(1, D))


def _peer(x, g, wq, keys, u, v, fgain, final):
    hn, st = _peer_scores(x, g, wq, keys)
    rank2, n1, e2, coef = _peer_topk(st)
    return _peer_dense(hn, u.astype(BF16), v.T.astype(BF16), (rank2, e2, n1, coef), x, fgain, final)


def _block_diag(blocks):
    H, r, c = blocks.shape
    eye = jnp.eye(H, dtype=blocks.dtype)
    return (eye[:, None, :, None] * blocks[:, :, None, :]).reshape(H * r, H * c)


def _in_proj_weights(w_in):
    sizes = (A_HEADS * A_HEAD_DIM, A_LATENT, IDX_HEADS * IDX_DIM, IDX_DIM, IDX_HEADS,
             R_HEADS * R_DK, R_HEADS * R_DK, R_HEADS * R_DV, R_HEADS * R_DV)
    pts = np.cumsum(sizes)[:-1].tolist()
    qa, ckv, qi, ki, wi, rq, rk, rv, rg = jnp.split(w_in, pts, axis=1)
    pad = jnp.zeros((w_in.shape[0], LANES - IDX_DIM - IDX_HEADS), w_in.dtype)
    return (jnp.concatenate([rq, rk, rv, rg, qa, ckv], axis=1),
            jnp.concatenate([qi, ki, wi, pad], axis=1))


def kernel(x, mix_norm_e, w_in, kv_norm, w_uk, w_uv, w_o, mix_norm_o, conv_w1, conv_b1, conv_dw, conv_dw_b, conv_ln_g, conv_ln_b, conv_w2, conv_b2, ffn_norm, peer_wq, peer_keys, peer_u, peer_v, final_norm):
    B, L, D = x.shape
    T = B * L
    depth = ffn_norm.shape[0]
    xf = x.reshape(T, D)
    for layer in range(depth):
        j = layer // 2
        if layer % 2 == 0:
            proj, idx = _in_proj(xf, mix_norm_e[j], *_in_proj_weights(w_in[j]))
            proj3 = proj.reshape(B, L, proj.shape[1])
            idx3 = idx.reshape(B, L, idx.shape[1])
            wuk_bd = _block_diag(jnp.swapaxes(w_uk[j], 1, 2)).astype(BF16)
            wuv_bd = _block_diag(w_uv[j]).astype(BF16)
            a_out = _dsa(proj3, idx3, kv_norm[j], wuk_bd, wuv_bd)
            b_out = _retention(proj3)
            wo = w_o[j].astype(BF16)
            na = A_HEADS * A_HEAD_DIM
            xf = _mm_res([a_out.reshape(T, na), b_out.reshape(T, -1)], [wo[:na], wo[na:]], xf)
        else:
            a = _norm_glu(xf, mix_norm_o[j], conv_w1[j].astype(BF16), conv_b1[j])
            y = _conv_ln(a.reshape(B, L, D), conv_dw[j], conv_dw_b[j], conv_ln_g[j], conv_ln_b[j])
            xf = _mm_res([y.reshape(T, D)], [conv_w2[j].astype(BF16)], xf, bias=conv_b2[j])
        xf = _peer(xf, ffn_norm[layer], peer_wq[layer], peer_keys[layer], peer_u[layer], peer_v[layer],
                   final_norm, final=(layer == depth - 1))
    return xf.reshape(B, L, D)
```

```python
import functools

import jax
import jax.numpy as jnp
import numpy as np
from jax import lax
from jax.experimental import pallas as pl
from jax.experimental.pallas import tpu as pltpu

F32 = jnp.float32
BF16 = jnp.bfloat16
I32 = jnp.int32

EPS = 1e-6
A_HEADS, A_HEAD_DIM, A_LATENT = 8, 64, 128
IDX_HEADS, IDX_DIM, TOPK_MAX = 8, 32, 256
R_HEADS, R_DK, R_DV, R_CHUNK = 4, 128, 128, 128
CONV_WIDTH = 31
P_HEADS, P_NKEYS, P_QDIM, P_TOPK = 8, 128, 256, 16

LANES = 128
SUBLANES = 8
VMEM_LIMIT = 56 * 1024 * 1024
INT_MIN = -(2 ** 31)
NEG_BIG = -1e30
LOG2_E = 1.4426950408889634

NT_DIMS = (((1,), (1,)), ((), ()))


def _cparams(sem, flags=None):
    return pltpu.CompilerParams(dimension_semantics=sem, vmem_limit_bytes=VMEM_LIMIT, flags=flags)


def _aligned(x, m):
    if isinstance(x, int):
        assert x % m == 0
        return x
    return pl.multiple_of(x, m)


def _rms(x, g):
    return x * lax.rsqrt(jnp.mean(x * x, axis=-1, keepdims=True) + EPS) * g


def _split(x):
    hi = x.astype(BF16)
    return hi, (x - hi.astype(F32)).astype(BF16)


def _dot3(a_hi, a_lo, b_hi, b_lo):
    d = functools.partial(jnp.dot, preferred_element_type=F32)
    return d(a_hi, b_hi) + (d(a_hi, b_lo) + d(a_lo, b_hi))


def _in_proj_kernel(x_ref, g_ref, w_ref, whi_ref, wlo_ref, o_ref, oi_ref):
    h = _rms(x_ref[...], g_ref[...])
    hi, lo = _split(h)
    o_ref[...] = jnp.dot(hi, w_ref[...], preferred_element_type=F32)
    oi_ref[...] = _dot3(hi, lo, whi_ref[...], wlo_ref[...])


def _in_proj(x, g, w_main, w_idx, tm=512):
    T, D = x.shape
    N, NI = w_main.shape[1], w_idx.shape[1]
    whi, wlo = _split(w_idx)
    return pl.pallas_call(
        _in_proj_kernel,
        grid=(T // tm,),
        in_specs=[pl.BlockSpec((tm, D), lambda i: (i, 0)),
                  pl.BlockSpec((1, D), lambda i: (0, 0)),
                  pl.BlockSpec((D, N), lambda i: (0, 0)),
                  pl.BlockSpec((D, NI), lambda i: (0, 0)),
                  pl.BlockSpec((D, NI), lambda i: (0, 0))],
        out_specs=[pl.BlockSpec((tm, N), lambda i: (i, 0)),
                   pl.BlockSpec((tm, NI), lambda i: (i, 0))],
        out_shape=[jax.ShapeDtypeStruct((T, N), F32), jax.ShapeDtypeStruct((T, NI), F32)],
        compiler_params=_cparams(("parallel",)),
        name="in_proj",
    )(x, g.reshape(1, D), w_main.astype(BF16), whi, wlo)


def _norm_glu_kernel(x_ref, g_ref, wa_ref, wg_ref, ba_ref, bg_ref, o_ref, h_scr):
    @pl.when(pl.program_id(1) == 0)
    def _():
        h_scr[...] = _rms(x_ref[...], g_ref[...]).astype(BF16)

    h = h_scr[...]
    a = jnp.dot(h, wa_ref[...], preferred_element_type=F32) + ba_ref[...]
    gate = jnp.dot(h, wg_ref[...], preferred_element_type=F32) + bg_ref[...]
    o_ref[...] = a * jax.nn.sigmoid(gate)


def _norm_glu(x, g, w1, b1, tm=512, tn=512):
    T, D = x.shape
    nj = D // tn
    b1 = b1.reshape(1, 2 * D)
    return pl.pallas_call(
        _norm_glu_kernel,
        grid=(T // tm, nj),
        in_specs=[pl.BlockSpec((tm, D), lambda i, j: (i, 0)),
                  pl.BlockSpec((1, D), lambda i, j: (0, 0)),
                  pl.BlockSpec((D, tn), lambda i, j: (0, j)),
                  pl.BlockSpec((D, tn), lambda i, j: (0, j + nj)),
                  pl.BlockSpec((1, tn), lambda i, j: (0, j)),
                  pl.BlockSpec((1, tn), lambda i, j: (0, j + nj))],
        out_specs=pl.BlockSpec((tm, tn), lambda i, j: (i, j)),
        out_shape=jax.ShapeDtypeStruct((T, D), F32),
        scratch_shapes=[pltpu.VMEM((tm, D), BF16)],
        compiler_params=_cparams(("parallel", "arbitrary")),
        name="norm_glu",
    )(x, g.reshape(1, D), w1, w1, b1, b1)


def _mm_res_kernel(*refs, n_lhs, has_bias):
    lhs = refs[:n_lhs]
    ws = refs[n_lhs:2 * n_lhs]
    rest = refs[2 * n_lhs:]
    if has_bias:
        b_ref, res_ref, o_ref = rest
        acc = res_ref[...] + b_ref[...]
    else:
        res_ref, o_ref = rest
        acc = res_ref[...]
    for a_ref, w_ref in zip(lhs, ws):
        acc = acc + jnp.dot(a_ref[...], w_ref[...], preferred_element_type=F32)
    o_ref[...] = acc


def _mm_res(lhs, ws, res, bias=None, tm=512):
    T, N = res.shape
    n = len(lhs)
    in_specs = [pl.BlockSpec((tm, a.shape[1]), lambda i: (i, 0)) for a in lhs]
    in_specs += [pl.BlockSpec(w.shape, lambda i: (0, 0)) for w in ws]
    args = list(lhs) + list(ws)
    if bias is not None:
        in_specs.append(pl.BlockSpec((1, N), lambda i: (0, 0)))
        args.append(bias.reshape(1, N))
    in_specs.append(pl.BlockSpec((tm, N), lambda i: (i, 0)))
    args.append(res)
    return pl.pallas_call(
        functools.partial(_mm_res_kernel, n_lhs=n, has_bias=bias is not None),
        grid=(T // tm,),
        in_specs=in_specs,
        out_specs=pl.BlockSpec((tm, N), lambda i: (i, 0)),
        out_shape=jax.ShapeDtypeStruct((T, N), F32),
        compiler_params=_cparams(("parallel",)),
        name="mm_res",
    )(*args)


def _dsa_kernel(qa_ref, qi_ref, misc_ref, ckv_ref, kit_ref, kvn_ref, wuk_ref, wuv_ref, o_ref,
                c_s, key_s, qst_s, qlat_s, wb_s, lg_s, bias_s, cmax_s, p_s, m_s, l_s, acc_s,
                *, Q, KC, topk, col_bits):
    qb = pl.program_id(1)
    H = A_HEADS
    NL = KC // LANES

    @pl.when(qb == 0)
    def _():
        c_s[...] = _rms(ckv_ref[0], kvn_ref[...]).astype(BF16)

    qi = qi_ref[0]
    wi = misc_ref[0][:, IDX_DIM:IDX_DIM + IDX_HEADS] * (IDX_HEADS ** -0.5)
    for h in range(IDX_HEADS):
        hi, lo = _split(qi[:, h * IDX_DIM:(h + 1) * IDX_DIM])
        qst_s[h * Q:(h + 1) * Q, :] = jnp.concatenate([hi, lo, hi], axis=1)
        wb_s[h] = jnp.broadcast_to(wi[:, h:h + 1], (Q, LANES))
    ql = jnp.dot(qa_ref[0].astype(BF16), wuk_ref[...], preferred_element_type=F32) * (A_HEAD_DIM ** -0.5 * LOG2_E)
    for h in range(H):
        qlat_s[h * Q:(h + 1) * Q, :] = ql[:, h * A_LATENT:(h + 1) * A_LATENT].astype(BF16)

    n_chunks = ((qb + 1) * Q + KC - 1) // KC
    q_pos = qb * Q + lax.broadcasted_iota(I32, (Q, LANES), 0)
    lane = lax.broadcasted_iota(I32, (Q, LANES), 1)

    RG = 4 * SUBLANES
    lane_rg = lax.broadcasted_iota(I32, (RG, LANES), 1)
    row_rg = lax.broadcasted_iota(I32, (RG, LANES), 0)

    def score_chunk(c, carry):
        off = pl.multiple_of(c * KC, KC)
        lg_s[...] = jnp.dot(qst_s[...], kit_ref[0, :, pl.ds(off, KC)], preferred_element_type=F32)
        for r0 in range(0, Q, RG):
            a = None
            for h in range(IDX_HEADS):
                w = jnp.concatenate([wb_s[h, r0:r0 + RG, :]] * NL, axis=1)
                t = jnp.maximum(lg_s[h * Q + r0:h * Q + r0 + RG, :], 0.0) * w
                a = t if a is None else a + t
            bits = pltpu.bitcast(a, I32)
            sgn = bits >> 31
            key = (bits ^ (sgn & 0x7FFFFFFF)) - sgn
            for l in range(NL):
                causal = off + l * LANES + lane_rg <= qb * Q + r0 + row_rg
                key_s[c * NL + l, r0:r0 + RG, :] = jnp.where(causal, key[:, l * LANES:(l + 1) * LANES], INT_MIN)
        return carry

    lax.fori_loop(0, n_chunks, score_chunk, 0)

    SR = min(Q, 16 * SUBLANES)
    lane_sr = lax.broadcasted_iota(I32, (SR, LANES), 1)

    def row_count(pred):
        cnts = []
        for r0 in range(0, Q, SR):
            def body(c, cnt, r0=r0):
                for l in range(NL):
                    blk = key_s[c * NL + l, r0:r0 + SR, :]
                    cnt = cnt + jnp.where(pred(blk, r0, c * KC + l * LANES), 1, 0)
                return cnt

            cnts.append(lax.fori_loop(0, n_chunks, body, jnp.zeros((SR, LANES), I32)))
        return jnp.sum(jnp.concatenate(cnts, axis=0).astype(F32), axis=1, keepdims=True)

    def bit_body(i, tau):
        cand = tau | lax.shift_left(jnp.int32(1), 31 - i)
        cand_s = cand ^ INT_MIN
        tot = row_count(lambda blk, r0, col0: blk >= cand_s[r0:r0 + SR])
        return jnp.where(tot >= topk, cand, tau)

    zero = jnp.zeros((Q, LANES), I32)
    tau = lax.fori_loop(0, 32, bit_body, zero)
    thr = jnp.maximum(tau ^ INT_MIN, INT_MIN + 1)

    n_ge = row_count(lambda blk, r0, col0: blk >= thr[r0:r0 + SR])
    need = topk - row_count(lambda blk, r0, col0: blk > thr[r0:r0 + SR])
    excess = n_ge > topk
    cmax_s[...] = jnp.full((Q, LANES), 2 ** 31 - 1, I32)

    @pl.when(jnp.max(jnp.where(excess, 1.0, 0.0)) > 0.0)
    def _():
        def col_bit(i, y):
            cand = y | lax.shift_left(jnp.int32(1), col_bits - 1 - i)
            tot = row_count(lambda blk, r0, col0:
                            (blk == thr[r0:r0 + SR]) & (col0 + lane_sr < cand[r0:r0 + SR]))
            return jnp.where(tot < need, cand, y)

        y = lax.fori_loop(0, col_bits, col_bit, zero)
        cmax_s[...] = jnp.where(excess, y, 2 ** 31 - 1)

    cmax = cmax_s[...]
    m_s[...] = jnp.full(m_s.shape, NEG_BIG, F32)
    l_s[...] = jnp.zeros(l_s.shape, F32)
    acc_s[...] = jnp.zeros(acc_s.shape, F32)

    def att_chunk(c, carry):
        off = pl.multiple_of(c * KC, KC)
        ck = c_s[pl.ds(off, KC), :]
        lg_s[...] = lax.dot_general(qlat_s[...], ck, NT_DIMS, preferred_element_type=F32)
        for l in range(NL):
            blk = key_s[c * NL + l]
            sel = (blk > thr) | ((blk == thr) & (off + l * LANES + lane <= cmax))
            bias_s[l] = jnp.where(sel, 0.0, NEG_BIG)

        for h in range(H):
            for r0 in range(0, Q, RG):
                rows = slice(h * Q + r0, h * Q + r0 + RG)
                m_prev = m_s[rows, :]
                lg = lg_s[rows, :]
                lgs = [lg[:, l * LANES:(l + 1) * LANES] + bias_s[l, r0:r0 + RG, :] for l in range(NL)]
                mx = lgs[0]
                for l in range(1, NL):
                    mx = jnp.maximum(mx, lgs[l])
                m_new = jnp.maximum(m_prev, jnp.max(mx, axis=1, keepdims=True))
                alpha = jnp.exp2(m_prev - m_new)
                ps = [jnp.exp2(x - m_new) for x in lgs]
                p_s[rows, :] = jnp.concatenate(ps, axis=1).astype(BF16)
                psum = ps[0]
                for l in range(1, NL):
                    psum = psum + ps[l]
                l_s[rows, :] = alpha * l_s[rows, :] + psum
                m_s[rows, :] = m_new
                acc_s[rows, :] = acc_s[rows, :] * alpha
        acc_s[...] += jnp.dot(p_s[...], ck, preferred_element_type=F32)
        return carry

    lax.fori_loop(0, n_chunks, att_chunk, 0)

    o = acc_s[...] / jnp.sum(l_s[...], axis=1, keepdims=True)
    o_cat = jnp.concatenate([o[h * Q:(h + 1) * Q, :] for h in range(H)], axis=1).astype(BF16)
    o_ref[0] = jnp.dot(o_cat, wuv_ref[...], preferred_element_type=F32).astype(BF16)


def _dsa(proj3, idx3, kv_norm, wuk_bd, wuv_bd, Q=256, KC=512):
    B, L, _ = proj3.shape
    H = A_HEADS
    topk = min(TOPK_MAX, L // 4)
    k_hi, k_lo = _split(jnp.swapaxes(idx3[:, :, 256:256 + IDX_DIM], 1, 2))
    kit = jnp.concatenate([k_hi, k_hi, k_lo], axis=1)
    kern = functools.partial(_dsa_kernel, Q=Q, KC=KC, topk=topk, col_bits=max(1, (L - 1).bit_length()))
    return pl.pallas_call(
        kern,
        grid=(B, L // Q),
        in_specs=[pl.BlockSpec((1, Q, 512), lambda b, q: (b, q, 4)),
                  pl.BlockSpec((1, Q, 256), lambda b, q: (b, q, 0)),
                  pl.BlockSpec((1, Q, 128), lambda b, q: (b, q, 2)),
                  pl.BlockSpec((1, L, 128), lambda b, q: (b, 0, 20)),
                  pl.BlockSpec((1, 3 * IDX_DIM, L), lambda b, q: (b, 0, 0)),
                  pl.BlockSpec((1, A_LATENT), lambda b, q: (0, 0)),
                  pl.BlockSpec(wuk_bd.shape, lambda b, q: (0, 0)),
                  pl.BlockSpec(wuv_bd.shape, lambda b, q: (0, 0))],
        out_specs=pl.BlockSpec((1, Q, H * A_HEAD_DIM), lambda b, q: (b, q, 0)),
        out_shape=jax.ShapeDtypeStruct((B, L, H * A_HEAD_DIM), BF16),
        scratch_shapes=[pltpu.VMEM((L, A_LATENT), BF16),
                        pltpu.VMEM((L // LANES, Q, LANES), I32),
                        pltpu.VMEM((IDX_HEADS * Q, 3 * IDX_DIM), BF16),
                        pltpu.VMEM((H * Q, A_LATENT), BF16),
                        pltpu.VMEM((IDX_HEADS, Q, LANES), F32),
                        pltpu.VMEM((H * Q, KC), F32),
                        pltpu.VMEM((KC // LANES, Q, LANES), F32),
                        pltpu.VMEM((Q, LANES), I32),
                        pltpu.VMEM((H * Q, KC), BF16),
                        pltpu.VMEM((H * Q, LANES), F32),
                        pltpu.VMEM((H * Q, LANES), F32),
                        pltpu.VMEM((H * Q, A_LATENT), F32)],
        compiler_params=_cparams(("arbitrary", "arbitrary")),
        name="dsa",
    )(proj3, idx3, idx3, proj3, kit, kv_norm.reshape(1, A_LATENT), wuk_bd, wuv_bd)


def _ret_kernel(q_ref, k_ref, v_ref, g_ref, cos_ref, sin_ref, dec_ref, xi_ref, zeta_ref, gch_ref,
                o_ref, st_s, *, B):
    @pl.when(pl.program_id(0) == 0)
    def _():
        st_s[...] = jnp.zeros(st_s.shape, F32)

    C = R_CHUNK
    cos = cos_ref[...]
    sin = sin_ref[...]
    even = (lax.broadcasted_iota(I32, (C, R_DK), 1) & 1) == 0

    def rot(x):
        partner = jnp.where(even, pltpu.roll(x, R_DK - 1, 1), pltpu.roll(x, 1, 1))
        return x * cos + partner * sin

    for b in range(B):
        for h in range(R_HEADS):
            sl = slice(h * R_DK, (h + 1) * R_DK)
            q = rot(q_ref[b, :, sl])
            k = rot(k_ref[b, :, sl]) * (R_DK ** -0.5)
            vb = v_ref[b, :, sl].astype(BF16)
            qb = q.astype(BF16)
            s = lax.dot_general(qb, k.astype(BF16), NT_DIMS, preferred_element_type=F32) * dec_ref[h]
            inner = jnp.dot(s.astype(BF16), vb, preferred_element_type=F32)
            st = st_s[b, h]
            cross = jnp.dot(qb, st.astype(BF16), preferred_element_type=F32) * xi_ref[h]
            kz = (k * zeta_ref[h]).astype(BF16)
            st_s[b, h] = st * gch_ref[h] + jnp.dot(kz.T, vb, preferred_element_type=F32)
            out = inner + cross
            mu = jnp.mean(out, axis=-1, keepdims=True)
            d = out - mu
            var = jnp.mean(d * d, axis=-1, keepdims=True)
            y = d * lax.rsqrt(var + EPS)
            gate = g_ref[b, :, sl]
            o_ref[b, :, sl] = (gate * jax.nn.sigmoid(gate) * y).astype(BF16)


def _retention(proj3):
    B, L, _ = proj3.shape
    C, H = R_CHUNK, R_HEADS
    N = L // C
    d2 = R_DK // 2
    inv_freq = 1.0 / (10000.0 ** jnp.linspace(0.0, 1.0, d2, dtype=F32))
    ang = jnp.arange(L, dtype=F32)[:, None] * inv_freq[None, :]
    cos = jnp.repeat(jnp.cos(ang), 2, axis=1)
    sin = jnp.stack([-jnp.sin(ang), jnp.sin(ang)], axis=-1).reshape(L, R_DK)
    log_g = jnp.log1p(-jnp.exp2(-5.0 - jnp.arange(H, dtype=F32)))
    pos = jnp.arange(C, dtype=F32)
    diff = pos[:, None] - pos[None, :]
    decay = jnp.where(diff >= 0, jnp.exp(jnp.maximum(diff, 0.0)[None] * log_g[:, None, None]), 0.0)
    xi = jnp.broadcast_to(jnp.exp((pos + 1.0)[None, :] * log_g[:, None])[:, :, None], (H, C, R_DV))
    zeta = jnp.broadcast_to(jnp.exp((C - 1.0 - pos)[None, :] * log_g[:, None])[:, :, None], (H, C, R_DK))
    gch = jnp.broadcast_to(jnp.exp(C * log_g)[:, None, None], (H, 1, R_DV))
    W = H * R_DK
    colspec = lambda j: pl.BlockSpec((B, C, W), lambda n: (0, n, j))
    cst = lambda shape: pl.BlockSpec(shape, lambda n: (0,) * len(shape))
    return pl.pallas_call(
        functools.partial(_ret_kernel, B=B),
        grid=(N,),
        in_specs=[colspec(0), colspec(1), colspec(2), colspec(3),
                  pl.BlockSpec((C, R_DK), lambda n: (n, 0)),
                  pl.BlockSpec((C, R_DK), lambda n: (n, 0)),
                  cst((H, C, C)), cst((H, C, R_DV)), cst((H, C, R_DK)), cst((H, 1, R_DV))],
        out_specs=pl.BlockSpec((B, C, W), lambda n: (0, n, 0)),
        out_shape=jax.ShapeDtypeStruct((B, L, W), BF16),
        scratch_shapes=[pltpu.VMEM((B, H, R_DK, R_DV), F32)],
        compiler_params=_cparams(("arbitrary",)),
        name="retention",
    )(proj3, proj3, proj3, proj3, cos, sin, decay, xi, zeta, gch)


HALO = 32


CONV_ROWS = 4 * SUBLANES


def _conv_kernel(cur_ref, prev_ref, w_ref, b_ref, lg_ref, lb_ref, o_ref, ext_s, sh_s, wb_s, y_s, *, tl):
    i = pl.program_id(1)
    D = ext_s.shape[1]

    @pl.when((pl.program_id(0) == 0) & (i == 0))
    def _():
        for j in range(CONV_WIDTH):
            wb_s[j] = jnp.broadcast_to(w_ref[j:j + 1, :], (SUBLANES, D))

    ext_s[HALO:, :] = cur_ref[0]

    @pl.when(i == 0)
    def _():
        ext_s[:HALO, :] = jnp.zeros((HALO, D), F32)

    @pl.when(i > 0)
    def _():
        ext_s[:HALO, :] = prev_ref[0]

    base = HALO - (CONV_WIDTH - 1)
    offs = range(base, base + CONV_WIDTH)
    for r in range(1, SUBLANES):
        span = max(o for o in offs if o % SUBLANES == r) - r + tl
        sh_s[r, :span, :] = ext_s[r:r + span, :]

    def rows(t, carry):
        row0 = pl.multiple_of(t * CONV_ROWS, CONV_ROWS)
        y = None
        for o in offs:
            r = o % SUBLANES
            start = pl.multiple_of(row0 + (o - r), SUBLANES)
            x = ext_s[pl.ds(start, CONV_ROWS), :] if r == 0 else sh_s[r, pl.ds(start, CONV_ROWS), :]
            term = x * jnp.concatenate([wb_s[o - base]] * (CONV_ROWS // SUBLANES), axis=0)
            y = term if y is None else y + term
        y_s[pl.ds(row0, CONV_ROWS), :] = y
        return carry

    lax.fori_loop(0, tl // CONV_ROWS, rows, 0)

    y = y_s[...] + b_ref[...]
    mu = jnp.mean(y, axis=-1, keepdims=True)
    d = y - mu
    var = jnp.mean(d * d, axis=-1, keepdims=True)
    z = d * lax.rsqrt(var + EPS) * lg_ref[...] + lb_ref[...]
    o_ref[0] = (z * jax.nn.sigmoid(z)).astype(BF16)


def _conv_ln(a3, w_dw, b_dw, ln_g, ln_b, tl=256):
    B, L, D = a3.shape
    r = tl // HALO
    row = lambda v: v.reshape(1, D)
    return pl.pallas_call(
        functools.partial(_conv_kernel, tl=tl),
        grid=(B, L // tl),
        in_specs=[pl.BlockSpec((1, tl, D), lambda b, i: (b, i, 0)),
                  pl.BlockSpec((1, HALO, D), lambda b, i: (b, jnp.maximum(i * r - 1, 0), 0)),
                  pl.BlockSpec((CONV_WIDTH, D), lambda b, i: (0, 0)),
                  pl.BlockSpec((1, D), lambda b, i: (0, 0)),
                  pl.BlockSpec((1, D), lambda b, i: (0, 0)),
                  pl.BlockSpec((1, D), lambda b, i: (0, 0))],
        out_specs=pl.BlockSpec((1, tl, D), lambda b, i: (b, i, 0)),
        out_shape=jax.ShapeDtypeStruct((B, L, D), BF16),
        scratch_shapes=[pltpu.VMEM((tl + HALO, D), F32),
                        pltpu.VMEM((SUBLANES, tl + HALO, D), F32),
                        pltpu.VMEM((CONV_WIDTH, SUBLANES, D), F32),
                        pltpu.VMEM((tl, D), F32)],
        compiler_params=_cparams(("arbitrary", "arbitrary")),
        name="conv_ln",
    )(a3, a3, w_dw, row(b_dw), row(ln_g), row(ln_b))


def _peer_score_kernel(x_ref, g_ref, whi_ref, wlo_ref, k3_ref, h_ref, st_ref, hi_s, lo_s):
    @pl.when(pl.program_id(1) == 0)
    def _():
        hi, lo = _split(_rms(x_ref[...], g_ref[...]))
        hi_s[...] = hi
        lo_s[...] = lo
        h_ref[...] = hi

    q = _dot3(hi_s[...], lo_s[...], whi_ref[...], wlo_ref[...])
    q_hi, q_lo = _split(q)
    half = P_QDIM // 2
    for s in range(q.shape[1] // half):
        cols = slice(s * half, (s + 1) * half)
        q3 = jnp.concatenate([q_hi[:, cols], q_lo[:, cols], q_hi[:, cols]], axis=1)
        st_ref[s // 2, s % 2] = lax.dot_general(k3_ref[s % 2], q3, NT_DIMS, preferred_element_type=F32)


def _peer_scores(x, g, wq, keys, tm=512, hps=2):
    T, D = x.shape
    whi, wlo = _split(wq)
    k_hi, k_lo = _split(keys)
    k3 = jnp.concatenate([k_hi, k_hi, k_lo], axis=-1)
    return pl.pallas_call(
        _peer_score_kernel,
        grid=(T // tm, P_HEADS // hps),
        in_specs=[pl.BlockSpec((tm, D), lambda i, j: (i, 0)),
                  pl.BlockSpec((1, D), lambda i, j: (0, 0)),
                  pl.BlockSpec((D, hps * P_QDIM), lambda i, j: (0, j)),
                  pl.BlockSpec((D, hps * P_QDIM), lambda i, j: (0, j)),
                  pl.BlockSpec(k3.shape, lambda i, j: (0, 0, 0))],
        out_specs=[pl.BlockSpec((tm, D), lambda i, j: (i, 0)),
                   pl.BlockSpec((hps, 2, P_NKEYS, tm), lambda i, j: (j, 0, 0, i))],
        out_shape=[jax.ShapeDtypeStruct((T, D), BF16),
                   jax.ShapeDtypeStruct((P_HEADS, 2, P_NKEYS, T), F32)],
        scratch_shapes=[pltpu.VMEM((tm, D), BF16), pltpu.VMEM((tm, D), BF16)],
        compiler_params=_cparams(("parallel", "arbitrary")),
        name="peer_scores",
    )(x, g.reshape(1, D), whi, wlo, k3)


def _take_top(s, n):
    rows = []
    for _ in range(n):
        mx = jnp.max(s, axis=0, keepdims=True)
        rows.append(mx)
        s = jnp.where(s == mx, -jnp.inf, s)
    return rows, s


def _take_top_ranked(s, n):
    rows = []
    rank = jnp.full(s.shape, float(n), F32)
    for r in range(n):
        mx = jnp.max(s, axis=0, keepdims=True)
        rows.append(mx)
        hit = s == mx
        rank = jnp.where(hit, float(r), rank)
        s = jnp.where(hit, -jnp.inf, s)
    return rows, rank


def _peer_topk_kernel(st_ref, rank2_ref, n1_ref, e2_ref, coef_ref):
    n = P_TOPK
    tt = st_ref.shape[-1]

    def head(h, carry):
        s1 = st_ref[h, 0]
        s2 = st_ref[h, 1]
        r1, rank1 = _take_top_ranked(s1, n)
        r2, rank2 = _take_top_ranked(s2, n)
        v2 = jnp.concatenate(r2, axis=0)
        blocks = [r1[0] + v2]
        rows8 = lax.broadcasted_iota(I32, (SUBLANES, tt), 0)
        for i in range(1, SUBLANES):
            blocks.append(jnp.where(rows8 < n // (i + 1), r1[i] + v2[:SUBLANES], -jnp.inf))
        blocks.append(jnp.concatenate(r1[SUBLANES:], axis=0) + r2[0])
        top, _ = _take_top(jnp.concatenate(blocks, axis=0), n)
        kth = top[-1]
        z = None
        for r in top:
            e = jnp.exp(r - top[0])
            z = e if z is None else z + e
        hits = [jnp.where(blk >= kth, 1.0, 0.0) for blk in blocks]
        per_rank = [jnp.sum(hb, axis=0, keepdims=True) for hb in hits[:SUBLANES]]
        per_rank += [hits[SUBLANES][k:k + 1] for k in range(n - SUBLANES)]
        n1 = jnp.zeros(s1.shape, F32)
        for i in range(n):
            n1 = jnp.where(rank1 == float(i), per_rank[i], n1)
        rank2_ref[h] = rank2.astype(BF16)
        n1_ref[h] = n1
        e2_ref[h] = jnp.exp(s2 - r2[0]).astype(BF16)
        coef_ref[h] = jnp.exp(s1 - r1[0]) * (0.5 / z)
        return carry

    lax.fori_loop(0, P_HEADS, head, 0)


def _peer_topk(st, tt=256):
    T = st.shape[-1]
    spec = pl.BlockSpec((P_HEADS, P_NKEYS, tt), lambda i: (0, 0, i))
    shape = lambda dt: jax.ShapeDtypeStruct((P_HEADS, P_NKEYS, T), dt)
    return pl.pallas_call(
        _peer_topk_kernel,
        grid=(T // tt,),
        in_specs=[pl.BlockSpec((P_HEADS, 2, P_NKEYS, tt), lambda i: (0, 0, 0, i))],
        out_specs=[spec, spec, spec, spec],
        out_shape=[shape(BF16), shape(F32), shape(BF16), shape(F32)],
        compiler_params=_cparams(("parallel",)),
        name="peer_topk",
    )(st)


TILE_ROWS = 32
PACKED_ROWS = 16


def _peer_dense_kernel(h_ref, u_ref, vt_ref, rank_ref, e2_ref, n1_ref, coef_ref, res_ref, fg_ref, o_ref,
                       bn_s, bcoef_s, acc_s, *, te, tt, final):
    j = pl.program_id(1)
    NK = P_NKEYS

    @pl.when(j == 0)
    def _():
        acc_s[...] = jnp.zeros(acc_s.shape, F32)

    a0 = pl.multiple_of(j * SUBLANES, SUBLANES)
    reps = TILE_ROWS // PACKED_ROWS
    for h in range(P_HEADS):
        n8 = n1_ref[h, pl.ds(a0, SUBLANES), :]
        coef8 = coef_ref[h, pl.ds(a0, SUBLANES), :]
        for al in range(SUBLANES):
            bn_s[h, al] = jnp.broadcast_to(n8[al:al + 1], (PACKED_ROWS, tt)).astype(BF16)
            bcoef_s[h, al] = jnp.broadcast_to(coef8[al:al + 1], (PACKED_ROWS, tt)).astype(BF16)

    hb = lax.dot_general(u_ref[...], h_ref[...], NT_DIMS, preferred_element_type=F32)
    acts = []
    for al in range(SUBLANES):
        for b0 in range(0, NK, TILE_ROWS):
            sub = slice(b0, b0 + TILE_ROWS)
            g = None
            for h in range(P_HEADS):
                pairs = jnp.concatenate([bn_s[h, al]] * reps, axis=0)
                coef = jnp.concatenate([bcoef_s[h, al]] * reps, axis=0)
                e2 = e2_ref[h, sub, :]
                t = jnp.where(rank_ref[h, sub, :] < pairs, e2, jnp.zeros_like(e2)) * coef
                g = t if g is None else g + t
            x = hb[al * NK + b0:al * NK + b0 + TILE_ROWS, :]
            acts.append((x * (1.0 + lax.erf(x * (2.0 ** -0.5)))).astype(BF16) * g)
    act = jnp.concatenate(acts, axis=0)
    acc_s[...] += jnp.dot(vt_ref[...], act, preferred_element_type=F32)

    @pl.when(j == pl.num_programs(1) - 1)
    def _():
        y = res_ref[...] + acc_s[...].T
        if final:
            y = _rms(y, fg_ref[...])
        o_ref[...] = y


def _peer_dense(hn, u, vt, sel, res, fgain, final, tt=512):
    T, D = res.shape
    E = u.shape[0]
    te = SUBLANES * P_NKEYS
    n_tiles = E // te
    kern = functools.partial(_peer_dense_kernel, te=te, tt=tt, final=final)
    per_token = pl.BlockSpec((P_HEADS, P_NKEYS, tt), lambda i, j: (0, 0, i))
    return pl.pallas_call(
        kern,
        grid=(T // tt, n_tiles),
        in_specs=[pl.BlockSpec((tt, D), lambda i, j: (i, 0)),
                  pl.BlockSpec((te, D), lambda i, j: (j, 0)),
                  pl.BlockSpec((D, te), lambda i, j: (0, j)),
                  per_token, per_token, per_token, per_token,
                  pl.BlockSpec((tt, D), lambda i, j: (i, 0)),
                  pl.BlockSpec((1, D), lambda i, j: (0, 0))],
        out_specs=pl.BlockSpec((tt, D), lambda i, j: (i, 0)),
        out_shape=jax.ShapeDtypeStruct((T, D), F32),
        scratch_shapes=[pltpu.VMEM((P_HEADS, SUBLANES, PACKED_ROWS, tt), BF16),
                        pltpu.VMEM((P_HEADS, SUBLANES, PACKED_ROWS, tt), BF16),
                        pltpu.VMEM((D, tt), F32)],
        compiler_params=_cparams(("parallel", "arbitrary")),
        name="peer_dense",
    )(hn, u, vt, *sel, res, fgain.reshape(1, D))


def _peer(x, g, wq, keys, u, v, fgain, final):
    hn, st = _peer_scores(x, g, wq, keys)
    rank2, n1, e2, coef = _peer_topk(st)
    return _peer_dense(hn, u.astype(BF16), v.T.astype(BF16), (rank2, e2, n1, coef), x, fgain, final)


def _block_diag(blocks):
    H, r, c = blocks.shape
    eye = jnp.eye(H, dtype=blocks.dtype)
    return (eye[:, None, :, None] * blocks[:, :, None, :]).reshape(H * r, H * c)


def _in_proj_weights(w_in):
    sizes = (A_HEADS * A_HEAD_DIM, A_LATENT, IDX_HEADS * IDX_DIM, IDX_DIM, IDX_HEADS,
             R_HEADS * R_DK, R_HEADS * R_DK, R_HEADS * R_DV, R_HEADS * R_DV)
    pts = np.cumsum(sizes)[:-1].tolist()
    qa, ckv, qi, ki, wi, rq, rk, rv, rg = jnp.split(w_in, pts, axis=1)
    pad = jnp.zeros((w_in.shape[0], LANES - IDX_DIM - IDX_HEADS), w_in.dtype)
    return (jnp.concatenate([rq, rk, rv, rg, qa, ckv], axis=1),
            jnp.concatenate([qi, ki, wi, pad], axis=1))


def kernel(x, mix_norm_e, w_in, kv_norm, w_uk, w_uv, w_o, mix_norm_o, conv_w1, conv_b1, conv_dw, conv_dw_b, conv_ln_g, conv_ln_b, conv_w2, conv_b2, ffn_norm, peer_wq, peer_keys, peer_u, peer_v, final_norm):
    B, L, D = x.shape
    T = B * L
    depth = ffn_norm.shape[0]
    xf = x.reshape(T, D)
    for layer in range(depth):
        j = layer // 2
        if layer % 2 == 0:
            proj, idx = _in_proj(xf, mix_norm_e[j], *_in_proj_weights(w_in[j]))
            proj3 = proj.reshape(B, L, proj.shape[1])
            idx3 = idx.reshape(B, L, idx.shape[1])
            wuk_bd = _block_diag(jnp.swapaxes(w_uk[j], 1, 2)).astype(BF16)
            wuv_bd = _block_diag(w_uv[j]).astype(BF16)
            a_out = _dsa(proj3, idx3, kv_norm[j], wuk_bd, wuv_bd)
            b_out = _retention(proj3)
            wo = w_o[j].astype(BF16)
            na = A_HEADS * A_HEAD_DIM
            xf = _mm_res([a_out.reshape(T, na), b_out.reshape(T, -1)], [wo[:na], wo[na:]], xf)
        else:
            a = _norm_glu(xf, mix_norm_o[j], conv_w1[j].astype(BF16), conv_b1[j])
            y = _conv_ln(a.reshape(B, L, D), conv_dw[j], conv_dw_b[j], conv_ln_g[j], conv_ln_b[j])
            xf = _mm_res([y.reshape(T, D)], [conv_w2[j].astype(BF16)], xf, bias=conv_b2[j])
        xf = _peer(xf, ffn_norm[layer], peer_wq[layer], peer_keys[layer], peer_u[layer], peer_v[layer],
                   final_norm, final=(layer == depth - 1))
    return xf.reshape(B, L, D)
```

```python
import functools

import jax
import jax.numpy as jnp
import numpy as np
from jax import lax
from jax.experimental import pallas as pl
from jax.experimental.pallas import tpu as pltpu

F32 = jnp.float32
BF16 = jnp.bfloat16
I32 = jnp.int32

EPS = 1e-6
A_HEADS, A_HEAD_DIM, A_LATENT = 8, 64, 128
IDX_HEADS, IDX_DIM, TOPK_MAX = 8, 32, 256
R_HEADS, R_DK, R_DV, R_CHUNK = 4, 128, 128, 128
CONV_WIDTH = 31
P_HEADS, P_NKEYS, P_QDIM, P_TOPK = 8, 128, 256, 16

LANES = 128
SUBLANES = 8
VMEM_LIMIT = 56 * 1024 * 1024
INT_MIN = -(2 ** 31)
NEG_BIG = -1e30
LOG2_E = 1.4426950408889634

NT_DIMS = (((1,), (1,)), ((), ()))


def _cparams(sem, flags=None):
    return pltpu.CompilerParams(dimension_semantics=sem, vmem_limit_bytes=VMEM_LIMIT, flags=flags)


def _aligned(x, m):
    if isinstance(x, int):
        assert x % m == 0
        return x
    return pl.multiple_of(x, m)


def _rms(x, g):
    return x * lax.rsqrt(jnp.mean(x * x, axis=-1, keepdims=True) + EPS) * g


def _split(x):
    hi = x.astype(BF16)
    return hi, (x - hi.astype(F32)).astype(BF16)


def _dot3(a_hi, a_lo, b_hi, b_lo):
    d = functools.partial(jnp.dot, preferred_element_type=F32)
    return d(a_hi, b_hi) + (d(a_hi, b_lo) + d(a_lo, b_hi))


def _in_proj_kernel(x_ref, g_ref, w_ref, whi_ref, wlo_ref, o_ref, oi_ref):
    h = _rms(x_ref[...], g_ref[...])
    hi, lo = _split(h)
    o_ref[...] = jnp.dot(hi, w_ref[...], preferred_element_type=F32)
    oi_ref[...] = _dot3(hi, lo, whi_ref[...], wlo_ref[...])


def _in_proj(x, g, w_main, w_idx, tm=512):
    T, D = x.shape
    N, NI = w_main.shape[1], w_idx.shape[1]
    whi, wlo = _split(w_idx)
    return pl.pallas_call(
        _in_proj_kernel,
        grid=(T // tm,),
        in_specs=[pl.BlockSpec((tm, D), lambda i: (i, 0)),
                  pl.BlockSpec((1, D), lambda i: (0, 0)),
                  pl.BlockSpec((D, N), lambda i: (0, 0)),
                  pl.BlockSpec((D, NI), lambda i: (0, 0)),
                  pl.BlockSpec((D, NI), lambda i: (0, 0))],
        out_specs=[pl.BlockSpec((tm, N), lambda i: (i, 0)),
                   pl.BlockSpec((tm, NI), lambda i: (i, 0))],
        out_shape=[jax.ShapeDtypeStruct((T, N), F32), jax.ShapeDtypeStruct((T, NI), F32)],
        compiler_params=_cparams(("parallel",)),
        name="in_proj",
    )(x, g.reshape(1, D), w_main.astype(BF16), whi, wlo)


def _norm_glu_kernel(x_ref, g_ref, wa_ref, wg_ref, ba_ref, bg_ref, o_ref, h_scr):
    @pl.when(pl.program_id(1) == 0)
    def _():
        h_scr[...] = _rms(x_ref[...], g_ref[...]).astype(BF16)

    h = h_scr[...]
    a = jnp.dot(h, wa_ref[...], preferred_element_type=F32) + ba_ref[...]
    gate = jnp.dot(h, wg_ref[...], preferred_element_type=F32) + bg_ref[...]
    o_ref[...] = a * jax.nn.sigmoid(gate)


def _norm_glu(x, g, w1, b1, tm=512, tn=512):
    T, D = x.shape
    nj = D // tn
    b1 = b1.reshape(1, 2 * D)
    return pl.pallas_call(
        _norm_glu_kernel,
        grid=(T // tm, nj),
        in_specs=[pl.BlockSpec((tm, D), lambda i, j: (i, 0)),
                  pl.BlockSpec((1, D), lambda i, j: (0, 0)),
                  pl.BlockSpec((D, tn), lambda i, j: (0, j)),
                  pl.BlockSpec((D, tn), lambda i, j: (0, j + nj)),
                  pl.BlockSpec((1, tn), lambda i, j: (0, j)),
                  pl.BlockSpec((1, tn), lambda i, j: (0, j + nj))],
        out_specs=pl.BlockSpec((tm, tn), lambda i, j: (i, j)),
        out_shape=jax.ShapeDtypeStruct((T, D), F32),
        scratch_shapes=[pltpu.VMEM((tm, D), BF16)],
        compiler_params=_cparams(("parallel", "arbitrary")),
        name="norm_glu",
    )(x, g.reshape(1, D), w1, w1, b1, b1)


def _mm_res_kernel(*refs, n_lhs, has_bias):
    lhs = refs[:n_lhs]
    ws = refs[n_lhs:2 * n_lhs]
    rest = refs[2 * n_lhs:]
    if has_bias:
        b_ref, res_ref, o_ref = rest
        acc = res_ref[...] + b_ref[...]
    else:
        res_ref, o_ref = rest
        acc = res_ref[...]
    for a_ref, w_ref in zip(lhs, ws):
        acc = acc + jnp.dot(a_ref[...], w_ref[...], preferred_element_type=F32)
    o_ref[...] = acc


def _mm_res(lhs, ws, res, bias=None, tm=512):
    T, N = res.shape
    n = len(lhs)
    in_specs = [pl.BlockSpec((tm, a.shape[1]), lambda i: (i, 0)) for a in lhs]
    in_specs += [pl.BlockSpec(w.shape, lambda i: (0, 0)) for w in ws]
    args = list(lhs) + list(ws)
    if bias is not None:
        in_specs.append(pl.BlockSpec((1, N), lambda i: (0, 0)))
        args.append(bias.reshape(1, N))
    in_specs.append(pl.BlockSpec((tm, N), lambda i: (i, 0)))
    args.append(res)
    return pl.pallas_call(
        functools.partial(_mm_res_kernel, n_lhs=n, has_bias=bias is not None),
        grid=(T // tm,),
        in_specs=in_specs,
        out_specs=pl.BlockSpec((tm, N), lambda i: (i, 0)),
        out_shape=jax.ShapeDtypeStruct((T, N), F32),
        compiler_params=_cparams(("parallel",)),
        name="mm_res",
    )(*args)


def _dsa_kernel(qa_ref, qi_ref, misc_ref, ckv_ref, kit_ref, kvn_ref, wuk_ref, wuv_ref, o_ref,
                c_s, key_s, qst_s, qlat_s, wb_s, lg_s, bias_s, cmax_s, p_s, m_s, l_s, acc_s,
                *, Q, KC, topk, col_bits):
    qb = pl.program_id(1)
    H = A_HEADS
    NL = KC // LANES

    @pl.when(qb == 0)
    def _():
        c_s[...] = _rms(ckv_ref[0], kvn_ref[...]).astype(BF16)

    qi = qi_ref[0]
    wi = misc_ref[0][:, IDX_DIM:IDX_DIM + IDX_HEADS] * (IDX_HEADS ** -0.5)
    for h in range(IDX_HEADS):
        hi, lo = _split(qi[:, h * IDX_DIM:(h + 1) * IDX_DIM])
        qst_s[h * Q:(h + 1) * Q, :] = jnp.concatenate([hi, lo, hi], axis=1)
        wb_s[h] = jnp.broadcast_to(wi[:, h:h + 1], (Q, LANES))
    ql = jnp.dot(qa_ref[0].astype(BF16), wuk_ref[...], preferred_element_type=F32) * (A_HEAD_DIM ** -0.5 * LOG2_E)
    for h in range(H):
        qlat_s[h * Q:(h + 1) * Q, :] = ql[:, h * A_LATENT:(h + 1) * A_LATENT].astype(BF16)

    n_chunks = ((qb + 1) * Q + KC - 1) // KC
    q_pos = qb * Q + lax.broadcasted_iota(I32, (Q, LANES), 0)
    lane = lax.broadcasted_iota(I32, (Q, LANES), 1)

    RG = 4 * SUBLANES
    lane_rg = lax.broadcasted_iota(I32, (RG, LANES), 1)
    row_rg = lax.broadcasted_iota(I32, (RG, LANES), 0)

    def score_chunk(c, carry):
        off = pl.multiple_of(c * KC, KC)
        lg_s[...] = jnp.dot(qst_s[...], kit_ref[0, :, pl.ds(off, KC)], preferred_element_type=F32)
        for r0 in range(0, Q, RG):
            a = None
            for h in range(IDX_HEADS):
                w = jnp.concatenate([wb_s[h, r0:r0 + RG, :]] * NL, axis=1)
                t = jnp.maximum(lg_s[h * Q + r0:h * Q + r0 + RG, :], 0.0) * w
                a = t if a is None else a + t
            bits = pltpu.bitcast(a, I32)
            sgn = bits >> 31
            key = (bits ^ (sgn & 0x7FFFFFFF)) - sgn
            for l in range(NL):
                causal = off + l * LANES + lane_rg <= qb * Q + r0 + row_rg
                key_s[c * NL + l, r0:r0 + RG, :] = jnp.where(causal, key[:, l * LANES:(l + 1) * LANES], INT_MIN)
        return carry

    lax.fori_loop(0, n_chunks, score_chunk, 0)

    SR = min(Q, 16 * SUBLANES)
    lane_sr = lax.broadcasted_iota(I32, (SR, LANES), 1)

    def row_count(pred):
        cnts = []
        for r0 in range(0, Q, SR):
            def body(c, cnt, r0=r0):
                for l in range(NL):
                    blk = key_s[c * NL + l, r0:r0 + SR, :]
                    cnt = cnt + jnp.where(pred(blk, r0, c * KC + l * LANES), 1, 0)
                return cnt

            cnts.append(lax.fori_loop(0, n_chunks, body, jnp.zeros((SR, LANES), I32)))
        return jnp.sum(jnp.concatenate(cnts, axis=0).astype(F32), axis=1, keepdims=True)

    def bit_body(i, carry):
        tau, n_at = carry
        cand = tau | lax.shift_left(jnp.int32(1), 31 - i)
        cand_s = cand ^ INT_MIN
        tot = row_count(lambda blk, r0, col0: blk >= cand_s[r0:r0 + SR])
        take = tot >= topk
        return jnp.where(take, cand, tau), jnp.where(take, tot, n_at)

    zero = jnp.zeros((Q, LANES), I32)
    tau, n_ge = lax.fori_loop(0, 32, bit_body, (zero, jnp.zeros((Q, 1), F32)))
    thr = jnp.maximum(tau ^ INT_MIN, INT_MIN + 1)

    excess = n_ge > topk
    cmax_s[...] = jnp.full((Q, LANES), 2 ** 31 - 1, I32)

    @pl.when(jnp.max(jnp.where(excess, 1.0, 0.0)) > 0.0)
    def _():
        need = topk - row_count(lambda blk, r0, col0: blk > thr[r0:r0 + SR])

        def col_bit(i, y):
            cand = y | lax.shift_left(jnp.int32(1), col_bits - 1 - i)
            tot = row_count(lambda blk, r0, col0:
                            (blk == thr[r0:r0 + SR]) & (col0 + lane_sr < cand[r0:r0 + SR]))
            return jnp.where(tot < need, cand, y)

        y = lax.fori_loop(0, col_bits, col_bit, zero)
        cmax_s[...] = jnp.where(excess, y, 2 ** 31 - 1)

    cmax = cmax_s[...]
    m_s[...] = jnp.full(m_s.shape, NEG_BIG, F32)
    l_s[...] = jnp.zeros(l_s.shape, F32)
    acc_s[...] = jnp.zeros(acc_s.shape, F32)

    def att_chunk(c, carry):
        off = pl.multiple_of(c * KC, KC)
        ck = c_s[pl.ds(off, KC), :]
        lg_s[...] = lax.dot_general(qlat_s[...], ck, NT_DIMS, preferred_element_type=F32)
        for l in range(NL):
            blk = key_s[c * NL + l]
            sel = (blk > thr) | ((blk == thr) & (off + l * LANES + lane <= cmax))
            bias_s[l] = jnp.where(sel, 0.0, NEG_BIG)

        for h in range(H):
            for r0 in range(0, Q, RG):
                rows = slice(h * Q + r0, h * Q + r0 + RG)
                m_prev = m_s[rows, :]
                lg = lg_s[rows, :]
                lgs = [lg[:, l * LANES:(l + 1) * LANES] + bias_s[l, r0:r0 + RG, :] for l in range(NL)]
                mx = lgs[0]
                for l in range(1, NL):
                    mx = jnp.maximum(mx, lgs[l])
                m_new = jnp.maximum(m_prev, jnp.max(mx, axis=1, keepdims=True))
                alpha = jnp.exp2(m_prev - m_new)
                ps = [jnp.exp2(x - m_new) for x in lgs]
                p_s[rows, :] = jnp.concatenate(ps, axis=1).astype(BF16)
                psum = ps[0]
                for l in range(1, NL):
                    psum = psum + ps[l]
                l_s[rows, :] = alpha * l_s[rows, :] + psum
                m_s[rows, :] = m_new
                acc_s[rows, :] = acc_s[rows, :] * alpha
        acc_s[...] += jnp.dot(p_s[...], ck, preferred_element_type=F32)
        return carry

    lax.fori_loop(0, n_chunks, att_chunk, 0)

    o = acc_s[...] / jnp.sum(l_s[...], axis=1, keepdims=True)
    o_cat = jnp.concatenate([o[h * Q:(h + 1) * Q, :] for h in range(H)], axis=1).astype(BF16)
    o_ref[0] = jnp.dot(o_cat, wuv_ref[...], preferred_element_type=F32).astype(BF16)


def _dsa(proj3, idx3, kv_norm, wuk_bd, wuv_bd, Q=256, KC=512):
    B, L, _ = proj3.shape
    H = A_HEADS
    topk = min(TOPK_MAX, L // 4)
    k_hi, k_lo = _split(jnp.swapaxes(idx3[:, :, 256:256 + IDX_DIM], 1, 2))
    kit = jnp.concatenate([k_hi, k_hi, k_lo], axis=1)
    kern = functools.partial(_dsa_kernel, Q=Q, KC=KC, topk=topk, col_bits=max(1, (L - 1).bit_length()))
    return pl.pallas_call(
        kern,
        grid=(B, L // Q),
        in_specs=[pl.BlockSpec((1, Q, 512), lambda b, q: (b, q, 4)),
                  pl.BlockSpec((1, Q, 256), lambda b, q: (b, q, 0)),
                  pl.BlockSpec((1, Q, 128), lambda b, q: (b, q, 2)),
                  pl.BlockSpec((1, L, 128), lambda b, q: (b, 0, 20)),
                  pl.BlockSpec((1, 3 * IDX_DIM, L), lambda b, q: (b, 0, 0)),
                  pl.BlockSpec((1, A_LATENT), lambda b, q: (0, 0)),
                  pl.BlockSpec(wuk_bd.shape, lambda b, q: (0, 0)),
                  pl.BlockSpec(wuv_bd.shape, lambda b, q: (0, 0))],
        out_specs=pl.BlockSpec((1, Q, H * A_HEAD_DIM), lambda b, q: (b, q, 0)),
        out_shape=jax.ShapeDtypeStruct((B, L, H * A_HEAD_DIM), BF16),
        scratch_shapes=[pltpu.VMEM((L, A_LATENT), BF16),
                        pltpu.VMEM((L // LANES, Q, LANES), I32),
                        pltpu.VMEM((IDX_HEADS * Q, 3 * IDX_DIM), BF16),
                        pltpu.VMEM((H * Q, A_LATENT), BF16),
                        pltpu.VMEM((IDX_HEADS, Q, LANES), F32),
                        pltpu.VMEM((H * Q, KC), F32),
                        pltpu.VMEM((KC // LANES, Q, LANES), F32),
                        pltpu.VMEM((Q, LANES), I32),
                        pltpu.VMEM((H * Q, KC), BF16),
                        pltpu.VMEM((H * Q, LANES), F32),
                        pltpu.VMEM((H * Q, LANES), F32),
                        pltpu.VMEM((H * Q, A_LATENT), F32)],
        compiler_params=_cparams(("arbitrary", "arbitrary")),
        name="dsa",
    )(proj3, idx3, idx3, proj3, kit, kv_norm.reshape(1, A_LATENT), wuk_bd, wuv_bd)


def _ret_kernel(q_ref, k_ref, v_ref, g_ref, cos_ref, sin_ref, dec_ref, xi_ref, zeta_ref, gch_ref,
                o_ref, st_s, *, B):
    @pl.when(pl.program_id(0) == 0)
    def _():
        st_s[...] = jnp.zeros(st_s.shape, F32)

    C = R_CHUNK
    cos = cos_ref[...]
    sin = sin_ref[...]
    even = (lax.broadcasted_iota(I32, (C, R_DK), 1) & 1) == 0

    def rot(x):
        partner = jnp.where(even, pltpu.roll(x, R_DK - 1, 1), pltpu.roll(x, 1, 1))
        return x * cos + partner * sin

    for b in range(B):
        for h in range(R_HEADS):
            sl = slice(h * R_DK, (h + 1) * R_DK)
            q = rot(q_ref[b, :, sl])
            k = rot(k_ref[b, :, sl]) * (R_DK ** -0.5)
            vb = v_ref[b, :, sl].astype(BF16)
            qb = q.astype(BF16)
            s = lax.dot_general(qb, k.astype(BF16), NT_DIMS, preferred_element_type=F32) * dec_ref[h]
            inner = jnp.dot(s.astype(BF16), vb, preferred_element_type=F32)
            st = st_s[b, h]
            cross = jnp.dot(qb, st.astype(BF16), preferred_element_type=F32) * xi_ref[h]
            kz = (k * zeta_ref[h]).astype(BF16)
            st_s[b, h] = st * gch_ref[h] + jnp.dot(kz.T, vb, preferred_element_type=F32)
            out = inner + cross
            mu = jnp.mean(out, axis=-1, keepdims=True)
            d = out - mu
            var = jnp.mean(d * d, axis=-1, keepdims=True)
            y = d * lax.rsqrt(var + EPS)
            gate = g_ref[b, :, sl]
            o_ref[b, :, sl] = (gate * jax.nn.sigmoid(gate) * y).astype(BF16)


def _retention(proj3):
    B, L, _ = proj3.shape
    C, H = R_CHUNK, R_HEADS
    N = L // C
    d2 = R_DK // 2
    inv_freq = 1.0 / (10000.0 ** jnp.linspace(0.0, 1.0, d2, dtype=F32))
    ang = jnp.arange(L, dtype=F32)[:, None] * inv_freq[None, :]
    cos = jnp.repeat(jnp.cos(ang), 2, axis=1)
    sin = jnp.stack([-jnp.sin(ang), jnp.sin(ang)], axis=-1).reshape(L, R_DK)
    log_g = jnp.log1p(-jnp.exp2(-5.0 - jnp.arange(H, dtype=F32)))
    pos = jnp.arange(C, dtype=F32)
    diff = pos[:, None] - pos[None, :]
    decay = jnp.where(diff >= 0, jnp.exp(jnp.maximum(diff, 0.0)[None] * log_g[:, None, None]), 0.0)
    xi = jnp.broadcast_to(jnp.exp((pos + 1.0)[None, :] * log_g[:, None])[:, :, None], (H, C, R_DV))
    zeta = jnp.broadcast_to(jnp.exp((C - 1.0 - pos)[None, :] * log_g[:, None])[:, :, None], (H, C, R_DK))
    gch = jnp.broadcast_to(jnp.exp(C * log_g)[:, None, None], (H, 1, R_DV))
    W = H * R_DK
    colspec = lambda j: pl.BlockSpec((B, C, W), lambda n: (0, n, j))
    cst = lambda shape: pl.BlockSpec(shape, lambda n: (0,) * len(shape))
    return pl.pallas_call(
        functools.partial(_ret_kernel, B=B),
        grid=(N,),
        in_specs=[colspec(0), colspec(1), colspec(2), colspec(3),
                  pl.BlockSpec((C, R_DK), lambda n: (n, 0)),
                  pl.BlockSpec((C, R_DK), lambda n: (n, 0)),
                  cst((H, C, C)), cst((H, C, R_DV)), cst((H, C, R_DK)), cst((H, 1, R_DV))],
        out_specs=pl.BlockSpec((B, C, W), lambda n: (0, n, 0)),
        out_shape=jax.ShapeDtypeStruct((B, L, W), BF16),
        scratch_shapes=[pltpu.VMEM((B, H, R_DK, R_DV), F32)],
        compiler_params=_cparams(("arbitrary",)),
        name="retention",
    )(proj3, proj3, proj3, proj3, cos, sin, decay, xi, zeta, gch)


HALO = 32


CONV_ROWS = 4 * SUBLANES


def _conv_kernel(cur_ref, prev_ref, w_ref, b_ref, lg_ref, lb_ref, o_ref, ext_s, sh_s, wb_s, y_s, *, tl):
    i = pl.program_id(1)
    D = ext_s.shape[1]

    @pl.when((pl.program_id(0) == 0) & (i == 0))
    def _():
        for j in range(CONV_WIDTH):
            wb_s[j] = jnp.broadcast_to(w_ref[j:j + 1, :], (SUBLANES, D))

    ext_s[HALO:, :] = cur_ref[0]

    @pl.when(i == 0)
    def _():
        ext_s[:HALO, :] = jnp.zeros((HALO, D), F32)

    @pl.when(i > 0)
    def _():
        ext_s[:HALO, :] = prev_ref[0]

    base = HALO - (CONV_WIDTH - 1)
    offs = range(base, base + CONV_WIDTH)
    for r in range(1, SUBLANES):
        span = max(o for o in offs if o % SUBLANES == r) - r + tl
        sh_s[r, :span, :] = ext_s[r:r + span, :]

    def rows(t, carry):
        row0 = pl.multiple_of(t * CONV_ROWS, CONV_ROWS)
        n_sub = CONV_ROWS // SUBLANES
        y = [None] * n_sub
        for o in offs:
            r = o % SUBLANES
            start = pl.multiple_of(row0 + (o - r), SUBLANES)
            x = ext_s[pl.ds(start, CONV_ROWS), :] if r == 0 else sh_s[r, pl.ds(start, CONV_ROWS), :]
            w = wb_s[o - base]
            for k in range(n_sub):
                term = x[k * SUBLANES:(k + 1) * SUBLANES] * w
                y[k] = term if y[k] is None else y[k] + term
        y_s[pl.ds(row0, CONV_ROWS), :] = jnp.concatenate(y, axis=0)
        return carry

    lax.fori_loop(0, tl // CONV_ROWS, rows, 0)

    y = y_s[...] + b_ref[...]
    mu = jnp.mean(y, axis=-1, keepdims=True)
    d = y - mu
    var = jnp.mean(d * d, axis=-1, keepdims=True)
    z = d * lax.rsqrt(var + EPS) * lg_ref[...] + lb_ref[...]
    o_ref[0] = (z * jax.nn.sigmoid(z)).astype(BF16)


def _conv_ln(a3, w_dw, b_dw, ln_g, ln_b, tl=256):
    B, L, D = a3.shape
    r = tl // HALO
    row = lambda v: v.reshape(1, D)
    return pl.pallas_call(
        functools.partial(_conv_kernel, tl=tl),
        grid=(B, L // tl),
        in_specs=[pl.BlockSpec((1, tl, D), lambda b, i: (b, i, 0)),
                  pl.BlockSpec((1, HALO, D), lambda b, i: (b, jnp.maximum(i * r - 1, 0), 0)),
                  pl.BlockSpec((CONV_WIDTH, D), lambda b, i: (0, 0)),
                  pl.BlockSpec((1, D), lambda b, i: (0, 0)),
                  pl.BlockSpec((1, D), lambda b, i: (0, 0)),
                  pl.BlockSpec((1, D), lambda b, i: (0, 0))],
        out_specs=pl.BlockSpec((1, tl, D), lambda b, i: (b, i, 0)),
        out_shape=jax.ShapeDtypeStruct((B, L, D), BF16),
        scratch_shapes=[pltpu.VMEM((tl + HALO, D), F32),
                        pltpu.VMEM((SUBLANES, tl + HALO, D), F32),
                        pltpu.VMEM((CONV_WIDTH, SUBLANES, D), F32),
                        pltpu.VMEM((tl, D), F32)],
        compiler_params=_cparams(("arbitrary", "arbitrary")),
        name="conv_ln",
    )(a3, a3, w_dw, row(b_dw), row(ln_g), row(ln_b))


def _peer_score_kernel(x_ref, g_ref, whi_ref, wlo_ref, k3_ref, h_ref, st_ref, hi_s, lo_s):
    @pl.when(pl.program_id(1) == 0)
    def _():
        hi, lo = _split(_rms(x_ref[...], g_ref[...]))
        hi_s[...] = hi
        lo_s[...] = lo
        h_ref[...] = hi

    q = _dot3(hi_s[...], lo_s[...], whi_ref[...], wlo_ref[...])
    q_hi, q_lo = _split(q)
    half = P_QDIM // 2
    for s in range(q.shape[1] // half):
        cols = slice(s * half, (s + 1) * half)
        q3 = jnp.concatenate([q_hi[:, cols], q_lo[:, cols], q_hi[:, cols]], axis=1)
        st_ref[s // 2, s % 2] = lax.dot_general(k3_ref[s % 2], q3, NT_DIMS, preferred_element_type=F32)


def _peer_scores(x, g, wq, keys, tm=512, hps=4):
    T, D = x.shape
    whi, wlo = _split(wq)
    k_hi, k_lo = _split(keys)
    k3 = jnp.concatenate([k_hi, k_hi, k_lo], axis=-1)
    return pl.pallas_call(
        _peer_score_kernel,
        grid=(T // tm, P_HEADS // hps),
        in_specs=[pl.BlockSpec((tm, D), lambda i, j: (i, 0)),
                  pl.BlockSpec((1, D), lambda i, j: (0, 0)),
                  pl.BlockSpec((D, hps * P_QDIM), lambda i, j: (0, j)),
                  pl.BlockSpec((D, hps * P_QDIM), lambda i, j: (0, j)),
                  pl.BlockSpec(k3.shape, lambda i, j: (0, 0, 0))],
        out_specs=[pl.BlockSpec((tm, D), lambda i, j: (i, 0)),
                   pl.BlockSpec((hps, 2, P_NKEYS, tm), lambda i, j: (j, 0, 0, i))],
        out_shape=[jax.ShapeDtypeStruct((T, D), BF16),
                   jax.ShapeDtypeStruct((P_HEADS, 2, P_NKEYS, T), F32)],
        scratch_shapes=[pltpu.VMEM((tm, D), BF16), pltpu.VMEM((tm, D), BF16)],
        compiler_params=_cparams(("parallel", "arbitrary")),
        name="peer_scores",
    )(x, g.reshape(1, D), whi, wlo, k3)


def _take_top(s, n):
    rows = []
    for _ in range(n):
        mx = jnp.max(s, axis=0, keepdims=True)
        rows.append(mx)
        s = jnp.where(s == mx, -jnp.inf, s)
    return rows, s


def _take_top_ranked(s, n):
    rows = []
    rank = jnp.full(s.shape, float(n), F32)
    for r in range(n):
        mx = jnp.max(s, axis=0, keepdims=True)
        rows.append(mx)
        hit = s == mx
        rank = jnp.where(hit, float(r), rank)
        s = jnp.where(hit, -jnp.inf, s)
    return rows, rank


def _peer_topk_kernel(st_ref, rank2_ref, n1_ref, e2_ref, coef_ref):
    n = P_TOPK
    tt = st_ref.shape[-1]

    def head(h, carry):
        s1 = st_ref[h, 0]
        s2 = st_ref[h, 1]
        r1, _ = _take_top(s1, n)
        r2, rank2 = _take_top_ranked(s2, n)
        v2 = jnp.concatenate(r2, axis=0)
        blocks = [r1[0] + v2]
        rows8 = lax.broadcasted_iota(I32, (SUBLANES, tt), 0)
        for i in range(1, SUBLANES):
            blocks.append(jnp.where(rows8 < n // (i + 1), r1[i] + v2[:SUBLANES], -jnp.inf))
        blocks.append(jnp.concatenate(r1[SUBLANES:], axis=0) + r2[0])
        top, _ = _take_top(jnp.concatenate(blocks, axis=0), n)
        kth = top[-1]
        z = None
        for r in top:
            e = jnp.exp(r - top[0])
            z = e if z is None else z + e
        hits = [jnp.where(blk >= kth, 1.0, 0.0) for blk in blocks]
        per_rank = [jnp.sum(hb, axis=0, keepdims=True) for hb in hits[:SUBLANES]]
        per_rank += [hits[SUBLANES][k:k + 1] for k in range(n - SUBLANES)]
        n1 = jnp.zeros(s1.shape, F32)
        for i in range(n):
            n1 = jnp.where(s1 == r1[i], per_rank[i], n1)
        rank2_ref[h] = rank2.astype(BF16)
        n1_ref[h] = n1
        e2_ref[h] = jnp.exp(s2 - r2[0]).astype(BF16)
        coef_ref[h] = jnp.exp(s1 - r1[0]) * (0.5 / z)
        return carry

    lax.fori_loop(0, P_HEADS, head, 0)


def _peer_topk(st, tt=256):
    T = st.shape[-1]
    spec = pl.BlockSpec((P_HEADS, P_NKEYS, tt), lambda i: (0, 0, i))
    shape = lambda dt: jax.ShapeDtypeStruct((P_HEADS, P_NKEYS, T), dt)
    return pl.pallas_call(
        _peer_topk_kernel,
        grid=(T // tt,),
        in_specs=[pl.BlockSpec((P_HEADS, 2, P_NKEYS, tt), lambda i: (0, 0, 0, i))],
        out_specs=[spec, spec, spec, spec],
        out_shape=[shape(BF16), shape(F32), shape(BF16), shape(F32)],
        compiler_params=_cparams(("parallel",)),
        name="peer_topk",
    )(st)


TILE_ROWS = 32
PACKED_ROWS = 16


def _peer_dense_kernel(h_ref, u_ref, vt_ref, rank_ref, e2_ref, n1_ref, coef_ref, res_ref, fg_ref, o_ref,
                       bn_s, bcoef_s, acc_s, *, te, tt, final):
    j = pl.program_id(1)
    NK = P_NKEYS

    @pl.when(j == 0)
    def _():
        acc_s[...] = jnp.zeros(acc_s.shape, F32)

    a0 = pl.multiple_of(j * SUBLANES, SUBLANES)
    reps = TILE_ROWS // PACKED_ROWS
    for h in range(P_HEADS):
        n8 = n1_ref[h, pl.ds(a0, SUBLANES), :]
        coef8 = coef_ref[h, pl.ds(a0, SUBLANES), :]
        for al in range(SUBLANES):
            bn_s[h, al] = jnp.broadcast_to(n8[al:al + 1], (PACKED_ROWS, tt)).astype(BF16)
            bcoef_s[h, al] = jnp.broadcast_to(coef8[al:al + 1], (PACKED_ROWS, tt)).astype(BF16)

    hb = lax.dot_general(u_ref[...], h_ref[...], NT_DIMS, preferred_element_type=F32)
    acts = []
    for al in range(SUBLANES):
        for b0 in range(0, NK, TILE_ROWS):
            sub = slice(b0, b0 + TILE_ROWS)
            g = None
            for h in range(P_HEADS):
                pairs = jnp.concatenate([bn_s[h, al]] * reps, axis=0)
                coef = jnp.concatenate([bcoef_s[h, al]] * reps, axis=0)
                e2 = e2_ref[h, sub, :]
                t = jnp.where(rank_ref[h, sub, :] < pairs, e2, jnp.zeros_like(e2)) * coef
                g = t if g is None else g + t
            x = hb[al * NK + b0:al * NK + b0 + TILE_ROWS, :]
            acts.append((x * (1.0 + lax.erf(x * (2.0 ** -0.5)))).astype(BF16) * g)
    act = jnp.concatenate(acts, axis=0)
    acc_s[...] += jnp.dot(vt_ref[...], act, preferred_element_type=F32)

    @pl.when(j == pl.num_programs(1) - 1)
    def _():
        y = res_ref[...] + acc_s[...].T
        if final:
            y = _rms(y, fg_ref[...])
        o_ref[...] = y


def _peer_dense(hn, u, vt, sel, res, fgain, final, tt=512):
    T, D = res.shape
    E = u.shape[0]
    te = SUBLANES * P_NKEYS
    n_tiles = E // te
    kern = functools.partial(_peer_dense_kernel, te=te, tt=tt, final=final)
    per_token = pl.BlockSpec((P_HEADS, P_NKEYS, tt), lambda i, j: (0, 0, i))
    return pl.pallas_call(
        kern,
        grid=(T // tt, n_tiles),
        in_specs=[pl.BlockSpec((tt, D), lambda i, j: (i, 0)),
                  pl.BlockSpec((te, D), lambda i, j: (j, 0)),
                  pl.BlockSpec((D, te), lambda i, j: (0, j)),
                  per_token, per_token, per_token, per_token,
                  pl.BlockSpec((tt, D), lambda i, j: (i, 0)),
                  pl.BlockSpec((1, D), lambda i, j: (0, 0))],
        out_specs=pl.BlockSpec((tt, D), lambda i, j: (i, 0)),
        out_shape=jax.ShapeDtypeStruct((T, D), F32),
        scratch_shapes=[pltpu.VMEM((P_HEADS, SUBLANES, PACKED_ROWS, tt), BF16),
                        pltpu.VMEM((P_HEADS, SUBLANES, PACKED_ROWS, tt), BF16),
                        pltpu.VMEM((D, tt), F32)],
        compiler_params=_cparams(("parallel", "arbitrary")),
        name="peer_dense",
    )(hn, u, vt, *sel, res, fgain.reshape(1, D))


def _peer(x, g, wq, keys, u, v, fgain, final):
    hn, st = _peer_scores(x, g, wq, keys)
    rank2, n1, e2, coef = _peer_topk(st)
    return _peer_dense(hn, u.astype(BF16), v.T.astype(BF16), (rank2, e2, n1, coef), x, fgain, final)


def _block_diag(blocks):
    H, r, c = blocks.shape
    eye = jnp.eye(H, dtype=blocks.dtype)
    return (eye[:, None, :, None] * blocks[:, :, None, :]).reshape(H * r, H * c)


def _in_proj_weights(w_in):
    sizes = (A_HEADS * A_HEAD_DIM, A_LATENT, IDX_HEADS * IDX_DIM, IDX_DIM, IDX_HEADS,
             R_HEADS * R_DK, R_HEADS * R_DK, R_HEADS * R_DV, R_HEADS * R_DV)
    pts = np.cumsum(sizes)[:-1].tolist()
    qa, ckv, qi, ki, wi, rq, rk, rv, rg = jnp.split(w_in, pts, axis=1)
    pad = jnp.zeros((w_in.shape[0], LANES - IDX_DIM - IDX_HEADS), w_in.dtype)
    return (jnp.concatenate([rq, rk, rv, rg, qa, ckv], axis=1),
            jnp.concatenate([qi, ki, wi, pad], axis=1))


def kernel(x, mix_norm_e, w_in, kv_norm, w_uk, w_uv, w_o, mix_norm_o, conv_w1, conv_b1, conv_dw, conv_dw_b, conv_ln_g, conv_ln_b, conv_w2, conv_b2, ffn_norm, peer_wq, peer_keys, peer_u, peer_v, final_norm):
    B, L, D = x.shape
    T = B * L
    depth = ffn_norm.shape[0]
    xf = x.reshape(T, D)
    for layer in range(depth):
        j = layer // 2
        if layer % 2 == 0:
            proj, idx = _in_proj(xf, mix_norm_e[j], *_in_proj_weights(w_in[j]))
            proj3 = proj.reshape(B, L, proj.shape[1])
            idx3 = idx.reshape(B, L, idx.shape[1])
            wuk_bd = _block_diag(jnp.swapaxes(w_uk[j], 1, 2)).astype(BF16)
            wuv_bd = _block_diag(w_uv[j]).astype(BF16)
            a_out = _dsa(proj3, idx3, kv_norm[j], wuk_bd, wuv_bd)
            b_out = _retention(proj3)
            wo = w_o[j].astype(BF16)
            na = A_HEADS * A_HEAD_DIM
            xf = _mm_res([a_out.reshape(T, na), b_out.reshape(T, -1)], [wo[:na], wo[na:]], xf)
        else:
            a = _norm_glu(xf, mix_norm_o[j], conv_w1[j].astype(BF16), conv_b1[j])
            y = _conv_ln(a.reshape(B, L, D), conv_dw[j], conv_dw_b[j], conv_ln_g[j], conv_ln_b[j])
            xf = _mm_res([y.reshape(T, D)], [conv_w2[j].astype(BF16)], xf, bias=conv_b2[j])
        xf = _peer(xf, ffn_norm[layer], peer_wq[layer], peer_keys[layer], peer_u[layer], peer_v[layer],
                   final_norm, final=(layer == depth - 1))
    return xf.reshape(B, L, D)
```

```python
import functools

import jax
import jax.numpy as jnp
import numpy as np
from jax import lax
from jax.experimental import pallas as pl
from jax.experimental.pallas import tpu as pltpu

F32 = jnp.float32
BF16 = jnp.bfloat16
I32 = jnp.int32

EPS = 1e-6
A_HEADS, A_HEAD_DIM, A_LATENT = 8, 64, 128
IDX_HEADS, IDX_DIM, TOPK_MAX = 8, 32, 256
R_HEADS, R_DK, R_DV, R_CHUNK = 4, 128, 128, 128
CONV_WIDTH = 31
P_HEADS, P_NKEYS, P_QDIM, P_TOPK = 8, 128, 256, 16

LANES = 128
SUBLANES = 8
VMEM_LIMIT = 56 * 1024 * 1024
INT_MIN = -(2 ** 31)
NEG_BIG = -1e30
LOG2_E = 1.4426950408889634

NT_DIMS = (((1,), (1,)), ((), ()))


def _cparams(sem, flags=None):
    return pltpu.CompilerParams(dimension_semantics=sem, vmem_limit_bytes=VMEM_LIMIT, flags=flags)


def _aligned(x, m):
    if isinstance(x, int):
        assert x % m == 0
        return x
    return pl.multiple_of(x, m)


def _rms(x, g):
    return x * lax.rsqrt(jnp.mean(x * x, axis=-1, keepdims=True) + EPS) * g


def _split(x):
    hi = x.astype(BF16)
    return hi, (x - hi.astype(F32)).astype(BF16)


def _dot3(a_hi, a_lo, b_hi, b_lo):
    d = functools.partial(jnp.dot, preferred_element_type=F32)
    return d(a_hi, b_hi) + (d(a_hi, b_lo) + d(a_lo, b_hi))


def _in_proj_kernel(x_ref, g_ref, w_ref, whi_ref, wlo_ref, o_ref, oi_ref):
    h = _rms(x_ref[...], g_ref[...])
    hi, lo = _split(h)
    o_ref[...] = jnp.dot(hi, w_ref[...], preferred_element_type=F32)
    oi_ref[...] = _dot3(hi, lo, whi_ref[...], wlo_ref[...])


def _in_proj(x, g, w_main, w_idx, tm=512):
    T, D = x.shape
    N, NI = w_main.shape[1], w_idx.shape[1]
    whi, wlo = _split(w_idx)
    return pl.pallas_call(
        _in_proj_kernel,
        grid=(T // tm,),
        in_specs=[pl.BlockSpec((tm, D), lambda i: (i, 0)),
                  pl.BlockSpec((1, D), lambda i: (0, 0)),
                  pl.BlockSpec((D, N), lambda i: (0, 0)),
                  pl.BlockSpec((D, NI), lambda i: (0, 0)),
                  pl.BlockSpec((D, NI), lambda i: (0, 0))],
        out_specs=[pl.BlockSpec((tm, N), lambda i: (i, 0)),
                   pl.BlockSpec((tm, NI), lambda i: (i, 0))],
        out_shape=[jax.ShapeDtypeStruct((T, N), F32), jax.ShapeDtypeStruct((T, NI), F32)],
        compiler_params=_cparams(("parallel",)),
        name="in_proj",
    )(x, g.reshape(1, D), w_main.astype(BF16), whi, wlo)


def _norm_glu_kernel(x_ref, g_ref, wa_ref, wg_ref, ba_ref, bg_ref, o_ref, h_scr):
    @pl.when(pl.program_id(1) == 0)
    def _():
        h_scr[...] = _rms(x_ref[...], g_ref[...]).astype(BF16)

    h = h_scr[...]
    a = jnp.dot(h, wa_ref[...], preferred_element_type=F32) + ba_ref[...]
    gate = jnp.dot(h, wg_ref[...], preferred_element_type=F32) + bg_ref[...]
    o_ref[...] = a * jax.nn.sigmoid(gate)


def _norm_glu(x, g, w1, b1, tm=512, tn=512):
    T, D = x.shape
    nj = D // tn
    b1 = b1.reshape(1, 2 * D)
    return pl.pallas_call(
        _norm_glu_kernel,
        grid=(T // tm, nj),
        in_specs=[pl.BlockSpec((tm, D), lambda i, j: (i, 0)),
                  pl.BlockSpec((1, D), lambda i, j: (0, 0)),
                  pl.BlockSpec((D, tn), lambda i, j: (0, j)),
                  pl.BlockSpec((D, tn), lambda i, j: (0, j + nj)),
                  pl.BlockSpec((1, tn), lambda i, j: (0, j)),
                  pl.BlockSpec((1, tn), lambda i, j: (0, j + nj))],
        out_specs=pl.BlockSpec((tm, tn), lambda i, j: (i, j)),
        out_shape=jax.ShapeDtypeStruct((T, D), F32),
        scratch_shapes=[pltpu.VMEM((tm, D), BF16)],
        compiler_params=_cparams(("parallel", "arbitrary")),
        name="norm_glu",
    )(x, g.reshape(1, D), w1, w1, b1, b1)


def _mm_res_kernel(*refs, n_lhs, has_bias):
    lhs = refs[:n_lhs]
    ws = refs[n_lhs:2 * n_lhs]
    rest = refs[2 * n_lhs:]
    if has_bias:
        b_ref, res_ref, o_ref = rest
        acc = res_ref[...] + b_ref[...]
    else:
        res_ref, o_ref = rest
        acc = res_ref[...]
    for a_ref, w_ref in zip(lhs, ws):
        acc = acc + jnp.dot(a_ref[...], w_ref[...], preferred_element_type=F32)
    o_ref[...] = acc


def _mm_res(lhs, ws, res, bias=None, tm=512):
    T, N = res.shape
    n = len(lhs)
    in_specs = [pl.BlockSpec((tm, a.shape[1]), lambda i: (i, 0)) for a in lhs]
    in_specs += [pl.BlockSpec(w.shape, lambda i: (0, 0)) for w in ws]
    args = list(lhs) + list(ws)
    if bias is not None:
        in_specs.append(pl.BlockSpec((1, N), lambda i: (0, 0)))
        args.append(bias.reshape(1, N))
    in_specs.append(pl.BlockSpec((tm, N), lambda i: (i, 0)))
    args.append(res)
    return pl.pallas_call(
        functools.partial(_mm_res_kernel, n_lhs=n, has_bias=bias is not None),
        grid=(T // tm,),
        in_specs=in_specs,
        out_specs=pl.BlockSpec((tm, N), lambda i: (i, 0)),
        out_shape=jax.ShapeDtypeStruct((T, N), F32),
        compiler_params=_cparams(("parallel",)),
        name="mm_res",
    )(*args)


def _dsa_kernel(qa_ref, qi_ref, misc_ref, ckv_ref, kit_ref, kvn_ref, wuk_ref, wuv_ref, o_ref,
                c_s, key_s, qst_s, qlat_s, wb_s, lg_s, bias_s, cmax_s, p_s, m_s, l_s, acc_s,
                *, Q, KC, topk, col_bits):
    qb = pl.program_id(1)
    H = A_HEADS
    NL = KC // LANES

    @pl.when(qb == 0)
    def _():
        c_s[...] = _rms(ckv_ref[0], kvn_ref[...]).astype(BF16)

    qi = qi_ref[0]
    wi = misc_ref[0][:, IDX_DIM:IDX_DIM + IDX_HEADS] * (IDX_HEADS ** -0.5)
    for h in range(IDX_HEADS):
        hi, lo = _split(qi[:, h * IDX_DIM:(h + 1) * IDX_DIM])
        qst_s[h * Q:(h + 1) * Q, :] = jnp.concatenate([hi, lo, hi], axis=1)
        wb_s[h] = jnp.broadcast_to(wi[:, h:h + 1], (Q, LANES))
    ql = jnp.dot(qa_ref[0].astype(BF16), wuk_ref[...], preferred_element_type=F32) * (A_HEAD_DIM ** -0.5 * LOG2_E)
    for h in range(H):
        qlat_s[h * Q:(h + 1) * Q, :] = ql[:, h * A_LATENT:(h + 1) * A_LATENT].astype(BF16)

    n_chunks = ((qb + 1) * Q + KC - 1) // KC
    q_pos = qb * Q + lax.broadcasted_iota(I32, (Q, LANES), 0)
    lane = lax.broadcasted_iota(I32, (Q, LANES), 1)

    RG = 4 * SUBLANES
    lane_rg = lax.broadcasted_iota(I32, (RG, LANES), 1)
    row_rg = lax.broadcasted_iota(I32, (RG, LANES), 0)

    def score_chunk(c, carry):
        off = pl.multiple_of(c * KC, KC)
        lg_s[...] = jnp.dot(qst_s[...], kit_ref[0, :, pl.ds(off, KC)], preferred_element_type=F32)
        for r0 in range(0, Q, RG):
            a = None
            for h in range(IDX_HEADS):
                w = jnp.concatenate([wb_s[h, r0:r0 + RG, :]] * NL, axis=1)
                t = jnp.maximum(lg_s[h * Q + r0:h * Q + r0 + RG, :], 0.0) * w
                a = t if a is None else a + t
            bits = pltpu.bitcast(a, I32)
            sgn = bits >> 31
            key = (bits ^ (sgn & 0x7FFFFFFF)) - sgn
            for l in range(NL):
                causal = off + l * LANES + lane_rg <= qb * Q + r0 + row_rg
                key_s[c * NL + l, r0:r0 + RG, :] = jnp.where(causal, key[:, l * LANES:(l + 1) * LANES], INT_MIN)
        return carry

    lax.fori_loop(0, n_chunks, score_chunk, 0)

    SR = min(Q, 16 * SUBLANES)
    lane_sr = lax.broadcasted_iota(I32, (SR, LANES), 1)

    def row_count(pred):
        cnts = []
        for r0 in range(0, Q, SR):
            def body(c, cnt, r0=r0):
                for l in range(NL):
                    blk = key_s[c * NL + l, r0:r0 + SR, :]
                    cnt = cnt + jnp.where(pred(blk, r0, c * KC + l * LANES), 1, 0)
                return cnt

            cnts.append(lax.fori_loop(0, n_chunks, body, jnp.zeros((SR, LANES), I32)))
        return jnp.sum(jnp.concatenate(cnts, axis=0).astype(F32), axis=1, keepdims=True)

    def bit_body(i, carry):
        tau, n_at = carry
        cand = tau | lax.shift_left(jnp.int32(1), 31 - i)
        cand_s = cand ^ INT_MIN
        tot = row_count(lambda blk, r0, col0: blk >= cand_s[r0:r0 + SR])
        take = tot >= topk
        return jnp.where(take, cand, tau), jnp.where(take, tot, n_at)

    zero = jnp.zeros((Q, LANES), I32)
    tau, n_ge = lax.fori_loop(0, 32, bit_body, (zero, jnp.zeros((Q, 1), F32)))
    thr = jnp.maximum(tau ^ INT_MIN, INT_MIN + 1)

    excess = n_ge > topk
    cmax_s[...] = jnp.full((Q, LANES), 2 ** 31 - 1, I32)

    @pl.when(jnp.max(jnp.where(excess, 1.0, 0.0)) > 0.0)
    def _():
        need = topk - row_count(lambda blk, r0, col0: blk > thr[r0:r0 + SR])

        def col_bit(i, y):
            cand = y | lax.shift_left(jnp.int32(1), col_bits - 1 - i)
            tot = row_count(lambda blk, r0, col0:
                            (blk == thr[r0:r0 + SR]) & (col0 + lane_sr < cand[r0:r0 + SR]))
            return jnp.where(tot < need, cand, y)

        y = lax.fori_loop(0, col_bits, col_bit, zero)
        cmax_s[...] = jnp.where(excess, y, 2 ** 31 - 1)

    cmax = cmax_s[...]
    m_s[...] = jnp.full(m_s.shape, NEG_BIG, F32)
    l_s[...] = jnp.zeros(l_s.shape, F32)
    acc_s[...] = jnp.zeros(acc_s.shape, F32)

    def att_chunk(c, carry):
        off = pl.multiple_of(c * KC, KC)
        ck = c_s[pl.ds(off, KC), :]
        lg_s[...] = lax.dot_general(qlat_s[...], ck, NT_DIMS, preferred_element_type=F32)
        for l in range(NL):
            blk = key_s[c * NL + l]
            sel = (blk > thr) | ((blk == thr) & (off + l * LANES + lane <= cmax))
            bias_s[l] = jnp.where(sel, 0.0, NEG_BIG)

        for h in range(H):
            for r0 in range(0, Q, RG):
                rows = slice(h * Q + r0, h * Q + r0 + RG)
                m_prev = m_s[rows, :]
                lg = lg_s[rows, :]
                lgs = [lg[:, l * LANES:(l + 1) * LANES] + bias_s[l, r0:r0 + RG, :] for l in range(NL)]
                mx = lgs[0]
                for l in range(1, NL):
                    mx = jnp.maximum(mx, lgs[l])
                m_new = jnp.maximum(m_prev, jnp.max(mx, axis=1, keepdims=True))
                alpha = jnp.exp2(m_prev - m_new)
                ps = [jnp.exp2(x - m_new) for x in lgs]
                p_s[rows, :] = jnp.concatenate(ps, axis=1).astype(BF16)
                psum = ps[0]
                for l in range(1, NL):
                    psum = psum + ps[l]
                l_s[rows, :] = alpha * l_s[rows, :] + psum
                m_s[rows, :] = m_new
                acc_s[rows, :] = acc_s[rows, :] * alpha
        acc_s[...] += jnp.dot(p_s[...], ck, preferred_element_type=F32)
        return carry

    lax.fori_loop(0, n_chunks, att_chunk, 0)

    o = acc_s[...] / jnp.sum(l_s[...], axis=1, keepdims=True)
    o_cat = jnp.concatenate([o[h * Q:(h + 1) * Q, :] for h in range(H)], axis=1).astype(BF16)
    o_ref[0] = jnp.dot(o_cat, wuv_ref[...], preferred_element_type=F32).astype(BF16)


def _dsa(proj3, idx3, kv_norm, wuk_bd, wuv_bd, Q=256, KC=512):
    B, L, _ = proj3.shape
    H = A_HEADS
    topk = min(TOPK_MAX, L // 4)
    k_hi, k_lo = _split(jnp.swapaxes(idx3[:, :, 256:256 + IDX_DIM], 1, 2))
    kit = jnp.concatenate([k_hi, k_hi, k_lo], axis=1)
    kern = functools.partial(_dsa_kernel, Q=Q, KC=KC, topk=topk, col_bits=max(1, (L - 1).bit_length()))
    return pl.pallas_call(
        kern,
        grid=(B, L // Q),
        in_specs=[pl.BlockSpec((1, Q, 512), lambda b, q: (b, q, 4)),
                  pl.BlockSpec((1, Q, 256), lambda b, q: (b, q, 0)),
                  pl.BlockSpec((1, Q, 128), lambda b, q: (b, q, 2)),
                  pl.BlockSpec((1, L, 128), lambda b, q: (b, 0, 20)),
                  pl.BlockSpec((1, 3 * IDX_DIM, L), lambda b, q: (b, 0, 0)),
                  pl.BlockSpec((1, A_LATENT), lambda b, q: (0, 0)),
                  pl.BlockSpec(wuk_bd.shape, lambda b, q: (0, 0)),
                  pl.BlockSpec(wuv_bd.shape, lambda b, q: (0, 0))],
        out_specs=pl.BlockSpec((1, Q, H * A_HEAD_DIM), lambda b, q: (b, q, 0)),
        out_shape=jax.ShapeDtypeStruct((B, L, H * A_HEAD_DIM), BF16),
        scratch_shapes=[pltpu.VMEM((L, A_LATENT), BF16),
                        pltpu.VMEM((L // LANES, Q, LANES), I32),
                        pltpu.VMEM((IDX_HEADS * Q, 3 * IDX_DIM), BF16),
                        pltpu.VMEM((H * Q, A_LATENT), BF16),
                        pltpu.VMEM((IDX_HEADS, Q, LANES), F32),
                        pltpu.VMEM((H * Q, KC), F32),
                        pltpu.VMEM((KC // LANES, Q, LANES), F32),
                        pltpu.VMEM((Q, LANES), I32),
                        pltpu.VMEM((H * Q, KC), BF16),
                        pltpu.VMEM((H * Q, LANES), F32),
                        pltpu.VMEM((H * Q, LANES), F32),
                        pltpu.VMEM((H * Q, A_LATENT), F32)],
        compiler_params=_cparams(("arbitrary", "arbitrary")),
        name="dsa",
    )(proj3, idx3, idx3, proj3, kit, kv_norm.reshape(1, A_LATENT), wuk_bd, wuv_bd)


def _ret_kernel(q_ref, k_ref, v_ref, g_ref, cos_ref, sin_ref, dec_ref, xi_ref, zeta_ref, gch_ref,
                o_ref, st_s, *, B):
    @pl.when(pl.program_id(0) == 0)
    def _():
        st_s[...] = jnp.zeros(st_s.shape, F32)

    C = R_CHUNK
    cos = cos_ref[...]
    sin = sin_ref[...]
    even = (lax.broadcasted_iota(I32, (C, R_DK), 1) & 1) == 0

    def rot(x):
        partner = jnp.where(even, pltpu.roll(x, R_DK - 1, 1), pltpu.roll(x, 1, 1))
        return x * cos + partner * sin

    for b in range(B):
        for h in range(R_HEADS):
            sl = slice(h * R_DK, (h + 1) * R_DK)
            q = rot(q_ref[b, :, sl])
            k = rot(k_ref[b, :, sl]) * (R_DK ** -0.5)
            vb = v_ref[b, :, sl].astype(BF16)
            qb = q.astype(BF16)
            s = lax.dot_general(qb, k.astype(BF16), NT_DIMS, preferred_element_type=F32) * dec_ref[h]
            inner = jnp.dot(s.astype(BF16), vb, preferred_element_type=F32)
            st = st_s[b, h]
            cross = jnp.dot(qb, st.astype(BF16), preferred_element_type=F32) * xi_ref[h]
            kz = (k * zeta_ref[h]).astype(BF16)
            st_s[b, h] = st * gch_ref[h] + jnp.dot(kz.T, vb, preferred_element_type=F32)
            out = inner + cross
            mu = jnp.mean(out, axis=-1, keepdims=True)
            d = out - mu
            var = jnp.mean(d * d, axis=-1, keepdims=True)
            y = d * lax.rsqrt(var + EPS)
            gate = g_ref[b, :, sl]
            o_ref[b, :, sl] = (gate * jax.nn.sigmoid(gate) * y).astype(BF16)


def _retention(proj3):
    B, L, _ = proj3.shape
    C, H = R_CHUNK, R_HEADS
    N = L // C
    d2 = R_DK // 2
    inv_freq = 1.0 / (10000.0 ** jnp.linspace(0.0, 1.0, d2, dtype=F32))
    ang = jnp.arange(L, dtype=F32)[:, None] * inv_freq[None, :]
    cos = jnp.repeat(jnp.cos(ang), 2, axis=1)
    sin = jnp.stack([-jnp.sin(ang), jnp.sin(ang)], axis=-1).reshape(L, R_DK)
    log_g = jnp.log1p(-jnp.exp2(-5.0 - jnp.arange(H, dtype=F32)))
    pos = jnp.arange(C, dtype=F32)
    diff = pos[:, None] - pos[None, :]
    decay = jnp.where(diff >= 0, jnp.exp(jnp.maximum(diff, 0.0)[None] * log_g[:, None, None]), 0.0)
    xi = jnp.broadcast_to(jnp.exp((pos + 1.0)[None, :] * log_g[:, None])[:, :, None], (H, C, R_DV))
    zeta = jnp.broadcast_to(jnp.exp((C - 1.0 - pos)[None, :] * log_g[:, None])[:, :, None], (H, C, R_DK))
    gch = jnp.broadcast_to(jnp.exp(C * log_g)[:, None, None], (H, 1, R_DV))
    W = H * R_DK
    colspec = lambda j: pl.BlockSpec((B, C, W), lambda n: (0, n, j))
    cst = lambda shape: pl.BlockSpec(shape, lambda n: (0,) * len(shape))
    return pl.pallas_call(
        functools.partial(_ret_kernel, B=B),
        grid=(N,),
        in_specs=[colspec(0), colspec(1), colspec(2), colspec(3),
                  pl.BlockSpec((C, R_DK), lambda n: (n, 0)),
                  pl.BlockSpec((C, R_DK), lambda n: (n, 0)),
                  cst((H, C, C)), cst((H, C, R_DV)), cst((H, C, R_DK)), cst((H, 1, R_DV))],
        out_specs=pl.BlockSpec((B, C, W), lambda n: (0, n, 0)),
        out_shape=jax.ShapeDtypeStruct((B, L, W), BF16),
        scratch_shapes=[pltpu.VMEM((B, H, R_DK, R_DV), F32)],
        compiler_params=_cparams(("arbitrary",)),
        name="retention",
    )(proj3, proj3, proj3, proj3, cos, sin, decay, xi, zeta, gch)


HALO = 32


CONV_ROWS = 4 * SUBLANES


def _conv_kernel(cur_ref, prev_ref, w_ref, b_ref, lg_ref, lb_ref, o_ref, ext_s, sh_s, wb_s, y_s, *, tl):
    i = pl.program_id(1)
    D = ext_s.shape[1]

    @pl.when((pl.program_id(0) == 0) & (i == 0))
    def _():
        for j in range(CONV_WIDTH):
            wb_s[j] = jnp.broadcast_to(w_ref[j:j + 1, :], (SUBLANES, D))

    ext_s[HALO:, :] = cur_ref[0]

    @pl.when(i == 0)
    def _():
        ext_s[:HALO, :] = jnp.zeros((HALO, D), F32)

    @pl.when(i > 0)
    def _():
        ext_s[:HALO, :] = prev_ref[0]

    base = HALO - (CONV_WIDTH - 1)
    offs = range(base, base + CONV_WIDTH)
    for r in range(1, SUBLANES):
        span = max(o for o in offs if o % SUBLANES == r) - r + tl
        sh_s[r, :span, :] = ext_s[r:r + span, :]

    def rows(t, carry):
        row0 = pl.multiple_of(t * CONV_ROWS, CONV_ROWS)
        n_sub = CONV_ROWS // SUBLANES
        y = [None] * n_sub
        for o in offs:
            r = o % SUBLANES
            start = pl.multiple_of(row0 + (o - r), SUBLANES)
            x = ext_s[pl.ds(start, CONV_ROWS), :] if r == 0 else sh_s[r, pl.ds(start, CONV_ROWS), :]
            w = wb_s[o - base]
            for k in range(n_sub):
                term = x[k * SUBLANES:(k + 1) * SUBLANES] * w
                y[k] = term if y[k] is None else y[k] + term
        y_s[pl.ds(row0, CONV_ROWS), :] = jnp.concatenate(y, axis=0)
        return carry

    lax.fori_loop(0, tl // CONV_ROWS, rows, 0)

    y = y_s[...] + b_ref[...]
    mu = jnp.mean(y, axis=-1, keepdims=True)
    d = y - mu
    var = jnp.mean(d * d, axis=-1, keepdims=True)
    z = d * lax.rsqrt(var + EPS) * lg_ref[...] + lb_ref[...]
    o_ref[0] = (z * jax.nn.sigmoid(z)).astype(BF16)


def _conv_ln(a3, w_dw, b_dw, ln_g, ln_b, tl=256):
    B, L, D = a3.shape
    r = tl // HALO
    row = lambda v: v.reshape(1, D)
    return pl.pallas_call(
        functools.partial(_conv_kernel, tl=tl),
        grid=(B, L // tl),
        in_specs=[pl.BlockSpec((1, tl, D), lambda b, i: (b, i, 0)),
                  pl.BlockSpec((1, HALO, D), lambda b, i: (b, jnp.maximum(i * r - 1, 0), 0)),
                  pl.BlockSpec((CONV_WIDTH, D), lambda b, i: (0, 0)),
                  pl.BlockSpec((1, D), lambda b, i: (0, 0)),
                  pl.BlockSpec((1, D), lambda b, i: (0, 0)),
                  pl.BlockSpec((1, D), lambda b, i: (0, 0))],
        out_specs=pl.BlockSpec((1, tl, D), lambda b, i: (b, i, 0)),
        out_shape=jax.ShapeDtypeStruct((B, L, D), BF16),
        scratch_shapes=[pltpu.VMEM((tl + HALO, D), F32),
                        pltpu.VMEM((SUBLANES, tl + HALO, D), F32),
                        pltpu.VMEM((CONV_WIDTH, SUBLANES, D), F32),
                        pltpu.VMEM((tl, D), F32)],
        compiler_params=_cparams(("arbitrary", "arbitrary")),
        name="conv_ln",
    )(a3, a3, w_dw, row(b_dw), row(ln_g), row(ln_b))


def _peer_score_kernel(x_ref, g_ref, whi_ref, wlo_ref, k3_ref, h_ref, st_ref, hi_s, lo_s):
    @pl.when(pl.program_id(1) == 0)
    def _():
        hi, lo = _split(_rms(x_ref[...], g_ref[...]))
        hi_s[...] = hi
        lo_s[...] = lo
        h_ref[...] = hi

    q = _dot3(hi_s[...], lo_s[...], whi_ref[...], wlo_ref[...])
    q_hi, q_lo = _split(q)
    half = P_QDIM // 2
    for s in range(q.shape[1] // half):
        cols = slice(s * half, (s + 1) * half)
        q3 = jnp.concatenate([q_hi[:, cols], q_lo[:, cols], q_hi[:, cols]], axis=1)
        st_ref[s // 2, s % 2] = lax.dot_general(k3_ref[s % 2], q3, NT_DIMS, preferred_element_type=F32)


def _peer_scores(x, g, wq, keys, tm=512, hps=4):
    T, D = x.shape
    whi, wlo = _split(wq)
    k_hi, k_lo = _split(keys)
    k3 = jnp.concatenate([k_hi, k_hi, k_lo], axis=-1)
    return pl.pallas_call(
        _peer_score_kernel,
        grid=(T // tm, P_HEADS // hps),
        in_specs=[pl.BlockSpec((tm, D), lambda i, j: (i, 0)),
                  pl.BlockSpec((1, D), lambda i, j: (0, 0)),
                  pl.BlockSpec((D, hps * P_QDIM), lambda i, j: (0, j)),
                  pl.BlockSpec((D, hps * P_QDIM), lambda i, j: (0, j)),
                  pl.BlockSpec(k3.shape, lambda i, j: (0, 0, 0))],
        out_specs=[pl.BlockSpec((tm, D), lambda i, j: (i, 0)),
                   pl.BlockSpec((hps, 2, P_NKEYS, tm), lambda i, j: (j, 0, 0, i))],
        out_shape=[jax.ShapeDtypeStruct((T, D), BF16),
                   jax.ShapeDtypeStruct((P_HEADS, 2, P_NKEYS, T), F32)],
        scratch_shapes=[pltpu.VMEM((tm, D), BF16), pltpu.VMEM((tm, D), BF16)],
        compiler_params=_cparams(("parallel", "arbitrary")),
        name="peer_scores",
    )(x, g.reshape(1, D), whi, wlo, k3)


def _take_top(s, n):
    rows = []
    for _ in range(n):
        mx = jnp.max(s, axis=0, keepdims=True)
        rows.append(mx)
        s = jnp.where(s == mx, -jnp.inf, s)
    return rows, s


def _take_top_ranked(s, n):
    rows = []
    rank = jnp.full(s.shape, float(n), F32)
    for r in range(n):
        mx = jnp.max(s, axis=0, keepdims=True)
        rows.append(mx)
        hit = s == mx
        rank = jnp.where(hit, float(r), rank)
        s = jnp.where(hit, -jnp.inf, s)
    return rows, rank


def _peer_topk_kernel(st_ref, rank2_ref, n1_ref, e2_ref, coef_ref):
    n = P_TOPK
    tt = st_ref.shape[-1]

    def head(h, carry):
        s1 = st_ref[h, 0]
        s2 = st_ref[h, 1]
        r1, _ = _take_top(s1, n)
        r2, rank2 = _take_top_ranked(s2, n)
        v2 = jnp.concatenate(r2, axis=0)
        blocks = [r1[0] + v2]
        rows8 = lax.broadcasted_iota(I32, (SUBLANES, tt), 0)
        for i in range(1, SUBLANES):
            blocks.append(jnp.where(rows8 < n // (i + 1), r1[i] + v2[:SUBLANES], -jnp.inf))
        blocks.append(jnp.concatenate(r1[SUBLANES:], axis=0) + r2[0])
        top, _ = _take_top(jnp.concatenate(blocks, axis=0), n)
        kth = top[-1]
        z = None
        for r in top:
            e = jnp.exp(r - top[0])
            z = e if z is None else z + e
        hits = [jnp.where(blk >= kth, 1.0, 0.0) for blk in blocks]
        per_rank = [jnp.sum(hb, axis=0, keepdims=True) for hb in hits[:SUBLANES]]
        per_rank += [hits[SUBLANES][k:k + 1] for k in range(n - SUBLANES)]
        n1 = jnp.zeros(s1.shape, F32)
        for i in range(n):
            n1 = jnp.where(s1 == r1[i], per_rank[i], n1)
        rank2_ref[h] = rank2.astype(BF16)
        n1_ref[h] = n1
        e2_ref[h] = jnp.exp(s2 - r2[0]).astype(BF16)
        coef_ref[h] = jnp.exp(s1 - r1[0]) * (0.5 / z)
        return carry

    lax.fori_loop(0, P_HEADS, head, 0)


def _peer_topk(st, tt=256):
    T = st.shape[-1]
    spec = pl.BlockSpec((P_HEADS, P_NKEYS, tt), lambda i: (0, 0, i))
    shape = lambda dt: jax.ShapeDtypeStruct((P_HEADS, P_NKEYS, T), dt)
    return pl.pallas_call(
        _peer_topk_kernel,
        grid=(T // tt,),
        in_specs=[pl.BlockSpec((P_HEADS, 2, P_NKEYS, tt), lambda i: (0, 0, 0, i))],
        out_specs=[spec, spec, spec, spec],
        out_shape=[shape(BF16), shape(F32), shape(BF16), shape(F32)],
        compiler_params=_cparams(("parallel",)),
        name="peer_topk",
    )(st)


TILE_ROWS = 32
PACKED_ROWS = 16
UP_ROWS = 2


def _peer_dense_kernel(h_ref, u_ref, vt_ref, rank_ref, e2_ref, n1_ref, coef_ref, res_ref, fg_ref, o_ref,
                       bn_s, bcoef_s, acc_s, *, te, tt, final):
    j = pl.program_id(1)
    NK = P_NKEYS

    @pl.when(j == 0)
    def _():
        acc_s[...] = jnp.zeros(acc_s.shape, F32)

    a0 = pl.multiple_of(j * SUBLANES, SUBLANES)
    reps = TILE_ROWS // PACKED_ROWS
    for h in range(P_HEADS):
        n8 = n1_ref[h, pl.ds(a0, SUBLANES), :]
        coef8 = coef_ref[h, pl.ds(a0, SUBLANES), :]
        for al in range(SUBLANES):
            bn_s[h, al] = jnp.broadcast_to(n8[al:al + 1], (PACKED_ROWS, tt)).astype(BF16)
            bcoef_s[h, al] = jnp.broadcast_to(coef8[al:al + 1], (PACKED_ROWS, tt)).astype(BF16)

    acts = []
    for al in range(SUBLANES):
        if al % UP_ROWS == 0:
            hb = lax.dot_general(u_ref[al * NK:(al + UP_ROWS) * NK, :], h_ref[...], NT_DIMS,
                                 preferred_element_type=F32)
        for b0 in range(0, NK, TILE_ROWS):
            sub = slice(b0, b0 + TILE_ROWS)
            g = None
            for h in range(P_HEADS):
                pairs = jnp.concatenate([bn_s[h, al]] * reps, axis=0)
                coef = jnp.concatenate([bcoef_s[h, al]] * reps, axis=0)
                e2 = e2_ref[h, sub, :]
                t = jnp.where(rank_ref[h, sub, :] < pairs, e2, jnp.zeros_like(e2)) * coef
                g = t if g is None else g + t
            r0 = (al % UP_ROWS) * NK + b0
            x = hb[r0:r0 + TILE_ROWS, :]
            acts.append((x * (1.0 + lax.erf(x * (2.0 ** -0.5)))).astype(BF16) * g)
    act = jnp.concatenate(acts, axis=0)
    acc_s[...] += jnp.dot(vt_ref[...], act, preferred_element_type=F32)

    @pl.when(j == pl.num_programs(1) - 1)
    def _():
        y = res_ref[...] + acc_s[...].T
        if final:
            y = _rms(y, fg_ref[...])
        o_ref[...] = y


def _peer_dense(hn, u, vt, sel, res, fgain, final, tt=512):
    T, D = res.shape
    E = u.shape[0]
    te = SUBLANES * P_NKEYS
    n_tiles = E // te
    kern = functools.partial(_peer_dense_kernel, te=te, tt=tt, final=final)
    per_token = pl.BlockSpec((P_HEADS, P_NKEYS, tt), lambda i, j: (0, 0, i))
    return pl.pallas_call(
        kern,
        grid=(T // tt, n_tiles),
        in_specs=[pl.BlockSpec((tt, D), lambda i, j: (i, 0)),
                  pl.BlockSpec((te, D), lambda i, j: (j, 0)),
                  pl.BlockSpec((D, te), lambda i, j: (0, j)),
                  per_token, per_token, per_token, per_token,
                  pl.BlockSpec((tt, D), lambda i, j: (i, 0)),
                  pl.BlockSpec((1, D), lambda i, j: (0, 0))],
        out_specs=pl.BlockSpec((tt, D), lambda i, j: (i, 0)),
        out_shape=jax.ShapeDtypeStruct((T, D), F32),
        scratch_shapes=[pltpu.VMEM((P_HEADS, SUBLANES, PACKED_ROWS, tt), BF16),
                        pltpu.VMEM((P_HEADS, SUBLANES, PACKED_ROWS, tt), BF16),
                        pltpu.VMEM((D, tt), F32)],
        compiler_params=_cparams(("parallel", "arbitrary")),
        name="peer_dense",
    )(hn, u, vt, *sel, res, fgain.reshape(1, D))


def _peer(x, g, wq, keys, u, v, fgain, final):
    hn, st = _peer_scores(x, g, wq, keys)
    rank2, n1, e2, coef = _peer_topk(st)
    return _peer_dense(hn, u.astype(BF16), v.T.astype(BF16), (rank2, e2, n1, coef), x, fgain, final)


def _block_diag(blocks):
    H, r, c = blocks.shape
    eye = jnp.eye(H, dtype=blocks.dtype)
    return (eye[:, None, :, None] * blocks[:, :, None, :]).reshape(H * r, H * c)


def _in_proj_weights(w_in):
    sizes = (A_HEADS * A_HEAD_DIM, A_LATENT, IDX_HEADS * IDX_DIM, IDX_DIM, IDX_HEADS,
             R_HEADS * R_DK, R_HEADS * R_DK, R_HEADS * R_DV, R_HEADS * R_DV)
    pts = np.cumsum(sizes)[:-1].tolist()
    qa, ckv, qi, ki, wi, rq, rk, rv, rg = jnp.split(w_in, pts, axis=1)
    pad = jnp.zeros((w_in.shape[0], LANES - IDX_DIM - IDX_HEADS), w_in.dtype)
    return (jnp.concatenate([rq, rk, rv, rg, qa, ckv], axis=1),
            jnp.concatenate([qi, ki, wi, pad], axis=1))


def kernel(x, mix_norm_e, w_in, kv_norm, w_uk, w_uv, w_o, mix_norm_o, conv_w1, conv_b1, conv_dw, conv_dw_b, conv_ln_g, conv_ln_b, conv_w2, conv_b2, ffn_norm, peer_wq, peer_keys, peer_u, peer_v, final_norm):
    B, L, D = x.shape
    T = B * L
    depth = ffn_norm.shape[0]
    xf = x.reshape(T, D)
    for layer in range(depth):
        j = layer // 2
        if layer % 2 == 0:
            proj, idx = _in_proj(xf, mix_norm_e[j], *_in_proj_weights(w_in[j]))
            proj3 = proj.reshape(B, L, proj.shape[1])
            idx3 = idx.reshape(B, L, idx.shape[1])
            wuk_bd = _block_diag(jnp.swapaxes(w_uk[j], 1, 2)).astype(BF16)
            wuv_bd = _block_diag(w_uv[j]).astype(BF16)
            a_out = _dsa(proj3, idx3, kv_norm[j], wuk_bd, wuv_bd)
            b_out = _retention(proj3)
            wo = w_o[j].astype(BF16)
            na = A_HEADS * A_HEAD_DIM
            xf = _mm_res([a_out.reshape(T, na), b_out.reshape(T, -1)], [wo[:na], wo[na:]], xf)
        else:
            a = _norm_glu(xf, mix_norm_o[j], conv_w1[j].astype(BF16), conv_b1[j])
            y = _conv_ln(a.reshape(B, L, D), conv_dw[j], conv_dw_b[j], conv_ln_g[j], conv_ln_b[j])
            xf = _mm_res([y.reshape(T, D)], [conv_w2[j].astype(BF16)], xf, bias=conv_b2[j])
        xf = _peer(xf, ffn_norm[layer], peer_wq[layer], peer_keys[layer], peer_u[layer], peer_v[layer],
                   final_norm, final=(layer == depth - 1))
    return xf.reshape(B, L, D)
```

```python
import functools

import jax
import jax.numpy as jnp
import numpy as np
from jax import lax
from jax.experimental import pallas as pl
from jax.experimental.pallas import tpu as pltpu

F32 = jnp.float32
BF16 = jnp.bfloat16
I32 = jnp.int32

EPS = 1e-6
A_HEADS, A_HEAD_DIM, A_LATENT = 8, 64, 128
IDX_HEADS, IDX_DIM, TOPK_MAX = 8, 32, 256
R_HEADS, R_DK, R_DV, R_CHUNK = 4, 128, 128, 128
CONV_WIDTH = 31
P_HEADS, P_NKEYS, P_QDIM, P_TOPK = 8, 128, 256, 16

LANES = 128
SUBLANES = 8
VMEM_LIMIT = 56 * 1024 * 1024
INT_MIN = -(2 ** 31)
NEG_BIG = -1e30
LOG2_E = 1.4426950408889634

NT_DIMS = (((1,), (1,)), ((), ()))


def _cparams(sem, flags=None):
    return pltpu.CompilerParams(dimension_semantics=sem, vmem_limit_bytes=VMEM_LIMIT, flags=flags)


def _aligned(x, m):
    if isinstance(x, int):
        assert x % m == 0
        return x
    return pl.multiple_of(x, m)


def _rms(x, g):
    return x * lax.rsqrt(jnp.mean(x * x, axis=-1, keepdims=True) + EPS) * g


def _split(x):
    hi = x.astype(BF16)
    return hi, (x - hi.astype(F32)).astype(BF16)


def _dot3(a_hi, a_lo, b_hi, b_lo):
    d = functools.partial(jnp.dot, preferred_element_type=F32)
    return d(a_hi, b_hi) + (d(a_hi, b_lo) + d(a_lo, b_hi))


def _in_proj_kernel(x_ref, g_ref, w_ref, whi_ref, wlo_ref, o_ref, oi_ref):
    h = _rms(x_ref[...], g_ref[...])
    hi, lo = _split(h)
    o_ref[...] = jnp.dot(hi, w_ref[...], preferred_element_type=F32)
    oi_ref[...] = _dot3(hi, lo, whi_ref[...], wlo_ref[...])


def _in_proj(x, g, w_main, w_idx, tm=512):
    T, D = x.shape
    N, NI = w_main.shape[1], w_idx.shape[1]
    whi, wlo = _split(w_idx)
    return pl.pallas_call(
        _in_proj_kernel,
        grid=(T // tm,),
        in_specs=[pl.BlockSpec((tm, D), lambda i: (i, 0)),
                  pl.BlockSpec((1, D), lambda i: (0, 0)),
                  pl.BlockSpec((D, N), lambda i: (0, 0)),
                  pl.BlockSpec((D, NI), lambda i: (0, 0)),
                  pl.BlockSpec((D, NI), lambda i: (0, 0))],
        out_specs=[pl.BlockSpec((tm, N), lambda i: (i, 0)),
                   pl.BlockSpec((tm, NI), lambda i: (i, 0))],
        out_shape=[jax.ShapeDtypeStruct((T, N), F32), jax.ShapeDtypeStruct((T, NI), F32)],
        compiler_params=_cparams(("parallel",)),
        name="in_proj",
    )(x, g.reshape(1, D), w_main.astype(BF16), whi, wlo)


def _norm_glu_kernel(x_ref, g_ref, wa_ref, wg_ref, ba_ref, bg_ref, o_ref, h_scr):
    @pl.when(pl.program_id(1) == 0)
    def _():
        h_scr[...] = _rms(x_ref[...], g_ref[...]).astype(BF16)

    h = h_scr[...]
    a = jnp.dot(h, wa_ref[...], preferred_element_type=F32) + ba_ref[...]
    gate = jnp.dot(h, wg_ref[...], preferred_element_type=F32) + bg_ref[...]
    o_ref[...] = a * jax.nn.sigmoid(gate)


def _norm_glu(x, g, w1, b1, tm=512, tn=512):
    T, D = x.shape
    nj = D // tn
    b1 = b1.reshape(1, 2 * D)
    return pl.pallas_call(
        _norm_glu_kernel,
        grid=(T // tm, nj),
        in_specs=[pl.BlockSpec((tm, D), lambda i, j: (i, 0)),
                  pl.BlockSpec((1, D), lambda i, j: (0, 0)),
                  pl.BlockSpec((D, tn), lambda i, j: (0, j)),
                  pl.BlockSpec((D, tn), lambda i, j: (0, j + nj)),
                  pl.BlockSpec((1, tn), lambda i, j: (0, j)),
                  pl.BlockSpec((1, tn), lambda i, j: (0, j + nj))],
        out_specs=pl.BlockSpec((tm, tn), lambda i, j: (i, j)),
        out_shape=jax.ShapeDtypeStruct((T, D), F32),
        scratch_shapes=[pltpu.VMEM((tm, D), BF16)],
        compiler_params=_cparams(("parallel", "arbitrary")),
        name="norm_glu",
    )(x, g.reshape(1, D), w1, w1, b1, b1)


def _mm_res_kernel(*refs, n_lhs, has_bias):
    lhs = refs[:n_lhs]
    ws = refs[n_lhs:2 * n_lhs]
    rest = refs[2 * n_lhs:]
    if has_bias:
        b_ref, res_ref, o_ref = rest
        acc = res_ref[...] + b_ref[...]
    else:
        res_ref, o_ref = rest
        acc = res_ref[...]
    for a_ref, w_ref in zip(lhs, ws):
        acc = acc + jnp.dot(a_ref[...], w_ref[...], preferred_element_type=F32)
    o_ref[...] = acc


def _mm_res(lhs, ws, res, bias=None, tm=512):
    T, N = res.shape
    n = len(lhs)
    in_specs = [pl.BlockSpec((tm, a.shape[1]), lambda i: (i, 0)) for a in lhs]
    in_specs += [pl.BlockSpec(w.shape, lambda i: (0, 0)) for w in ws]
    args = list(lhs) + list(ws)
    if bias is not None:
        in_specs.append(pl.BlockSpec((1, N), lambda i: (0, 0)))
        args.append(bias.reshape(1, N))
    in_specs.append(pl.BlockSpec((tm, N), lambda i: (i, 0)))
    args.append(res)
    return pl.pallas_call(
        functools.partial(_mm_res_kernel, n_lhs=n, has_bias=bias is not None),
        grid=(T // tm,),
        in_specs=in_specs,
        out_specs=pl.BlockSpec((tm, N), lambda i: (i, 0)),
        out_shape=jax.ShapeDtypeStruct((T, N), F32),
        compiler_params=_cparams(("parallel",)),
        name="mm_res",
    )(*args)


def _dsa_kernel(qa_ref, qi_ref, misc_ref, ckv_ref, kit_ref, kvn_ref, wuk_ref, wuv_ref, o_ref,
                c_s, key_s, qst_s, qlat_s, wb_s, lg_s, bias_s, cmax_s, p_s, m_s, l_s, acc_s,
                *, Q, KC, topk, col_bits):
    qb = pl.program_id(1)
    H = A_HEADS
    NL = KC // LANES

    @pl.when(qb == 0)
    def _():
        c_s[...] = _rms(ckv_ref[0], kvn_ref[...]).astype(BF16)

    qi = qi_ref[0]
    wi = misc_ref[0][:, IDX_DIM:IDX_DIM + IDX_HEADS] * (IDX_HEADS ** -0.5)
    for h in range(IDX_HEADS):
        hi, lo = _split(qi[:, h * IDX_DIM:(h + 1) * IDX_DIM])
        qst_s[h * Q:(h + 1) * Q, :] = jnp.concatenate([hi, lo, hi], axis=1)
        wb_s[h] = jnp.broadcast_to(wi[:, h:h + 1], (Q, LANES))
    ql = jnp.dot(qa_ref[0].astype(BF16), wuk_ref[...], preferred_element_type=F32) * (A_HEAD_DIM ** -0.5 * LOG2_E)
    for h in range(H):
        qlat_s[h * Q:(h + 1) * Q, :] = ql[:, h * A_LATENT:(h + 1) * A_LATENT].astype(BF16)

    n_chunks = ((qb + 1) * Q + KC - 1) // KC
    q_pos = qb * Q + lax.broadcasted_iota(I32, (Q, LANES), 0)
    lane = lax.broadcasted_iota(I32, (Q, LANES), 1)

    RG = 4 * SUBLANES
    lane_rg = lax.broadcasted_iota(I32, (RG, LANES), 1)
    row_rg = lax.broadcasted_iota(I32, (RG, LANES), 0)

    def score_chunk(c, carry):
        off = pl.multiple_of(c * KC, KC)
        lg_s[...] = jnp.dot(qst_s[...], kit_ref[0, :, pl.ds(off, KC)], preferred_element_type=F32)
        for r0 in range(0, Q, RG):
            a = None
            for h in range(IDX_HEADS):
                w = jnp.concatenate([wb_s[h, r0:r0 + RG, :]] * NL, axis=1)
                t = jnp.maximum(lg_s[h * Q + r0:h * Q + r0 + RG, :], 0.0) * w
                a = t if a is None else a + t
            bits = pltpu.bitcast(a, I32)
            sgn = bits >> 31
            key = (bits ^ (sgn & 0x7FFFFFFF)) - sgn
            for l in range(NL):
                causal = off + l * LANES + lane_rg <= qb * Q + r0 + row_rg
                key_s[c * NL + l, r0:r0 + RG, :] = jnp.where(causal, key[:, l * LANES:(l + 1) * LANES], INT_MIN)
        return carry

    lax.fori_loop(0, n_chunks, score_chunk, 0)

    SR = min(Q, 16 * SUBLANES)
    lane_sr = lax.broadcasted_iota(I32, (SR, LANES), 1)

    def row_count(pred):
        cnts = []
        for r0 in range(0, Q, SR):
            def body(c, cnt, r0=r0):
                for l in range(NL):
                    blk = key_s[c * NL + l, r0:r0 + SR, :]
                    cnt = cnt + jnp.where(pred(blk, r0, c * KC + l * LANES), 1, 0)
                return cnt

            cnts.append(lax.fori_loop(0, n_chunks, body, jnp.zeros((SR, LANES), I32)))
        return jnp.sum(jnp.concatenate(cnts, axis=0).astype(F32), axis=1, keepdims=True)

    def bit_body(i, carry):
        tau, n_at = carry
        cand = tau | lax.shift_left(jnp.int32(1), 31 - i)
        cand_s = cand ^ INT_MIN
        tot = row_count(lambda blk, r0, col0: blk >= cand_s[r0:r0 + SR])
        take = tot >= topk
        return jnp.where(take, cand, tau), jnp.where(take, tot, n_at)

    zero = jnp.zeros((Q, LANES), I32)
    tau, n_ge = lax.fori_loop(0, 32, bit_body, (zero, jnp.zeros((Q, 1), F32)))
    thr = jnp.maximum(tau ^ INT_MIN, INT_MIN + 1)

    excess = n_ge > topk
    cmax_s[...] = jnp.full((Q, LANES), 2 ** 31 - 1, I32)

    @pl.when(jnp.max(jnp.where(excess, 1.0, 0.0)) > 0.0)
    def _():
        need = topk - row_count(lambda blk, r0, col0: blk > thr[r0:r0 + SR])

        def col_bit(i, y):
            cand = y | lax.shift_left(jnp.int32(1), col_bits - 1 - i)
            tot = row_count(lambda blk, r0, col0:
                            (blk == thr[r0:r0 + SR]) & (col0 + lane_sr < cand[r0:r0 + SR]))
            return jnp.where(tot < need, cand, y)

        y = lax.fori_loop(0, col_bits, col_bit, zero)
        cmax_s[...] = jnp.where(excess, y, 2 ** 31 - 1)

    cmax = cmax_s[...]
    m_s[...] = jnp.full(m_s.shape, NEG_BIG, F32)
    l_s[...] = jnp.zeros(l_s.shape, F32)
    acc_s[...] = jnp.zeros(acc_s.shape, F32)

    def att_chunk(c, carry):
        off = pl.multiple_of(c * KC, KC)
        ck = c_s[pl.ds(off, KC), :]
        lg_s[...] = lax.dot_general(qlat_s[...], ck, NT_DIMS, preferred_element_type=F32)
        for l in range(NL):
            blk = key_s[c * NL + l]
            sel = (blk > thr) | ((blk == thr) & (off + l * LANES + lane <= cmax))
            bias_s[l] = jnp.where(sel, 0.0, NEG_BIG)

        for h in range(H):
            for r0 in range(0, Q, RG):
                rows = slice(h * Q + r0, h * Q + r0 + RG)
                m_prev = m_s[rows, :]
                lg = lg_s[rows, :]
                lgs = [lg[:, l * LANES:(l + 1) * LANES] + bias_s[l, r0:r0 + RG, :] for l in range(NL)]
                mx = lgs[0]
                for l in range(1, NL):
                    mx = jnp.maximum(mx, lgs[l])
                m_new = jnp.maximum(m_prev, jnp.max(mx, axis=1, keepdims=True))
                alpha = jnp.exp2(m_prev - m_new)
                ps = [jnp.exp2(x - m_new) for x in lgs]
                p_s[rows, :] = jnp.concatenate(ps, axis=1).astype(BF16)
                psum = ps[0]
                for l in range(1, NL):
                    psum = psum + ps[l]
                l_s[rows, :] = alpha * l_s[rows, :] + psum
                m_s[rows, :] = m_new
                acc_s[rows, :] = acc_s[rows, :] * alpha
        acc_s[...] += jnp.dot(p_s[...], ck, preferred_element_type=F32)
        return carry

    lax.fori_loop(0, n_chunks, att_chunk, 0)

    o = acc_s[...] / jnp.sum(l_s[...], axis=1, keepdims=True)
    o_cat = jnp.concatenate([o[h * Q:(h + 1) * Q, :] for h in range(H)], axis=1).astype(BF16)
    o_ref[0] = jnp.dot(o_cat, wuv_ref[...], preferred_element_type=F32).astype(BF16)


def _dsa(proj3, idx3, kv_norm, wuk_bd, wuv_bd, Q=256, KC=512):
    B, L, _ = proj3.shape
    H = A_HEADS
    topk = min(TOPK_MAX, L // 4)
    k_hi, k_lo = _split(jnp.swapaxes(idx3[:, :, 256:256 + IDX_DIM], 1, 2))
    kit = jnp.concatenate([k_hi, k_hi, k_lo], axis=1)
    kern = functools.partial(_dsa_kernel, Q=Q, KC=KC, topk=topk, col_bits=max(1, (L - 1).bit_length()))
    return pl.pallas_call(
        kern,
        grid=(B, L // Q),
        in_specs=[pl.BlockSpec((1, Q, 512), lambda b, q: (b, q, 4)),
                  pl.BlockSpec((1, Q, 256), lambda b, q: (b, q, 0)),
                  pl.BlockSpec((1, Q, 128), lambda b, q: (b, q, 2)),
                  pl.BlockSpec((1, L, 128), lambda b, q: (b, 0, 20)),
                  pl.BlockSpec((1, 3 * IDX_DIM, L), lambda b, q: (b, 0, 0)),
                  pl.BlockSpec((1, A_LATENT), lambda b, q: (0, 0)),
                  pl.BlockSpec(wuk_bd.shape, lambda b, q: (0, 0)),
                  pl.BlockSpec(wuv_bd.shape, lambda b, q: (0, 0))],
        out_specs=pl.BlockSpec((1, Q, H * A_HEAD_DIM), lambda b, q: (b, q, 0)),
        out_shape=jax.ShapeDtypeStruct((B, L, H * A_HEAD_DIM), BF16),
        scratch_shapes=[pltpu.VMEM((L, A_LATENT), BF16),
                        pltpu.VMEM((L // LANES, Q, LANES), I32),
                        pltpu.VMEM((IDX_HEADS * Q, 3 * IDX_DIM), BF16),
                        pltpu.VMEM((H * Q, A_LATENT), BF16),
                        pltpu.VMEM((IDX_HEADS, Q, LANES), F32),
                        pltpu.VMEM((H * Q, KC), F32),
                        pltpu.VMEM((KC // LANES, Q, LANES), F32),
                        pltpu.VMEM((Q, LANES), I32),
                        pltpu.VMEM((H * Q, KC), BF16),
                        pltpu.VMEM((H * Q, LANES), F32),
                        pltpu.VMEM((H * Q, LANES), F32),
                        pltpu.VMEM((H * Q, A_LATENT), F32)],
        compiler_params=_cparams(("arbitrary", "arbitrary")),
        name="dsa",
    )(proj3, idx3, idx3, proj3, kit, kv_norm.reshape(1, A_LATENT), wuk_bd, wuv_bd)


def _ret_kernel(q_ref, k_ref, v_ref, g_ref, cos_ref, sin_ref, dec_ref, xi_ref, zeta_ref, gch_ref,
                o_ref, st_s, *, B):
    @pl.when(pl.program_id(0) == 0)
    def _():
        st_s[...] = jnp.zeros(st_s.shape, F32)

    C = R_CHUNK
    cos = cos_ref[...]
    sin = sin_ref[...]
    even = (lax.broadcasted_iota(I32, (C, R_DK), 1) & 1) == 0

    def rot(x):
        partner = jnp.where(even, pltpu.roll(x, R_DK - 1, 1), pltpu.roll(x, 1, 1))
        return x * cos + partner * sin

    for b in range(B):
        for h in range(R_HEADS):
            sl = slice(h * R_DK, (h + 1) * R_DK)
            q = rot(q_ref[b, :, sl])
            k = rot(k_ref[b, :, sl]) * (R_DK ** -0.5)
            vb = v_ref[b, :, sl].astype(BF16)
            qb = q.astype(BF16)
            s = lax.dot_general(qb, k.astype(BF16), NT_DIMS, preferred_element_type=F32) * dec_ref[h]
            inner = jnp.dot(s.astype(BF16), vb, preferred_element_type=F32)
            st = st_s[b, h]
            cross = jnp.dot(qb, st.astype(BF16), preferred_element_type=F32) * xi_ref[h]
            kz = (k * zeta_ref[h]).astype(BF16)
            st_s[b, h] = st * gch_ref[h] + jnp.dot(kz.T, vb, preferred_element_type=F32)
            out = inner + cross
            mu = jnp.mean(out, axis=-1, keepdims=True)
            d = out - mu
            var = jnp.mean(d * d, axis=-1, keepdims=True)
            y = d * lax.rsqrt(var + EPS)
            gate = g_ref[b, :, sl]
            o_ref[b, :, sl] = (gate * jax.nn.sigmoid(gate) * y).astype(BF16)


def _retention(proj3):
    B, L, _ = proj3.shape
    C, H = R_CHUNK, R_HEADS
    N = L // C
    d2 = R_DK // 2
    inv_freq = 1.0 / (10000.0 ** jnp.linspace(0.0, 1.0, d2, dtype=F32))
    ang = jnp.arange(L, dtype=F32)[:, None] * inv_freq[None, :]
    cos = jnp.repeat(jnp.cos(ang), 2, axis=1)
    sin = jnp.stack([-jnp.sin(ang), jnp.sin(ang)], axis=-1).reshape(L, R_DK)
    log_g = jnp.log1p(-jnp.exp2(-5.0 - jnp.arange(H, dtype=F32)))
    pos = jnp.arange(C, dtype=F32)
    diff = pos[:, None] - pos[None, :]
    decay = jnp.where(diff >= 0, jnp.exp(jnp.maximum(diff, 0.0)[None] * log_g[:, None, None]), 0.0)
    xi = jnp.broadcast_to(jnp.exp((pos + 1.0)[None, :] * log_g[:, None])[:, :, None], (H, C, R_DV))
    zeta = jnp.broadcast_to(jnp.exp((C - 1.0 - pos)[None, :] * log_g[:, None])[:, :, None], (H, C, R_DK))
    gch = jnp.broadcast_to(jnp.exp(C * log_g)[:, None, None], (H, 1, R_DV))
    W = H * R_DK
    colspec = lambda j: pl.BlockSpec((B, C, W), lambda n: (0, n, j))
    cst = lambda shape: pl.BlockSpec(shape, lambda n: (0,) * len(shape))
    return pl.pallas_call(
        functools.partial(_ret_kernel, B=B),
        grid=(N,),
        in_specs=[colspec(0), colspec(1), colspec(2), colspec(3),
                  pl.BlockSpec((C, R_DK), lambda n: (n, 0)),
                  pl.BlockSpec((C, R_DK), lambda n: (n, 0)),
                  cst((H, C, C)), cst((H, C, R_DV)), cst((H, C, R_DK)), cst((H, 1, R_DV))],
        out_specs=pl.BlockSpec((B, C, W), lambda n: (0, n, 0)),
        out_shape=jax.ShapeDtypeStruct((B, L, W), BF16),
        scratch_shapes=[pltpu.VMEM((B, H, R_DK, R_DV), F32)],
        compiler_params=_cparams(("arbitrary",)),
        name="retention",
    )(proj3, proj3, proj3, proj3, cos, sin, decay, xi, zeta, gch)


HALO = 32


CONV_ROWS = 4 * SUBLANES


def _conv_kernel(cur_ref, prev_ref, w_ref, b_ref, lg_ref, lb_ref, o_ref, ext_s, sh_s, wb_s, y_s, *, tl):
    i = pl.program_id(1)
    D = ext_s.shape[1]

    @pl.when((pl.program_id(0) == 0) & (i == 0))
    def _():
        for j in range(CONV_WIDTH):
            wb_s[j] = jnp.broadcast_to(w_ref[j:j + 1, :], (SUBLANES, D))

    ext_s[HALO:, :] = cur_ref[0]

    @pl.when(i == 0)
    def _():
        ext_s[:HALO, :] = jnp.zeros((HALO, D), F32)

    @pl.when(i > 0)
    def _():
        ext_s[:HALO, :] = prev_ref[0]

    base = HALO - (CONV_WIDTH - 1)
    offs = range(base, base + CONV_WIDTH)
    for r in range(1, SUBLANES):
        span = max(o for o in offs if o % SUBLANES == r) - r + tl
        sh_s[r, :span, :] = ext_s[r:r + span, :]

    def rows(t, carry):
        row0 = pl.multiple_of(t * CONV_ROWS, CONV_ROWS)
        n_sub = CONV_ROWS // SUBLANES
        y = [None] * n_sub
        for o in offs:
            r = o % SUBLANES
            start = pl.multiple_of(row0 + (o - r), SUBLANES)
            x = ext_s[pl.ds(start, CONV_ROWS), :] if r == 0 else sh_s[r, pl.ds(start, CONV_ROWS), :]
            w = wb_s[o - base]
            for k in range(n_sub):
                term = x[k * SUBLANES:(k + 1) * SUBLANES] * w
                y[k] = term if y[k] is None else y[k] + term
        y_s[pl.ds(row0, CONV_ROWS), :] = jnp.concatenate(y, axis=0)
        return carry

    lax.fori_loop(0, tl // CONV_ROWS, rows, 0)

    y = y_s[...] + b_ref[...]
    mu = jnp.mean(y, axis=-1, keepdims=True)
    d = y - mu
    var = jnp.mean(d * d, axis=-1, keepdims=True)
    z = d * lax.rsqrt(var + EPS) * lg_ref[...] + lb_ref[...]
    o_ref[0] = (z * jax.nn.sigmoid(z)).astype(BF16)


def _conv_ln(a3, w_dw, b_dw, ln_g, ln_b, tl=256):
    B, L, D = a3.shape
    r = tl // HALO
    row = lambda v: v.reshape(1, D)
    return pl.pallas_call(
        functools.partial(_conv_kernel, tl=tl),
        grid=(B, L // tl),
        in_specs=[pl.BlockSpec((1, tl, D), lambda b, i: (b, i, 0)),
                  pl.BlockSpec((1, HALO, D), lambda b, i: (b, jnp.maximum(i * r - 1, 0), 0)),
                  pl.BlockSpec((CONV_WIDTH, D), lambda b, i: (0, 0)),
                  pl.BlockSpec((1, D), lambda b, i: (0, 0)),
                  pl.BlockSpec((1, D), lambda b, i: (0, 0)),
                  pl.BlockSpec((1, D), lambda b, i: (0, 0))],
        out_specs=pl.BlockSpec((1, tl, D), lambda b, i: (b, i, 0)),
        out_shape=jax.ShapeDtypeStruct((B, L, D), BF16),
        scratch_shapes=[pltpu.VMEM((tl + HALO, D), F32),
                        pltpu.VMEM((SUBLANES, tl + HALO, D), F32),
                        pltpu.VMEM((CONV_WIDTH, SUBLANES, D), F32),
                        pltpu.VMEM((tl, D), F32)],
        compiler_params=_cparams(("arbitrary", "arbitrary")),
        name="conv_ln",
    )(a3, a3, w_dw, row(b_dw), row(ln_g), row(ln_b))


def _take_top(s, n):
    rows = []
    for _ in range(n):
        mx = jnp.max(s, axis=0, keepdims=True)
        rows.append(mx)
        s = jnp.where(s == mx, -jnp.inf, s)
    return rows, s


def _take_top_ranked(s, n):
    rows = []
    rank = jnp.full(s.shape, float(n), F32)
    for r in range(n):
        mx = jnp.max(s, axis=0, keepdims=True)
        rows.append(mx)
        hit = s == mx
        rank = jnp.where(hit, float(r), rank)
        s = jnp.where(hit, -jnp.inf, s)
    return rows, rank


def _select_pairs(s1, s2):
    n = P_TOPK
    tt = s1.shape[-1]
    r1, _ = _take_top(s1, n)
    r2, rank2 = _take_top_ranked(s2, n)
    v2 = jnp.concatenate(r2, axis=0)
    blocks = [r1[0] + v2]
    rows8 = lax.broadcasted_iota(I32, (SUBLANES, tt), 0)
    for i in range(1, SUBLANES):
        blocks.append(jnp.where(rows8 < n // (i + 1), r1[i] + v2[:SUBLANES], -jnp.inf))
    blocks.append(jnp.concatenate(r1[SUBLANES:], axis=0) + r2[0])
    top, _ = _take_top(jnp.concatenate(blocks, axis=0), n)
    kth = top[-1]
    z = None
    for r in top:
        e = jnp.exp(r - top[0])
        z = e if z is None else z + e
    hits = [jnp.where(blk >= kth, 1.0, 0.0) for blk in blocks]
    per_rank = [jnp.sum(hb, axis=0, keepdims=True) for hb in hits[:SUBLANES]]
    per_rank += [hits[SUBLANES][k:k + 1] for k in range(n - SUBLANES)]
    n1 = jnp.zeros(s1.shape, F32)
    for i in range(n):
        n1 = jnp.where(s1 == r1[i], per_rank[i], n1)
    e2 = jnp.exp(s2 - r2[0]).astype(BF16)
    coef = jnp.exp(s1 - r1[0]) * (0.5 / z)
    return rank2.astype(BF16), n1, e2, coef


def _peer_score_kernel(x_ref, g_ref, whi_ref, wlo_ref, k3_ref, h_ref, st_ref, hi_s, lo_s):
    @pl.when(pl.program_id(1) == 0)
    def _():
        hi, lo = _split(_rms(x_ref[...], g_ref[...]))
        hi_s[...] = hi
        lo_s[...] = lo
        h_ref[...] = hi

    q = _dot3(hi_s[...], lo_s[...], whi_ref[...], wlo_ref[...])
    q_hi, q_lo = _split(q)
    half = P_QDIM // 2
    for s in range(q.shape[1] // half):
        cols = slice(s * half, (s + 1) * half)
        q3 = jnp.concatenate([q_hi[:, cols], q_lo[:, cols], q_hi[:, cols]], axis=1)
        st_ref[s // 2, s % 2] = lax.dot_general(k3_ref[s % 2], q3, NT_DIMS, preferred_element_type=F32)


def _peer_scores(x, g, wq, keys, tm=512, hps=4):
    T, D = x.shape
    whi, wlo = _split(wq)
    k_hi, k_lo = _split(keys)
    k3 = jnp.concatenate([k_hi, k_hi, k_lo], axis=-1)
    return pl.pallas_call(
        _peer_score_kernel,
        grid=(T // tm, P_HEADS // hps),
        in_specs=[pl.BlockSpec((tm, D), lambda i, j: (i, 0)),
                  pl.BlockSpec((1, D), lambda i, j: (0, 0)),
                  pl.BlockSpec((D, hps * P_QDIM), lambda i, j: (0, j)),
                  pl.BlockSpec((D, hps * P_QDIM), lambda i, j: (0, j)),
                  pl.BlockSpec(k3.shape, lambda i, j: (0, 0, 0))],
        out_specs=[pl.BlockSpec((tm, D), lambda i, j: (i, 0)),
                   pl.BlockSpec((hps, 2, P_NKEYS, tm), lambda i, j: (j, 0, 0, i))],
        out_shape=[jax.ShapeDtypeStruct((T, D), BF16),
                   jax.ShapeDtypeStruct((P_HEADS, 2, P_NKEYS, T), F32)],
        scratch_shapes=[pltpu.VMEM((tm, D), BF16), pltpu.VMEM((tm, D), BF16)],
        compiler_params=_cparams(("parallel", "arbitrary")),
        name="peer_scores",
    )(x, g.reshape(1, D), whi, wlo, k3)


def _peer_topk_kernel(st_ref, rank2_ref, n1_ref, e2_ref, coef_ref):
    def head(h, carry):
        rank2_ref[h], n1_ref[h], e2_ref[h], coef_ref[h] = _select_pairs(st_ref[h, 0], st_ref[h, 1])
        return carry

    lax.fori_loop(0, P_HEADS, head, 0)


def _peer_topk(st, tt=256):
    T = st.shape[-1]
    spec = pl.BlockSpec((P_HEADS, P_NKEYS, tt), lambda i: (0, 0, i))
    shape = lambda dt: jax.ShapeDtypeStruct((P_HEADS, P_NKEYS, T), dt)
    return pl.pallas_call(
        _peer_topk_kernel,
        grid=(T // tt,),
        in_specs=[pl.BlockSpec((P_HEADS, 2, P_NKEYS, tt), lambda i: (0, 0, 0, i))],
        out_specs=[spec, spec, spec, spec],
        out_shape=[shape(BF16), shape(F32), shape(BF16), shape(F32)],
        compiler_params=_cparams(("parallel",)),
        name="peer_topk",
    )(st)


TILE_ROWS = 32
PACKED_ROWS = 16
UP_ROWS = 2
DENSE_ROWS = 16


def _peer_dense_kernel(h_ref, u_ref, vt_ref, rank_ref, e2_ref, n1_ref, coef_ref, res_ref, fg_ref, o_ref,
                       bn_s, bcoef_s, acc_s, *, te, tt, final):
    j = pl.program_id(1)
    NK = P_NKEYS

    @pl.when(j == 0)
    def _():
        acc_s[...] = jnp.zeros(acc_s.shape, F32)

    n_rows = te // NK
    reps = TILE_ROWS // PACKED_ROWS
    for h in range(P_HEADS):
        for grp in range(n_rows // SUBLANES):
            a0 = pl.multiple_of(j * n_rows + grp * SUBLANES, SUBLANES)
            n8 = n1_ref[h, pl.ds(a0, SUBLANES), :]
            coef8 = coef_ref[h, pl.ds(a0, SUBLANES), :]
            for r in range(SUBLANES):
                al = grp * SUBLANES + r
                bn_s[h, al] = jnp.broadcast_to(n8[r:r + 1], (PACKED_ROWS, tt)).astype(BF16)
                bcoef_s[h, al] = jnp.broadcast_to(coef8[r:r + 1], (PACKED_ROWS, tt)).astype(BF16)

    acts = []
    for al in range(n_rows):
        if al % UP_ROWS == 0:
            hb = lax.dot_general(u_ref[al * NK:(al + UP_ROWS) * NK, :], h_ref[...], NT_DIMS,
                                 preferred_element_type=F32)
        for b0 in range(0, NK, TILE_ROWS):
            sub = slice(b0, b0 + TILE_ROWS)
            g = None
            for h in range(P_HEADS):
                pairs = jnp.concatenate([bn_s[h, al]] * reps, axis=0)
                coef = jnp.concatenate([bcoef_s[h, al]] * reps, axis=0)
                e2 = e2_ref[h, sub, :]
                t = jnp.where(rank_ref[h, sub, :] < pairs, e2, jnp.zeros_like(e2)) * coef
                g = t if g is None else g + t
            r0 = (al % UP_ROWS) * NK + b0
            x = hb[r0:r0 + TILE_ROWS, :]
            acts.append((x * (1.0 + lax.erf(x * (2.0 ** -0.5)))).astype(BF16) * g)
    act = jnp.concatenate(acts, axis=0)
    acc_s[...] += lax.dot_general(vt_ref[...], act, (((0,), (0,)), ((), ())),
                                  preferred_element_type=F32)

    @pl.when(j == pl.num_programs(1) - 1)
    def _():
        y = res_ref[...] + acc_s[...].T
        if final:
            y = _rms(y, fg_ref[...])
        o_ref[...] = y


def _peer_dense(hn, u, vt, sel, res, fgain, final, tt=512):
    T, D = res.shape
    E = u.shape[0]
    te = DENSE_ROWS * P_NKEYS
    n_tiles = E // te
    kern = functools.partial(_peer_dense_kernel, te=te, tt=tt, final=final)
    per_token = pl.BlockSpec((P_HEADS, P_NKEYS, tt), lambda i, j: (0, 0, i))
    return pl.pallas_call(
        kern,
        grid=(T // tt, n_tiles),
        in_specs=[pl.BlockSpec((tt, D), lambda i, j: (i, 0)),
                  pl.BlockSpec((te, D), lambda i, j: (j, 0)),
                  pl.BlockSpec((te, D), lambda i, j: (j, 0)),
                  per_token, per_token, per_token, per_token,
                  pl.BlockSpec((tt, D), lambda i, j: (i, 0)),
                  pl.BlockSpec((1, D), lambda i, j: (0, 0))],
        out_specs=pl.BlockSpec((tt, D), lambda i, j: (i, 0)),
        out_shape=jax.ShapeDtypeStruct((T, D), F32),
        scratch_shapes=[pltpu.VMEM((P_HEADS, DENSE_ROWS, PACKED_ROWS, tt), BF16),
                        pltpu.VMEM((P_HEADS, DENSE_ROWS, PACKED_ROWS, tt), BF16),
                        pltpu.VMEM((D, tt), F32)],
        compiler_params=_cparams(("parallel", "arbitrary")),
        name="peer_dense",
    )(hn, u, vt, *sel, res, fgain.reshape(1, D))


def _peer(x, g, wq, keys, u, v, fgain, final):
    hn, st = _peer_scores(x, g, wq, keys)
    rank2, n1, e2, coef = _peer_topk(st)
    return _peer_dense(hn, u.astype(BF16), v.astype(BF16), (rank2, e2, n1, coef), x, fgain, final)


def _block_diag(blocks):
    H, r, c = blocks.shape
    eye = jnp.eye(H, dtype=blocks.dtype)
    return (eye[:, None, :, None] * blocks[:, :, None, :]).reshape(H * r, H * c)


def _in_proj_weights(w_in):
    sizes = (A_HEADS * A_HEAD_DIM, A_LATENT, IDX_HEADS * IDX_DIM, IDX_DIM, IDX_HEADS,
             R_HEADS * R_DK, R_HEADS * R_DK, R_HEADS * R_DV, R_HEADS * R_DV)
    pts = np.cumsum(sizes)[:-1].tolist()
    qa, ckv, qi, ki, wi, rq, rk, rv, rg = jnp.split(w_in, pts, axis=1)
    pad = jnp.zeros((w_in.shape[0], LANES - IDX_DIM - IDX_HEADS), w_in.dtype)
    return (jnp.concatenate([rq, rk, rv, rg, qa, ckv], axis=1),
            jnp.concatenate([qi, ki, wi, pad], axis=1))


def kernel(x, mix_norm_e, w_in, kv_norm, w_uk, w_uv, w_o, mix_norm_o, conv_w1, conv_b1, conv_dw, conv_dw_b, conv_ln_g, conv_ln_b, conv_w2, conv_b2, ffn_norm, peer_wq, peer_keys, peer_u, peer_v, final_norm):
    B, L, D = x.shape
    T = B * L
    depth = ffn_norm.shape[0]
    xf = x.reshape(T, D)
    for layer in range(depth):
        j = layer // 2
        if layer % 2 == 0:
            proj, idx = _in_proj(xf, mix_norm_e[j], *_in_proj_weights(w_in[j]))
            proj3 = proj.reshape(B, L, proj.shape[1])
            idx3 = idx.reshape(B, L, idx.shape[1])
            wuk_bd = _block_diag(jnp.swapaxes(w_uk[j], 1, 2)).astype(BF16)
            wuv_bd = _block_diag(w_uv[j]).astype(BF16)
            a_out = _dsa(proj3, idx3, kv_norm[j], wuk_bd, wuv_bd)
            b_out = _retention(proj3)
            wo = w_o[j].astype(BF16)
            na = A_HEADS * A_HEAD_DIM
            xf = _mm_res([a_out.reshape(T, na), b_out.reshape(T, -1)], [wo[:na], wo[na:]], xf)
        else:
            a = _norm_glu(xf, mix_norm_o[j], conv_w1[j].astype(BF16), conv_b1[j])
            y = _conv_ln(a.reshape(B, L, D), conv_dw[j], conv_dw_b[j], conv_ln_g[j], conv_ln_b[j])
            xf = _mm_res([y.reshape(T, D)], [conv_w2[j].astype(BF16)], xf, bias=conv_b2[j])
        xf = _peer(xf, ffn_norm[layer], peer_wq[layer], peer_keys[layer], peer_u[layer], peer_v[layer],
                   final_norm, final=(layer == depth - 1))
    return xf.reshape(B, L, D)
```

```python
import functools

import jax
import jax.numpy as jnp
import numpy as np
from jax import lax
from jax.experimental import pallas as pl
from jax.experimental.pallas import tpu as pltpu

F32 = jnp.float32
BF16 = jnp.bfloat16
I32 = jnp.int32

EPS = 1e-6
A_HEADS, A_HEAD_DIM, A_LATENT = 8, 64, 128
IDX_HEADS, IDX_DIM, TOPK_MAX = 8, 32, 256
R_HEADS, R_DK, R_DV, R_CHUNK = 4, 128, 128, 128
CONV_WIDTH = 31
P_HEADS, P_NKEYS, P_QDIM, P_TOPK = 8, 128, 256, 16

LANES = 128
SUBLANES = 8
VMEM_LIMIT = 56 * 1024 * 1024
INT_MIN = -(2 ** 31)
NEG_BIG = -1e30
LOG2_E = 1.4426950408889634

NT_DIMS = (((1,), (1,)), ((), ()))


def _cparams(sem, flags=None):
    return pltpu.CompilerParams(dimension_semantics=sem, vmem_limit_bytes=VMEM_LIMIT, flags=flags)


def _aligned(x, m):
    if isinstance(x, int):
        assert x % m == 0
        return x
    return pl.multiple_of(x, m)


def _rms(x, g):
    return x * lax.rsqrt(jnp.mean(x * x, axis=-1, keepdims=True) + EPS) * g


def _split(x):
    hi = x.astype(BF16)
    return hi, (x - hi.astype(F32)).astype(BF16)


def _dot3(a_hi, a_lo, b_hi, b_lo):
    d = functools.partial(jnp.dot, preferred_element_type=F32)
    return d(a_hi, b_hi) + (d(a_hi, b_lo) + d(a_lo, b_hi))


def _in_proj_kernel(x_ref, g_ref, w_ref, whi_ref, wlo_ref, o_ref, oi_ref):
    h = _rms(x_ref[...], g_ref[...])
    hi, lo = _split(h)
    o_ref[...] = jnp.dot(hi, w_ref[...], preferred_element_type=F32)
    oi_ref[...] = _dot3(hi, lo, whi_ref[...], wlo_ref[...])


def _in_proj(x, g, w_main, w_idx, tm=512):
    T, D = x.shape
    N, NI = w_main.shape[1], w_idx.shape[1]
    whi, wlo = _split(w_idx)
    return pl.pallas_call(
        _in_proj_kernel,
        grid=(T // tm,),
        in_specs=[pl.BlockSpec((tm, D), lambda i: (i, 0)),
                  pl.BlockSpec((1, D), lambda i: (0, 0)),
                  pl.BlockSpec((D, N), lambda i: (0, 0)),
                  pl.BlockSpec((D, NI), lambda i: (0, 0)),
                  pl.BlockSpec((D, NI), lambda i: (0, 0))],
        out_specs=[pl.BlockSpec((tm, N), lambda i: (i, 0)),
                   pl.BlockSpec((tm, NI), lambda i: (i, 0))],
        out_shape=[jax.ShapeDtypeStruct((T, N), F32), jax.ShapeDtypeStruct((T, NI), F32)],
        compiler_params=_cparams(("parallel",)),
        name="in_proj",
    )(x, g.reshape(1, D), w_main.astype(BF16), whi, wlo)


def _norm_glu_kernel(x_ref, g_ref, wa_ref, wg_ref, ba_ref, bg_ref, o_ref, h_scr):
    @pl.when(pl.program_id(1) == 0)
    def _():
        h_scr[...] = _rms(x_ref[...], g_ref[...]).astype(BF16)

    h = h_scr[...]
    a = jnp.dot(h, wa_ref[...], preferred_element_type=F32) + ba_ref[...]
    gate = jnp.dot(h, wg_ref[...], preferred_element_type=F32) + bg_ref[...]
    o_ref[...] = a * jax.nn.sigmoid(gate)


def _norm_glu(x, g, w1, b1, tm=512, tn=1024):
    T, D = x.shape
    nj = D // tn
    b1 = b1.reshape(1, 2 * D)
    return pl.pallas_call(
        _norm_glu_kernel,
        grid=(T // tm, nj),
        in_specs=[pl.BlockSpec((tm, D), lambda i, j: (i, 0)),
                  pl.BlockSpec((1, D), lambda i, j: (0, 0)),
                  pl.BlockSpec((D, tn), lambda i, j: (0, j)),
                  pl.BlockSpec((D, tn), lambda i, j: (0, j + nj)),
                  pl.BlockSpec((1, tn), lambda i, j: (0, j)),
                  pl.BlockSpec((1, tn), lambda i, j: (0, j + nj))],
        out_specs=pl.BlockSpec((tm, tn), lambda i, j: (i, j)),
        out_shape=jax.ShapeDtypeStruct((T, D), F32),
        scratch_shapes=[pltpu.VMEM((tm, D), BF16)],
        compiler_params=_cparams(("parallel", "arbitrary")),
        name="norm_glu",
    )(x, g.reshape(1, D), w1, w1, b1, b1)


def _mm_res_kernel(*refs, n_lhs, has_bias):
    lhs = refs[:n_lhs]
    ws = refs[n_lhs:2 * n_lhs]
    rest = refs[2 * n_lhs:]
    if has_bias:
        b_ref, res_ref, o_ref = rest
        acc = res_ref[...] + b_ref[...]
    else:
        res_ref, o_ref = rest
        acc = res_ref[...]
    for a_ref, w_ref in zip(lhs, ws):
        acc = acc + jnp.dot(a_ref[...], w_ref[...], preferred_element_type=F32)
    o_ref[...] = acc


def _mm_res(lhs, ws, res, bias=None, tm=512):
    T, N = res.shape
    n = len(lhs)
    in_specs = [pl.BlockSpec((tm, a.shape[1]), lambda i: (i, 0)) for a in lhs]
    in_specs += [pl.BlockSpec(w.shape, lambda i: (0, 0)) for w in ws]
    args = list(lhs) + list(ws)
    if bias is not None:
        in_specs.append(pl.BlockSpec((1, N), lambda i: (0, 0)))
        args.append(bias.reshape(1, N))
    in_specs.append(pl.BlockSpec((tm, N), lambda i: (i, 0)))
    args.append(res)
    return pl.pallas_call(
        functools.partial(_mm_res_kernel, n_lhs=n, has_bias=bias is not None),
        grid=(T // tm,),
        in_specs=in_specs,
        out_specs=pl.BlockSpec((tm, N), lambda i: (i, 0)),
        out_shape=jax.ShapeDtypeStruct((T, N), F32),
        compiler_params=_cparams(("parallel",)),
        name="mm_res",
    )(*args)


def _dsa_kernel(qa_ref, qi_ref, misc_ref, ckv_ref, kit_ref, kvn_ref, wuk_ref, wuv_ref, o_ref,
                c_s, key_s, qst_s, qlat_s, wb_s, lg_s, bias_s, cmax_s, p_s, m_s, l_s, acc_s,
                *, Q, KC, topk, col_bits):
    qb = pl.program_id(1)
    H = A_HEADS
    NL = KC // LANES

    @pl.when(qb == 0)
    def _():
        c_s[...] = _rms(ckv_ref[0], kvn_ref[...]).astype(BF16)

    qi = qi_ref[0]
    wi = misc_ref[0][:, IDX_DIM:IDX_DIM + IDX_HEADS] * (IDX_HEADS ** -0.5)
    for h in range(IDX_HEADS):
        hi, lo = _split(qi[:, h * IDX_DIM:(h + 1) * IDX_DIM])
        qst_s[h * Q:(h + 1) * Q, :] = jnp.concatenate([hi, lo, hi], axis=1)
        wb_s[h] = jnp.broadcast_to(wi[:, h:h + 1], (Q, LANES))
    ql = jnp.dot(qa_ref[0].astype(BF16), wuk_ref[...], preferred_element_type=F32) * (A_HEAD_DIM ** -0.5 * LOG2_E)
    for h in range(H):
        qlat_s[h * Q:(h + 1) * Q, :] = ql[:, h * A_LATENT:(h + 1) * A_LATENT].astype(BF16)

    n_chunks = ((qb + 1) * Q + KC - 1) // KC
    q_pos = qb * Q + lax.broadcasted_iota(I32, (Q, LANES), 0)
    lane = lax.broadcasted_iota(I32, (Q, LANES), 1)

    RG = 4 * SUBLANES
    lane_rg = lax.broadcasted_iota(I32, (RG, LANES), 1)
    row_rg = lax.broadcasted_iota(I32, (RG, LANES), 0)

    def score_chunk(c, carry):
        off = pl.multiple_of(c * KC, KC)
        lg_s[...] = jnp.dot(qst_s[...], kit_ref[0, :, pl.ds(off, KC)], preferred_element_type=F32)
        for r0 in range(0, Q, RG):
            a = None
            for h in range(IDX_HEADS):
                w = jnp.concatenate([wb_s[h, r0:r0 + RG, :]] * NL, axis=1)
                t = jnp.maximum(lg_s[h * Q + r0:h * Q + r0 + RG, :], 0.0) * w
                a = t if a is None else a + t
            bits = pltpu.bitcast(a, I32)
            sgn = bits >> 31
            key = (bits ^ (sgn & 0x7FFFFFFF)) - sgn
            for l in range(NL):
                causal = off + l * LANES + lane_rg <= qb * Q + r0 + row_rg
                key_s[c * NL + l, r0:r0 + RG, :] = jnp.where(causal, key[:, l * LANES:(l + 1) * LANES], INT_MIN)
        return carry

    lax.fori_loop(0, n_chunks, score_chunk, 0)

    SR = min(Q, 16 * SUBLANES)
    lane_sr = lax.broadcasted_iota(I32, (SR, LANES), 1)

    def row_count(pred):
        cnts = []
        for r0 in range(0, Q, SR):
            def body(c, cnt, r0=r0):
                for l in range(NL):
                    blk = key_s[c * NL + l, r0:r0 + SR, :]
                    cnt = cnt + jnp.where(pred(blk, r0, c * KC + l * LANES), 1, 0)
                return cnt

            cnts.append(lax.fori_loop(0, n_chunks, body, jnp.zeros((SR, LANES), I32)))
        return jnp.sum(jnp.concatenate(cnts, axis=0).astype(F32), axis=1, keepdims=True)

    def bit_body(i, carry):
        tau, n_at = carry
        cand = tau | lax.shift_left(jnp.int32(1), 31 - i)
        cand_s = cand ^ INT_MIN
        tot = row_count(lambda blk, r0, col0: blk >= cand_s[r0:r0 + SR])
        take = tot >= topk
        return jnp.where(take, cand, tau), jnp.where(take, tot, n_at)

    zero = jnp.zeros((Q, LANES), I32)
    tau, n_ge = lax.fori_loop(0, 32, bit_body, (zero, jnp.zeros((Q, 1), F32)))
    thr = jnp.maximum(tau ^ INT_MIN, INT_MIN + 1)

    excess = n_ge > topk
    cmax_s[...] = jnp.full((Q, LANES), 2 ** 31 - 1, I32)

    @pl.when(jnp.max(jnp.where(excess, 1.0, 0.0)) > 0.0)
    def _():
        need = topk - row_count(lambda blk, r0, col0: blk > thr[r0:r0 + SR])

        def col_bit(i, y):
            cand = y | lax.shift_left(jnp.int32(1), col_bits - 1 - i)
            tot = row_count(lambda blk, r0, col0:
                            (blk == thr[r0:r0 + SR]) & (col0 + lane_sr < cand[r0:r0 + SR]))
            return jnp.where(tot < need, cand, y)

        y = lax.fori_loop(0, col_bits, col_bit, zero)
        cmax_s[...] = jnp.where(excess, y, 2 ** 31 - 1)

    cmax = cmax_s[...]
    m_s[...] = jnp.full(m_s.shape, NEG_BIG, F32)
    l_s[...] = jnp.zeros(l_s.shape, F32)
    acc_s[...] = jnp.zeros(acc_s.shape, F32)

    def att_chunk(c, carry):
        off = pl.multiple_of(c * KC, KC)
        ck = c_s[pl.ds(off, KC), :]
        lg_s[...] = lax.dot_general(qlat_s[...], ck, NT_DIMS, preferred_element_type=F32)
        for l in range(NL):
            blk = key_s[c * NL + l]
            sel = (blk > thr) | ((blk == thr) & (off + l * LANES + lane <= cmax))
            bias_s[l] = jnp.where(sel, 0.0, NEG_BIG)

        for h in range(H):
            for r0 in range(0, Q, RG):
                rows = slice(h * Q + r0, h * Q + r0 + RG)
                m_prev = m_s[rows, :]
                lg = lg_s[rows, :]
                lgs = [lg[:, l * LANES:(l + 1) * LANES] + bias_s[l, r0:r0 + RG, :] for l in range(NL)]
                mx = lgs[0]
                for l in range(1, NL):
                    mx = jnp.maximum(mx, lgs[l])
                m_new = jnp.maximum(m_prev, jnp.max(mx, axis=1, keepdims=True))
                alpha = jnp.exp2(m_prev - m_new)
                ps = [jnp.exp2(x - m_new) for x in lgs]
                p_s[rows, :] = jnp.concatenate(ps, axis=1).astype(BF16)
                psum = ps[0]
                for l in range(1, NL):
                    psum = psum + ps[l]
                l_s[rows, :] = alpha * l_s[rows, :] + psum
                m_s[rows, :] = m_new
                acc_s[rows, :] = acc_s[rows, :] * alpha
        acc_s[...] += jnp.dot(p_s[...], ck, preferred_element_type=F32)
        return carry

    lax.fori_loop(0, n_chunks, att_chunk, 0)

    o = acc_s[...] / jnp.sum(l_s[...], axis=1, keepdims=True)
    o_cat = jnp.concatenate([o[h * Q:(h + 1) * Q, :] for h in range(H)], axis=1).astype(BF16)
    o_ref[0] = jnp.dot(o_cat, wuv_ref[...], preferred_element_type=F32).astype(BF16)


def _dsa(proj3, idx3, kv_norm, wuk_bd, wuv_bd, Q=256, KC=512):
    B, L, _ = proj3.shape
    H = A_HEADS
    topk = min(TOPK_MAX, L // 4)
    qa_w, qi_w = A_HEADS * A_HEAD_DIM, IDX_HEADS * IDX_DIM
    qa_col = 4 * R_HEADS * R_DK
    k_hi, k_lo = _split(jnp.swapaxes(idx3[:, :, qi_w:qi_w + IDX_DIM], 1, 2))
    kit = jnp.concatenate([k_hi, k_hi, k_lo], axis=1)
    kern = functools.partial(_dsa_kernel, Q=Q, KC=KC, topk=topk, col_bits=max(1, (L - 1).bit_length()))
    return pl.pallas_call(
        kern,
        grid=(B, L // Q),
        in_specs=[pl.BlockSpec((1, Q, qa_w), lambda b, q: (b, q, qa_col // qa_w)),
                  pl.BlockSpec((1, Q, qi_w), lambda b, q: (b, q, 0)),
                  pl.BlockSpec((1, Q, LANES), lambda b, q: (b, q, qi_w // LANES)),
                  pl.BlockSpec((1, L, A_LATENT), lambda b, q: (b, 0, (qa_col + qa_w) // A_LATENT)),
                  pl.BlockSpec((1, 3 * IDX_DIM, L), lambda b, q: (b, 0, 0)),
                  pl.BlockSpec((1, A_LATENT), lambda b, q: (0, 0)),
                  pl.BlockSpec(wuk_bd.shape, lambda b, q: (0, 0)),
                  pl.BlockSpec(wuv_bd.shape, lambda b, q: (0, 0))],
        out_specs=pl.BlockSpec((1, Q, H * A_HEAD_DIM), lambda b, q: (b, q, 0)),
        out_shape=jax.ShapeDtypeStruct((B, L, H * A_HEAD_DIM), BF16),
        scratch_shapes=[pltpu.VMEM((L, A_LATENT), BF16),
                        pltpu.VMEM((L // LANES, Q, LANES), I32),
                        pltpu.VMEM((IDX_HEADS * Q, 3 * IDX_DIM), BF16),
                        pltpu.VMEM((H * Q, A_LATENT), BF16),
                        pltpu.VMEM((IDX_HEADS, Q, LANES), F32),
                        pltpu.VMEM((H * Q, KC), F32),
                        pltpu.VMEM((KC // LANES, Q, LANES), F32),
                        pltpu.VMEM((Q, LANES), I32),
                        pltpu.VMEM((H * Q, KC), BF16),
                        pltpu.VMEM((H * Q, LANES), F32),
                        pltpu.VMEM((H * Q, LANES), F32),
                        pltpu.VMEM((H * Q, A_LATENT), F32)],
        compiler_params=_cparams(("arbitrary", "arbitrary")),
        name="dsa",
    )(proj3, idx3, idx3, proj3, kit, kv_norm.reshape(1, A_LATENT), wuk_bd, wuv_bd)


def _ret_kernel(q_ref, k_ref, v_ref, g_ref, cos_ref, sin_ref, dec_ref, xi_ref, zeta_ref, gch_ref,
                o_ref, st_s, *, B):
    @pl.when(pl.program_id(0) == 0)
    def _():
        st_s[...] = jnp.zeros(st_s.shape, F32)

    C = R_CHUNK
    cos = cos_ref[...]
    sin = sin_ref[...]
    even = (lax.broadcasted_iota(I32, (C, R_DK), 1) & 1) == 0

    def rot(x):
        partner = jnp.where(even, pltpu.roll(x, R_DK - 1, 1), pltpu.roll(x, 1, 1))
        return x * cos + partner * sin

    for b in range(B):
        for h in range(R_HEADS):
            sl = slice(h * R_DK, (h + 1) * R_DK)
            q = rot(q_ref[b, :, sl])
            k = rot(k_ref[b, :, sl]) * (R_DK ** -0.5)
            vb = v_ref[b, :, sl].astype(BF16)
            qb = q.astype(BF16)
            s = lax.dot_general(qb, k.astype(BF16), NT_DIMS, preferred_element_type=F32) * dec_ref[h]
            inner = jnp.dot(s.astype(BF16), vb, preferred_element_type=F32)
            st = st_s[b, h]
            cross = jnp.dot(qb, st.astype(BF16), preferred_element_type=F32) * xi_ref[h]
            kz = (k * zeta_ref[h]).astype(BF16)
            st_s[b, h] = st * gch_ref[h] + jnp.dot(kz.T, vb, preferred_element_type=F32)
            out = inner + cross
            mu = jnp.mean(out, axis=-1, keepdims=True)
            d = out - mu
            var = jnp.mean(d * d, axis=-1, keepdims=True)
            y = d * lax.rsqrt(var + EPS)
            gate = g_ref[b, :, sl]
            o_ref[b, :, sl] = (gate * jax.nn.sigmoid(gate) * y).astype(BF16)


def _retention(proj3):
    B, L, _ = proj3.shape
    C, H = R_CHUNK, R_HEADS
    N = L // C
    d2 = R_DK // 2
    inv_freq = 1.0 / (10000.0 ** jnp.linspace(0.0, 1.0, d2, dtype=F32))
    ang = jnp.arange(L, dtype=F32)[:, None] * inv_freq[None, :]
    cos = jnp.repeat(jnp.cos(ang), 2, axis=1)
    sin = jnp.stack([-jnp.sin(ang), jnp.sin(ang)], axis=-1).reshape(L, R_DK)
    log_g = jnp.log1p(-jnp.exp2(-5.0 - jnp.arange(H, dtype=F32)))
    pos = jnp.arange(C, dtype=F32)
    diff = pos[:, None] - pos[None, :]
    decay = jnp.where(diff >= 0, jnp.exp(jnp.maximum(diff, 0.0)[None] * log_g[:, None, None]), 0.0)
    xi = jnp.broadcast_to(jnp.exp((pos + 1.0)[None, :] * log_g[:, None])[:, :, None], (H, C, R_DV))
    zeta = jnp.broadcast_to(jnp.exp((C - 1.0 - pos)[None, :] * log_g[:, None])[:, :, None], (H, C, R_DK))
    gch = jnp.broadcast_to(jnp.exp(C * log_g)[:, None, None], (H, 1, R_DV))
    W = H * R_DK
    colspec = lambda j: pl.BlockSpec((B, C, W), lambda n: (0, n, j))
    cst = lambda shape: pl.BlockSpec(shape, lambda n: (0,) * len(shape))
    return pl.pallas_call(
        functools.partial(_ret_kernel, B=B),
        grid=(N,),
        in_specs=[colspec(0), colspec(1), colspec(2), colspec(3),
                  pl.BlockSpec((C, R_DK), lambda n: (n, 0)),
                  pl.BlockSpec((C, R_DK), lambda n: (n, 0)),
                  cst((H, C, C)), cst((H, C, R_DV)), cst((H, C, R_DK)), cst((H, 1, R_DV))],
        out_specs=pl.BlockSpec((B, C, W), lambda n: (0, n, 0)),
        out_shape=jax.ShapeDtypeStruct((B, L, W), BF16),
        scratch_shapes=[pltpu.VMEM((B, H, R_DK, R_DV), F32)],
        compiler_params=_cparams(("arbitrary",)),
        name="retention",
    )(proj3, proj3, proj3, proj3, cos, sin, decay, xi, zeta, gch)


HALO = 32


CONV_ROWS = 4 * SUBLANES


def _conv_kernel(cur_ref, prev_ref, w_ref, b_ref, lg_ref, lb_ref, o_ref, ext_s, sh_s, wb_s, y_s, *, tl):
    i = pl.program_id(1)
    D = ext_s.shape[1]

    @pl.when((pl.program_id(0) == 0) & (i == 0))
    def _():
        for j in range(CONV_WIDTH):
            wb_s[j] = jnp.broadcast_to(w_ref[j:j + 1, :], (SUBLANES, D))

    ext_s[HALO:, :] = cur_ref[0]

    @pl.when(i == 0)
    def _():
        ext_s[:HALO, :] = jnp.zeros((HALO, D), F32)

    @pl.when(i > 0)
    def _():
        ext_s[:HALO, :] = prev_ref[0]

    base = HALO - (CONV_WIDTH - 1)
    offs = range(base, base + CONV_WIDTH)
    for r in range(1, SUBLANES):
        span = max(o for o in offs if o % SUBLANES == r) - r + tl
        sh_s[r, :span, :] = ext_s[r:r + span, :]

    def rows(t, carry):
        row0 = pl.multiple_of(t * CONV_ROWS, CONV_ROWS)
        n_sub = CONV_ROWS // SUBLANES
        y = [None] * n_sub
        for o in offs:
            r = o % SUBLANES
            start = pl.multiple_of(row0 + (o - r), SUBLANES)
            x = ext_s[pl.ds(start, CONV_ROWS), :] if r == 0 else sh_s[r, pl.ds(start, CONV_ROWS), :]
            w = wb_s[o - base]
            for k in range(n_sub):
                term = x[k * SUBLANES:(k + 1) * SUBLANES] * w
                y[k] = term if y[k] is None else y[k] + term
        y_s[pl.ds(row0, CONV_ROWS), :] = jnp.concatenate(y, axis=0)
        return carry

    lax.fori_loop(0, tl // CONV_ROWS, rows, 0)

    y = y_s[...] + b_ref[...]
    mu = jnp.mean(y, axis=-1, keepdims=True)
    d = y - mu
    var = jnp.mean(d * d, axis=-1, keepdims=True)
    z = d * lax.rsqrt(var + EPS) * lg_ref[...] + lb_ref[...]
    o_ref[0] = (z * jax.nn.sigmoid(z)).astype(BF16)


def _conv_ln(a3, w_dw, b_dw, ln_g, ln_b, tl=256):
    B, L, D = a3.shape
    r = tl // HALO
    row = lambda v: v.reshape(1, D)
    return pl.pallas_call(
        functools.partial(_conv_kernel, tl=tl),
        grid=(B, L // tl),
        in_specs=[pl.BlockSpec((1, tl, D), lambda b, i: (b, i, 0)),
                  pl.BlockSpec((1, HALO, D), lambda b, i: (b, jnp.maximum(i * r - 1, 0), 0)),
                  pl.BlockSpec((CONV_WIDTH, D), lambda b, i: (0, 0)),
                  pl.BlockSpec((1, D), lambda b, i: (0, 0)),
                  pl.BlockSpec((1, D), lambda b, i: (0, 0)),
                  pl.BlockSpec((1, D), lambda b, i: (0, 0))],
        out_specs=pl.BlockSpec((1, tl, D), lambda b, i: (b, i, 0)),
        out_shape=jax.ShapeDtypeStruct((B, L, D), BF16),
        scratch_shapes=[pltpu.VMEM((tl + HALO, D), F32),
                        pltpu.VMEM((SUBLANES, tl + HALO, D), F32),
                        pltpu.VMEM((CONV_WIDTH, SUBLANES, D), F32),
                        pltpu.VMEM((tl, D), F32)],
        compiler_params=_cparams(("arbitrary", "arbitrary")),
        name="conv_ln",
    )(a3, a3, w_dw, row(b_dw), row(ln_g), row(ln_b))


def _take_top(s, n):
    rows = []
    for _ in range(n):
        mx = jnp.max(s, axis=0, keepdims=True)
        rows.append(mx)
        s = jnp.where(s == mx, -jnp.inf, s)
    return rows, s


def _take_top_ranked(s, n):
    rows = []
    rank = jnp.full(s.shape, float(n), F32)
    for r in range(n):
        mx = jnp.max(s, axis=0, keepdims=True)
        rows.append(mx)
        hit = s == mx
        rank = jnp.where(hit, float(r), rank)
        s = jnp.where(hit, -jnp.inf, s)
    return rows, rank


def _select_pairs(s1, s2):
    n = P_TOPK
    tt = s1.shape[-1]
    r1, _ = _take_top(s1, n)
    r2, rank2 = _take_top_ranked(s2, n)
    v2 = jnp.concatenate(r2, axis=0)
    blocks = [r1[0] + v2]
    rows8 = lax.broadcasted_iota(I32, (SUBLANES, tt), 0)
    for i in range(1, SUBLANES):
        blocks.append(jnp.where(rows8 < n // (i + 1), r1[i] + v2[:SUBLANES], -jnp.inf))
    blocks.append(jnp.concatenate(r1[SUBLANES:], axis=0) + r2[0])
    top, _ = _take_top(jnp.concatenate(blocks, axis=0), n)
    kth = top[-1]
    z = None
    for r in top:
        e = jnp.exp(r - top[0])
        z = e if z is None else z + e
    hits = [jnp.where(blk >= kth, 1.0, 0.0) for blk in blocks]
    per_rank = [jnp.sum(hb, axis=0, keepdims=True) for hb in hits[:SUBLANES]]
    per_rank += [hits[SUBLANES][k:k + 1] for k in range(n - SUBLANES)]
    n1 = jnp.zeros(s1.shape, F32)
    for i in range(n):
        n1 = jnp.where(s1 == r1[i], per_rank[i], n1)
    e2 = jnp.exp(s2 - r2[0]).astype(BF16)
    coef = jnp.exp(s1 - r1[0]) * (0.5 / z)
    return rank2.astype(BF16), n1, e2, coef


def _peer_score_kernel(x_ref, g_ref, whi_ref, wlo_ref, k3_ref, h_ref, st_ref, hi_s, lo_s):
    @pl.when(pl.program_id(1) == 0)
    def _():
        hi, lo = _split(_rms(x_ref[...], g_ref[...]))
        hi_s[...] = hi
        lo_s[...] = lo
        h_ref[...] = hi

    q = _dot3(hi_s[...], lo_s[...], whi_ref[...], wlo_ref[...])
    q_hi, q_lo = _split(q)
    half = P_QDIM // 2
    for s in range(q.shape[1] // half):
        cols = slice(s * half, (s + 1) * half)
        q3 = jnp.concatenate([q_hi[:, cols], q_lo[:, cols], q_hi[:, cols]], axis=1)
        st_ref[s // 2, s % 2] = lax.dot_general(k3_ref[s % 2], q3, NT_DIMS, preferred_element_type=F32)


def _peer_scores(x, g, wq, keys, tm=512, hps=4):
    T, D = x.shape
    whi, wlo = _split(wq)
    k_hi, k_lo = _split(keys)
    k3 = jnp.concatenate([k_hi, k_hi, k_lo], axis=-1)
    return pl.pallas_call(
        _peer_score_kernel,
        grid=(T // tm, P_HEADS // hps),
        in_specs=[pl.BlockSpec((tm, D), lambda i, j: (i, 0)),
                  pl.BlockSpec((1, D), lambda i, j: (0, 0)),
                  pl.BlockSpec((D, hps * P_QDIM), lambda i, j: (0, j)),
                  pl.BlockSpec((D, hps * P_QDIM), lambda i, j: (0, j)),
                  pl.BlockSpec(k3.shape, lambda i, j: (0, 0, 0))],
        out_specs=[pl.BlockSpec((tm, D), lambda i, j: (i, 0)),
                   pl.BlockSpec((hps, 2, P_NKEYS, tm), lambda i, j: (j, 0, 0, i))],
        out_shape=[jax.ShapeDtypeStruct((T, D), BF16),
                   jax.ShapeDtypeStruct((P_HEADS, 2, P_NKEYS, T), F32)],
        scratch_shapes=[pltpu.VMEM((tm, D), BF16), pltpu.VMEM((tm, D), BF16)],
        compiler_params=_cparams(("parallel", "arbitrary")),
        name="peer_scores",
    )(x, g.reshape(1, D), whi, wlo, k3)


def _peer_topk_kernel(st_ref, rank2_ref, n1_ref, e2_ref, coef_ref):
    def head(h, carry):
        rank2_ref[h], n1_ref[h], e2_ref[h], coef_ref[h] = _select_pairs(st_ref[h, 0], st_ref[h, 1])
        return carry

    lax.fori_loop(0, P_HEADS, head, 0)


def _peer_topk(st, tt=256):
    T = st.shape[-1]
    spec = pl.BlockSpec((P_HEADS, P_NKEYS, tt), lambda i: (0, 0, i))
    shape = lambda dt: jax.ShapeDtypeStruct((P_HEADS, P_NKEYS, T), dt)
    return pl.pallas_call(
        _peer_topk_kernel,
        grid=(T // tt,),
        in_specs=[pl.BlockSpec((P_HEADS, 2, P_NKEYS, tt), lambda i: (0, 0, 0, i))],
        out_specs=[spec, spec, spec, spec],
        out_shape=[shape(BF16), shape(F32), shape(BF16), shape(F32)],
        compiler_params=_cparams(("parallel",)),
        name="peer_topk",
    )(st)


TILE_ROWS = 32
PACKED_ROWS = 16
UP_ROWS = 2
DENSE_ROWS = 16


def _peer_dense_kernel(h_ref, u_ref, vt_ref, rank_ref, e2_ref, n1_ref, coef_ref, res_ref, fg_ref, o_ref,
                       bn_s, bcoef_s, acc_s, *, te, tt, final):
    j = pl.program_id(1)
    NK = P_NKEYS

    @pl.when(j == 0)
    def _():
        acc_s[...] = jnp.zeros(acc_s.shape, F32)

    n_rows = te // NK
    reps = TILE_ROWS // PACKED_ROWS
    for h in range(P_HEADS):
        for grp in range(n_rows // SUBLANES):
            a0 = pl.multiple_of(j * n_rows + grp * SUBLANES, SUBLANES)
            n8 = n1_ref[h, pl.ds(a0, SUBLANES), :]
            coef8 = coef_ref[h, pl.ds(a0, SUBLANES), :]
            for r in range(SUBLANES):
                al = grp * SUBLANES + r
                bn_s[h, al] = jnp.broadcast_to(n8[r:r + 1], (PACKED_ROWS, tt)).astype(BF16)
                bcoef_s[h, al] = jnp.broadcast_to(coef8[r:r + 1], (PACKED_ROWS, tt)).astype(BF16)

    acts = []
    for al in range(n_rows):
        if al % UP_ROWS == 0:
            hb = lax.dot_general(u_ref[al * NK:(al + UP_ROWS) * NK, :], h_ref[...], NT_DIMS,
                                 preferred_element_type=F32)
        for b0 in range(0, NK, TILE_ROWS):
            sub = slice(b0, b0 + TILE_ROWS)
            g = None
            for h in range(P_HEADS):
                pairs = jnp.concatenate([bn_s[h, al]] * reps, axis=0)
                coef = jnp.concatenate([bcoef_s[h, al]] * reps, axis=0)
                e2 = e2_ref[h, sub, :]
                t = jnp.where(rank_ref[h, sub, :] < pairs, e2, jnp.zeros_like(e2)) * coef
                g = t if g is None else g + t
            r0 = (al % UP_ROWS) * NK + b0
            x = hb[r0:r0 + TILE_ROWS, :]
            acts.append((x * (1.0 + lax.erf(x * (2.0 ** -0.5)))).astype(BF16) * g)
    act = jnp.concatenate(acts, axis=0)
    acc_s[...] += lax.dot_general(vt_ref[...], act, (((0,), (0,)), ((), ())),
                                  preferred_element_type=F32)

    @pl.when(j == pl.num_programs(1) - 1)
    def _():
        y = res_ref[...] + acc_s[...].T
        if final:
            y = _rms(y, fg_ref[...])
        o_ref[...] = y


def _peer_dense(hn, u, vt, sel, res, fgain, final, tt=512):
    T, D = res.shape
    E = u.shape[0]
    te = DENSE_ROWS * P_NKEYS
    n_tiles = E // te
    kern = functools.partial(_peer_dense_kernel, te=te, tt=tt, final=final)
    per_token = pl.BlockSpec((P_HEADS, P_NKEYS, tt), lambda i, j: (0, 0, i))
    return pl.pallas_call(
        kern,
        grid=(T // tt, n_tiles),
        in_specs=[pl.BlockSpec((tt, D), lambda i, j: (i, 0)),
                  pl.BlockSpec((te, D), lambda i, j: (j, 0)),
                  pl.BlockSpec((te, D), lambda i, j: (j, 0)),
                  per_token, per_token, per_token, per_token,
                  pl.BlockSpec((tt, D), lambda i, j: (i, 0)),
                  pl.BlockSpec((1, D), lambda i, j: (0, 0))],
        out_specs=pl.BlockSpec((tt, D), lambda i, j: (i, 0)),
        out_shape=jax.ShapeDtypeStruct((T, D), F32),
        scratch_shapes=[pltpu.VMEM((P_HEADS, DENSE_ROWS, PACKED_ROWS, tt), BF16),
                        pltpu.VMEM((P_HEADS, DENSE_ROWS, PACKED_ROWS, tt), BF16),
                        pltpu.VMEM((D, tt), F32)],
        compiler_params=_cparams(("parallel", "arbitrary")),
        name="peer_dense",
    )(hn, u, vt, *sel, res, fgain.reshape(1, D))


def _peer(x, g, wq, keys, u, v, fgain, final):
    hn, st = _peer_scores(x, g, wq, keys)
    rank2, n1, e2, coef = _peer_topk(st)
    return _peer_dense(hn, u.astype(BF16), v.astype(BF16), (rank2, e2, n1, coef), x, fgain, final)


def _block_diag(blocks):
    H, r, c = blocks.shape
    eye = jnp.eye(H, dtype=blocks.dtype)
    return (eye[:, None, :, None] * blocks[:, :, None, :]).reshape(H * r, H * c)


def _in_proj_weights(w_in):
    sizes = (A_HEADS * A_HEAD_DIM, A_LATENT, IDX_HEADS * IDX_DIM, IDX_DIM, IDX_HEADS,
             R_HEADS * R_DK, R_HEADS * R_DK, R_HEADS * R_DV, R_HEADS * R_DV)
    pts = np.cumsum(sizes)[:-1].tolist()
    qa, ckv, qi, ki, wi, rq, rk, rv, rg = jnp.split(w_in, pts, axis=1)
    pad = jnp.zeros((w_in.shape[0], LANES - IDX_DIM - IDX_HEADS), w_in.dtype)
    return (jnp.concatenate([rq, rk, rv, rg, qa, ckv], axis=1),
            jnp.concatenate([qi, ki, wi, pad], axis=1))


def kernel(x, mix_norm_e, w_in, kv_norm, w_uk, w_uv, w_o, mix_norm_o, conv_w1, conv_b1, conv_dw, conv_dw_b, conv_ln_g, conv_ln_b, conv_w2, conv_b2, ffn_norm, peer_wq, peer_keys, peer_u, peer_v, final_norm):
    B, L, D = x.shape
    T = B * L
    depth = ffn_norm.shape[0]
    xf = x.reshape(T, D)
    for layer in range(depth):
        j = layer // 2
        if layer % 2 == 0:
            proj, idx = _in_proj(xf, mix_norm_e[j], *_in_proj_weights(w_in[j]))
            proj3 = proj.reshape(B, L, proj.shape[1])
            idx3 = idx.reshape(B, L, idx.shape[1])
            wuk_bd = _block_diag(jnp.swapaxes(w_uk[j], 1, 2)).astype(BF16)
            wuv_bd = _block_diag(w_uv[j]).astype(BF16)
            a_out = _dsa(proj3, idx3, kv_norm[j], wuk_bd, wuv_bd)
            b_out = _retention(proj3)
            wo = w_o[j].astype(BF16)
            na = A_HEADS * A_HEAD_DIM
            xf = _mm_res([a_out.reshape(T, na), b_out.reshape(T, -1)], [wo[:na], wo[na:]], xf)
        else:
            a = _norm_glu(xf, mix_norm_o[j], conv_w1[j].astype(BF16), conv_b1[j])
            y = _conv_ln(a.reshape(B, L, D), conv_dw[j], conv_dw_b[j], conv_ln_g[j], conv_ln_b[j])
            xf = _mm_res([y.reshape(T, D)], [conv_w2[j].astype(BF16)], xf, bias=conv_b2[j])
        xf = _peer(xf, ffn_norm[layer], peer_wq[layer], peer_keys[layer], peer_u[layer], peer_v[layer],
                   final_norm, final=(layer == depth - 1))
    return xf.reshape(B, L, D)
```

```python
import functools

import jax
import jax.numpy as jnp
import numpy as np
from jax import lax
from jax.experimental import pallas as pl
from jax.experimental.pallas import tpu as pltpu

F32 = jnp.float32
BF16 = jnp.bfloat16
I32 = jnp.int32

EPS = 1e-6
A_HEADS, A_HEAD_DIM, A_LATENT = 8, 64, 128
IDX_HEADS, IDX_DIM, TOPK_MAX = 8, 32, 256
R_HEADS, R_DK, R_DV, R_CHUNK = 4, 128, 128, 128
CONV_WIDTH = 31
P_HEADS, P_NKEYS, P_QDIM, P_TOPK = 8, 128, 256, 16

LANES = 128
SUBLANES = 8
VMEM_LIMIT = 56 * 1024 * 1024
INT_MIN = -(2 ** 31)
NEG_BIG = -1e30
LOG2_E = 1.4426950408889634

NT_DIMS = (((1,), (1,)), ((), ()))


def _cparams(sem, flags=None):
    return pltpu.CompilerParams(dimension_semantics=sem, vmem_limit_bytes=VMEM_LIMIT, flags=flags)


def _aligned(x, m):
    if isinstance(x, int):
        assert x % m == 0
        return x
    return pl.multiple_of(x, m)


def _rms(x, g):
    return x * lax.rsqrt(jnp.mean(x * x, axis=-1, keepdims=True) + EPS) * g


def _split(x):
    hi = x.astype(BF16)
    return hi, (x - hi.astype(F32)).astype(BF16)


def _dot3(a_hi, a_lo, b_hi, b_lo):
    d = functools.partial(jnp.dot, preferred_element_type=F32)
    return d(a_hi, b_hi) + (d(a_hi, b_lo) + d(a_lo, b_hi))


def _in_proj_kernel(x_ref, g_ref, w_ref, whi_ref, wlo_ref, o_ref, oi_ref):
    h = _rms(x_ref[...], g_ref[...])
    hi, lo = _split(h)
    o_ref[...] = jnp.dot(hi, w_ref[...], preferred_element_type=F32)
    oi_ref[...] = _dot3(hi, lo, whi_ref[...], wlo_ref[...])


def _in_proj(x, g, w_main, w_idx, tm=512):
    T, D = x.shape
    N, NI = w_main.shape[1], w_idx.shape[1]
    whi, wlo = _split(w_idx)
    return pl.pallas_call(
        _in_proj_kernel,
        grid=(T // tm,),
        in_specs=[pl.BlockSpec((tm, D), lambda i: (i, 0)),
                  pl.BlockSpec((1, D), lambda i: (0, 0)),
                  pl.BlockSpec((D, N), lambda i: (0, 0)),
                  pl.BlockSpec((D, NI), lambda i: (0, 0)),
                  pl.BlockSpec((D, NI), lambda i: (0, 0))],
        out_specs=[pl.BlockSpec((tm, N), lambda i: (i, 0)),
                   pl.BlockSpec((tm, NI), lambda i: (i, 0))],
        out_shape=[jax.ShapeDtypeStruct((T, N), F32), jax.ShapeDtypeStruct((T, NI), F32)],
        compiler_params=_cparams(("parallel",)),
        name="in_proj",
    )(x, g.reshape(1, D), w_main.astype(BF16), whi, wlo)


def _norm_glu_kernel(x_ref, g_ref, wa_ref, wg_ref, ba_ref, bg_ref, o_ref, h_scr):
    @pl.when(pl.program_id(1) == 0)
    def _():
        h_scr[...] = _rms(x_ref[...], g_ref[...]).astype(BF16)

    h = h_scr[...]
    a = jnp.dot(h, wa_ref[...], preferred_element_type=F32) + ba_ref[...]
    gate = jnp.dot(h, wg_ref[...], preferred_element_type=F32) + bg_ref[...]
    o_ref[...] = a * jax.nn.sigmoid(gate)


def _norm_glu(x, g, w1, b1, tm=512, tn=1024):
    T, D = x.shape
    nj = D // tn
    b1 = b1.reshape(1, 2 * D)
    return pl.pallas_call(
        _norm_glu_kernel,
        grid=(T // tm, nj),
        in_specs=[pl.BlockSpec((tm, D), lambda i, j: (i, 0)),
                  pl.BlockSpec((1, D), lambda i, j: (0, 0)),
                  pl.BlockSpec((D, tn), lambda i, j: (0, j)),
                  pl.BlockSpec((D, tn), lambda i, j: (0, j + nj)),
                  pl.BlockSpec((1, tn), lambda i, j: (0, j)),
                  pl.BlockSpec((1, tn), lambda i, j: (0, j + nj))],
        out_specs=pl.BlockSpec((tm, tn), lambda i, j: (i, j)),
        out_shape=jax.ShapeDtypeStruct((T, D), F32),
        scratch_shapes=[pltpu.VMEM((tm, D), BF16)],
        compiler_params=_cparams(("parallel", "arbitrary")),
        name="norm_glu",
    )(x, g.reshape(1, D), w1, w1, b1, b1)


def _mm_res_kernel(*refs, n_lhs, has_bias):
    lhs = refs[:n_lhs]
    ws = refs[n_lhs:2 * n_lhs]
    rest = refs[2 * n_lhs:]
    if has_bias:
        b_ref, res_ref, o_ref = rest
        acc = res_ref[...] + b_ref[...]
    else:
        res_ref, o_ref = rest
        acc = res_ref[...]
    for a_ref, w_ref in zip(lhs, ws):
        acc = acc + jnp.dot(a_ref[...], w_ref[...], preferred_element_type=F32)
    o_ref[...] = acc


def _mm_res(lhs, ws, res, bias=None, tm=512):
    T, N = res.shape
    n = len(lhs)
    in_specs = [pl.BlockSpec((tm, a.shape[1]), lambda i: (i, 0)) for a in lhs]
    in_specs += [pl.BlockSpec(w.shape, lambda i: (0, 0)) for w in ws]
    args = list(lhs) + list(ws)
    if bias is not None:
        in_specs.append(pl.BlockSpec((1, N), lambda i: (0, 0)))
        args.append(bias.reshape(1, N))
    in_specs.append(pl.BlockSpec((tm, N), lambda i: (i, 0)))
    args.append(res)
    return pl.pallas_call(
        functools.partial(_mm_res_kernel, n_lhs=n, has_bias=bias is not None),
        grid=(T // tm,),
        in_specs=in_specs,
        out_specs=pl.BlockSpec((tm, N), lambda i: (i, 0)),
        out_shape=jax.ShapeDtypeStruct((T, N), F32),
        compiler_params=_cparams(("parallel",)),
        name="mm_res",
    )(*args)


def _dsa_kernel(qa_ref, qi_ref, misc_ref, ckv_ref, kit_ref, kvn_ref, wuk_ref, wuv_ref, o_ref,
                c_s, key_s, qst_s, qlat_s, wb_s, lg_s, bias_s, cmax_s, p_s, m_s, l_s, acc_s,
                *, Q, KC, topk, col_bits):
    qb = pl.program_id(1)
    H = A_HEADS
    NL = KC // LANES

    @pl.when(qb == 0)
    def _():
        c_s[...] = _rms(ckv_ref[0], kvn_ref[...]).astype(BF16)

    qi = qi_ref[0]
    wi = misc_ref[0][:, IDX_DIM:IDX_DIM + IDX_HEADS] * (IDX_HEADS ** -0.5)
    for h in range(IDX_HEADS):
        hi, lo = _split(qi[:, h * IDX_DIM:(h + 1) * IDX_DIM])
        qst_s[h * Q:(h + 1) * Q, :] = jnp.concatenate([hi, lo, hi], axis=1)
        wb_s[h] = jnp.broadcast_to(wi[:, h:h + 1], (Q, LANES))
    ql = jnp.dot(qa_ref[0].astype(BF16), wuk_ref[...], preferred_element_type=F32) * (A_HEAD_DIM ** -0.5 * LOG2_E)
    for h in range(H):
        qlat_s[h * Q:(h + 1) * Q, :] = ql[:, h * A_LATENT:(h + 1) * A_LATENT].astype(BF16)

    n_chunks = ((qb + 1) * Q + KC - 1) // KC
    q_pos = qb * Q + lax.broadcasted_iota(I32, (Q, LANES), 0)
    lane = lax.broadcasted_iota(I32, (Q, LANES), 1)

    RG = 4 * SUBLANES
    lane_rg = lax.broadcasted_iota(I32, (RG, LANES), 1)
    row_rg = lax.broadcasted_iota(I32, (RG, LANES), 0)

    def score_chunk(c, carry):
        off = pl.multiple_of(c * KC, KC)
        lg_s[...] = jnp.dot(qst_s[...], kit_ref[0, :, pl.ds(off, KC)], preferred_element_type=F32)
        for r0 in range(0, Q, RG):
            a = None
            for h in range(IDX_HEADS):
                w = jnp.concatenate([wb_s[h, r0:r0 + RG, :]] * NL, axis=1)
                t = jnp.maximum(lg_s[h * Q + r0:h * Q + r0 + RG, :], 0.0) * w
                a = t if a is None else a + t
            bits = pltpu.bitcast(a, I32)
            sgn = bits >> 31
            key = (bits ^ (sgn & 0x7FFFFFFF)) - sgn
            for l in range(NL):
                causal = off + l * LANES + lane_rg <= qb * Q + r0 + row_rg
                key_s[c * NL + l, r0:r0 + RG, :] = jnp.where(causal, key[:, l * LANES:(l + 1) * LANES], INT_MIN)
        return carry

    lax.fori_loop(0, n_chunks, score_chunk, 0)

    SR = min(Q, 16 * SUBLANES)
    lane_sr = lax.broadcasted_iota(I32, (SR, LANES), 1)

    def row_count(pred):
        cnts = []
        for r0 in range(0, Q, SR):
            def body(c, cnt, r0=r0):
                for l in range(NL):
                    blk = key_s[c * NL + l, r0:r0 + SR, :]
                    cnt = cnt + jnp.where(pred(blk, r0, c * KC + l * LANES), 1, 0)
                return cnt

            cnts.append(lax.fori_loop(0, n_chunks, body, jnp.zeros((SR, LANES), I32)))
        return jnp.sum(jnp.concatenate(cnts, axis=0).astype(F32), axis=1, keepdims=True)

    def bit_body(i, carry):
        tau, n_at = carry
        cand = tau | lax.shift_left(jnp.int32(1), 31 - i)
        cand_s = cand ^ INT_MIN
        tot = row_count(lambda blk, r0, col0: blk >= cand_s[r0:r0 + SR])
        take = tot >= topk
        return jnp.where(take, cand, tau), jnp.where(take, tot, n_at)

    zero = jnp.zeros((Q, LANES), I32)
    tau, n_ge = lax.fori_loop(0, 32, bit_body, (zero, jnp.zeros((Q, 1), F32)))
    thr = jnp.maximum(tau ^ INT_MIN, INT_MIN + 1)

    excess = n_ge > topk
    cmax_s[...] = jnp.full((Q, LANES), 2 ** 31 - 1, I32)

    @pl.when(jnp.max(jnp.where(excess, 1.0, 0.0)) > 0.0)
    def _():
        need = topk - row_count(lambda blk, r0, col0: blk > thr[r0:r0 + SR])

        def col_bit(i, y):
            cand = y | lax.shift_left(jnp.int32(1), col_bits - 1 - i)
            tot = row_count(lambda blk, r0, col0:
                            (blk == thr[r0:r0 + SR]) & (col0 + lane_sr < cand[r0:r0 + SR]))
            return jnp.where(tot < need, cand, y)

        y = lax.fori_loop(0, col_bits, col_bit, zero)
        cmax_s[...] = jnp.where(excess, y, 2 ** 31 - 1)

    cmax = cmax_s[...]
    m_s[...] = jnp.full(m_s.shape, NEG_BIG, F32)
    l_s[...] = jnp.zeros(l_s.shape, F32)
    acc_s[...] = jnp.zeros(acc_s.shape, F32)

    def att_chunk(c, carry):
        off = pl.multiple_of(c * KC, KC)
        ck = c_s[pl.ds(off, KC), :]
        lg_s[...] = lax.dot_general(qlat_s[...], ck, NT_DIMS, preferred_element_type=F32)
        for l in range(NL):
            blk = key_s[c * NL + l]
            sel = (blk > thr) | ((blk == thr) & (off + l * LANES + lane <= cmax))
            bias_s[l] = jnp.where(sel, 0.0, NEG_BIG)

        for h in range(H):
            for r0 in range(0, Q, RG):
                rows = slice(h * Q + r0, h * Q + r0 + RG)
                m_prev = m_s[rows, :]
                lg = lg_s[rows, :]
                lgs = [lg[:, l * LANES:(l + 1) * LANES] + bias_s[l, r0:r0 + RG, :] for l in range(NL)]
                mx = lgs[0]
                for l in range(1, NL):
                    mx = jnp.maximum(mx, lgs[l])
                m_new = jnp.maximum(m_prev, jnp.max(mx, axis=1, keepdims=True))
                alpha = jnp.exp2(m_prev - m_new)
                ps = [jnp.exp2(x - m_new) for x in lgs]
                p_s[rows, :] = jnp.concatenate(ps, axis=1).astype(BF16)
                psum = ps[0]
                for l in range(1, NL):
                    psum = psum + ps[l]
                l_s[rows, :] = alpha * l_s[rows, :] + psum
                m_s[rows, :] = m_new
                acc_s[rows, :] = acc_s[rows, :] * alpha
        acc_s[...] += jnp.dot(p_s[...], ck, preferred_element_type=F32)
        return carry

    lax.fori_loop(0, n_chunks, att_chunk, 0)

    o = acc_s[...] / jnp.sum(l_s[...], axis=1, keepdims=True)
    o_cat = jnp.concatenate([o[h * Q:(h + 1) * Q, :] for h in range(H)], axis=1).astype(BF16)
    o_ref[0] = jnp.dot(o_cat, wuv_ref[...], preferred_element_type=F32).astype(BF16)


def _dsa(proj3, idx3, kv_norm, wuk_bd, wuv_bd, Q=256, KC=512):
    B, L, _ = proj3.shape
    H = A_HEADS
    topk = min(TOPK_MAX, L // 4)
    qa_w, qi_w = A_HEADS * A_HEAD_DIM, IDX_HEADS * IDX_DIM
    qa_col = 4 * R_HEADS * R_DK
    k_hi, k_lo = _split(jnp.swapaxes(idx3[:, :, qi_w:qi_w + IDX_DIM], 1, 2))
    kit = jnp.concatenate([k_hi, k_hi, k_lo], axis=1)
    kern = functools.partial(_dsa_kernel, Q=Q, KC=KC, topk=topk, col_bits=max(1, (L - 1).bit_length()))
    return pl.pallas_call(
        kern,
        grid=(B, L // Q),
        in_specs=[pl.BlockSpec((1, Q, qa_w), lambda b, q: (b, q, qa_col // qa_w)),
                  pl.BlockSpec((1, Q, qi_w), lambda b, q: (b, q, 0)),
                  pl.BlockSpec((1, Q, LANES), lambda b, q: (b, q, qi_w // LANES)),
                  pl.BlockSpec((1, L, A_LATENT), lambda b, q: (b, 0, (qa_col + qa_w) // A_LATENT)),
                  pl.BlockSpec((1, 3 * IDX_DIM, L), lambda b, q: (b, 0, 0)),
                  pl.BlockSpec((1, A_LATENT), lambda b, q: (0, 0)),
                  pl.BlockSpec(wuk_bd.shape, lambda b, q: (0, 0)),
                  pl.BlockSpec(wuv_bd.shape, lambda b, q: (0, 0))],
        out_specs=pl.BlockSpec((1, Q, H * A_HEAD_DIM), lambda b, q: (b, q, 0)),
        out_shape=jax.ShapeDtypeStruct((B, L, H * A_HEAD_DIM), BF16),
        scratch_shapes=[pltpu.VMEM((L, A_LATENT), BF16),
                        pltpu.VMEM((L // LANES, Q, LANES), I32),
                        pltpu.VMEM((IDX_HEADS * Q, 3 * IDX_DIM), BF16),
                        pltpu.VMEM((H * Q, A_LATENT), BF16),
                        pltpu.VMEM((IDX_HEADS, Q, LANES), F32),
                        pltpu.VMEM((H * Q, KC), F32),
                        pltpu.VMEM((KC // LANES, Q, LANES), F32),
                        pltpu.VMEM((Q, LANES), I32),
                        pltpu.VMEM((H * Q, KC), BF16),
                        pltpu.VMEM((H * Q, LANES), F32),
                        pltpu.VMEM((H * Q, LANES), F32),
                        pltpu.VMEM((H * Q, A_LATENT), F32)],
        compiler_params=_cparams(("arbitrary", "arbitrary")),
        name="dsa",
    )(proj3, idx3, idx3, proj3, kit, kv_norm.reshape(1, A_LATENT), wuk_bd, wuv_bd)


def _ret_kernel(q_ref, k_ref, v_ref, g_ref, cos_ref, sin_ref, dec_ref, xi_ref, zeta_ref, gch_ref,
                o_ref, st_s, *, B):
    @pl.when(pl.program_id(0) == 0)
    def _():
        st_s[...] = jnp.zeros(st_s.shape, F32)

    C = R_CHUNK
    cos = cos_ref[...]
    sin = sin_ref[...]
    even = (lax.broadcasted_iota(I32, (C, R_DK), 1) & 1) == 0

    def rot(x):
        partner = jnp.where(even, pltpu.roll(x, R_DK - 1, 1), pltpu.roll(x, 1, 1))
        return x * cos + partner * sin

    for b in range(B):
        for h in range(R_HEADS):
            sl = slice(h * R_DK, (h + 1) * R_DK)
            q = rot(q_ref[b, :, sl])
            k = rot(k_ref[b, :, sl]) * (R_DK ** -0.5)
            vb = v_ref[b, :, sl].astype(BF16)
            qb = q.astype(BF16)
            s = lax.dot_general(qb, k.astype(BF16), NT_DIMS, preferred_element_type=F32) * dec_ref[h]
            inner = jnp.dot(s.astype(BF16), vb, preferred_element_type=F32)
            st = st_s[b, h]
            cross = jnp.dot(qb, st.astype(BF16), preferred_element_type=F32) * xi_ref[h]
            kz = (k * zeta_ref[h]).astype(BF16)
            st_s[b, h] = st * gch_ref[h] + jnp.dot(kz.T, vb, preferred_element_type=F32)
            out = inner + cross
            mu = jnp.mean(out, axis=-1, keepdims=True)
            d = out - mu
            var = jnp.mean(d * d, axis=-1, keepdims=True)
            y = d * lax.rsqrt(var + EPS)
            gate = g_ref[b, :, sl]
            o_ref[b, :, sl] = (gate * jax.nn.sigmoid(gate) * y).astype(BF16)


def _retention(proj3):
    B, L, _ = proj3.shape
    C, H = R_CHUNK, R_HEADS
    N = L // C
    d2 = R_DK // 2
    inv_freq = 1.0 / (10000.0 ** jnp.linspace(0.0, 1.0, d2, dtype=F32))
    ang = jnp.arange(L, dtype=F32)[:, None] * inv_freq[None, :]
    cos = jnp.repeat(jnp.cos(ang), 2, axis=1)
    sin = jnp.stack([-jnp.sin(ang), jnp.sin(ang)], axis=-1).reshape(L, R_DK)
    log_g = jnp.log1p(-jnp.exp2(-5.0 - jnp.arange(H, dtype=F32)))
    pos = jnp.arange(C, dtype=F32)
    diff = pos[:, None] - pos[None, :]
    decay = jnp.where(diff >= 0, jnp.exp(jnp.maximum(diff, 0.0)[None] * log_g[:, None, None]), 0.0)
    xi = jnp.broadcast_to(jnp.exp((pos + 1.0)[None, :] * log_g[:, None])[:, :, None], (H, C, R_DV))
    zeta = jnp.broadcast_to(jnp.exp((C - 1.0 - pos)[None, :] * log_g[:, None])[:, :, None], (H, C, R_DK))
    gch = jnp.broadcast_to(jnp.exp(C * log_g)[:, None, None], (H, 1, R_DV))
    W = H * R_DK
    colspec = lambda j: pl.BlockSpec((B, C, W), lambda n: (0, n, j))
    cst = lambda shape: pl.BlockSpec(shape, lambda n: (0,) * len(shape))
    return pl.pallas_call(
        functools.partial(_ret_kernel, B=B),
        grid=(N,),
        in_specs=[colspec(0), colspec(1), colspec(2), colspec(3),
                  pl.BlockSpec((C, R_DK), lambda n: (n, 0)),
                  pl.BlockSpec((C, R_DK), lambda n: (n, 0)),
                  cst((H, C, C)), cst((H, C, R_DV)), cst((H, C, R_DK)), cst((H, 1, R_DV))],
        out_specs=pl.BlockSpec((B, C, W), lambda n: (0, n, 0)),
        out_shape=jax.ShapeDtypeStruct((B, L, W), BF16),
        scratch_shapes=[pltpu.VMEM((B, H, R_DK, R_DV), F32)],
        compiler_params=_cparams(("arbitrary",)),
        name="retention",
    )(proj3, proj3, proj3, proj3, cos, sin, decay, xi, zeta, gch)


HALO = 32


CONV_ROWS = 4 * SUBLANES


def _conv_kernel(cur_ref, prev_ref, w_ref, b_ref, lg_ref, lb_ref, o_ref, ext_s, sh_s, wb_s, y_s, *, tl):
    i = pl.program_id(1)
    D = ext_s.shape[1]

    @pl.when((pl.program_id(0) == 0) & (i == 0))
    def _():
        for j in range(CONV_WIDTH):
            wb_s[j] = jnp.broadcast_to(w_ref[j:j + 1, :], (SUBLANES, D))

    ext_s[HALO:, :] = cur_ref[0]

    @pl.when(i == 0)
    def _():
        ext_s[:HALO, :] = jnp.zeros((HALO, D), F32)

    @pl.when(i > 0)
    def _():
        ext_s[:HALO, :] = prev_ref[0]

    base = HALO - (CONV_WIDTH - 1)
    offs = range(base, base + CONV_WIDTH)
    for r in range(1, SUBLANES):
        span = max(o for o in offs if o % SUBLANES == r) - r + tl
        sh_s[r, :span, :] = ext_s[r:r + span, :]

    def rows(t, carry):
        row0 = pl.multiple_of(t * CONV_ROWS, CONV_ROWS)
        n_sub = CONV_ROWS // SUBLANES
        y = [None] * n_sub
        for o in offs:
            r = o % SUBLANES
            start = pl.multiple_of(row0 + (o - r), SUBLANES)
            x = ext_s[pl.ds(start, CONV_ROWS), :] if r == 0 else sh_s[r, pl.ds(start, CONV_ROWS), :]
            w = wb_s[o - base]
            for k in range(n_sub):
                term = x[k * SUBLANES:(k + 1) * SUBLANES] * w
                y[k] = term if y[k] is None else y[k] + term
        y_s[pl.ds(row0, CONV_ROWS), :] = jnp.concatenate(y, axis=0)
        return carry

    lax.fori_loop(0, tl // CONV_ROWS, rows, 0)

    y = y_s[...] + b_ref[...]
    mu = jnp.mean(y, axis=-1, keepdims=True)
    d = y - mu
    var = jnp.mean(d * d, axis=-1, keepdims=True)
    z = d * lax.rsqrt(var + EPS) * lg_ref[...] + lb_ref[...]
    o_ref[0] = (z * jax.nn.sigmoid(z)).astype(BF16)


def _conv_ln(a3, w_dw, b_dw, ln_g, ln_b, tl=512):
    B, L, D = a3.shape
    r = tl // HALO
    row = lambda v: v.reshape(1, D)
    return pl.pallas_call(
        functools.partial(_conv_kernel, tl=tl),
        grid=(B, L // tl),
        in_specs=[pl.BlockSpec((1, tl, D), lambda b, i: (b, i, 0)),
                  pl.BlockSpec((1, HALO, D), lambda b, i: (b, jnp.maximum(i * r - 1, 0), 0)),
                  pl.BlockSpec((CONV_WIDTH, D), lambda b, i: (0, 0)),
                  pl.BlockSpec((1, D), lambda b, i: (0, 0)),
                  pl.BlockSpec((1, D), lambda b, i: (0, 0)),
                  pl.BlockSpec((1, D), lambda b, i: (0, 0))],
        out_specs=pl.BlockSpec((1, tl, D), lambda b, i: (b, i, 0)),
        out_shape=jax.ShapeDtypeStruct((B, L, D), BF16),
        scratch_shapes=[pltpu.VMEM((tl + HALO, D), F32),
                        pltpu.VMEM((SUBLANES, tl + HALO, D), F32),
                        pltpu.VMEM((CONV_WIDTH, SUBLANES, D), F32),
                        pltpu.VMEM((tl, D), F32)],
        compiler_params=_cparams(("arbitrary", "arbitrary")),
        name="conv_ln",
    )(a3, a3, w_dw, row(b_dw), row(ln_g), row(ln_b))


def _take_top(s, n):
    rows = []
    for _ in range(n):
        mx = jnp.max(s, axis=0, keepdims=True)
        rows.append(mx)
        s = jnp.where(s == mx, -jnp.inf, s)
    return rows, s


def _take_top_ranked(s, n):
    rows = []
    rank = jnp.full(s.shape, float(n), F32)
    for r in range(n):
        mx = jnp.max(s, axis=0, keepdims=True)
        rows.append(mx)
        hit = s == mx
        rank = jnp.where(hit, float(r), rank)
        s = jnp.where(hit, -jnp.inf, s)
    return rows, rank


def _select_pairs(s1, s2):
    n = P_TOPK
    tt = s1.shape[-1]
    r1, _ = _take_top(s1, n)
    r2, rank2 = _take_top_ranked(s2, n)
    v2 = jnp.concatenate(r2, axis=0)
    blocks = [r1[0] + v2]
    rows8 = lax.broadcasted_iota(I32, (SUBLANES, tt), 0)
    for i in range(1, SUBLANES):
        blocks.append(jnp.where(rows8 < n // (i + 1), r1[i] + v2[:SUBLANES], -jnp.inf))
    blocks.append(jnp.concatenate(r1[SUBLANES:], axis=0) + r2[0])
    top, _ = _take_top(jnp.concatenate(blocks, axis=0), n)
    kth = top[-1]
    z = None
    for r in top:
        e = jnp.exp(r - top[0])
        z = e if z is None else z + e
    hits = [jnp.where(blk >= kth, 1.0, 0.0) for blk in blocks]
    per_rank = [jnp.sum(hb, axis=0, keepdims=True) for hb in hits[:SUBLANES]]
    per_rank += [hits[SUBLANES][k:k + 1] for k in range(n - SUBLANES)]
    n1 = jnp.zeros(s1.shape, F32)
    for i in range(n):
        n1 = jnp.where(s1 == r1[i], per_rank[i], n1)
    e2 = jnp.exp(s2 - r2[0]).astype(BF16)
    coef = jnp.exp(s1 - r1[0]) * (0.5 / z)
    return rank2.astype(BF16), n1, e2, coef


def _peer_score_kernel(x_ref, g_ref, whi_ref, wlo_ref, k3_ref, h_ref, st_ref, hi_s, lo_s):
    @pl.when(pl.program_id(1) == 0)
    def _():
        hi, lo = _split(_rms(x_ref[...], g_ref[...]))
        hi_s[...] = hi
        lo_s[...] = lo
        h_ref[...] = hi

    q = _dot3(hi_s[...], lo_s[...], whi_ref[...], wlo_ref[...])
    q_hi, q_lo = _split(q)
    half = P_QDIM // 2
    for s in range(q.shape[1] // half):
        cols = slice(s * half, (s + 1) * half)
        q3 = jnp.concatenate([q_hi[:, cols], q_lo[:, cols], q_hi[:, cols]], axis=1)
        st_ref[s // 2, s % 2] = lax.dot_general(k3_ref[s % 2], q3, NT_DIMS, preferred_element_type=F32)


def _peer_scores(x, g, wq, keys, tm=512, hps=8):
    T, D = x.shape
    whi, wlo = _split(wq)
    k_hi, k_lo = _split(keys)
    k3 = jnp.concatenate([k_hi, k_hi, k_lo], axis=-1)
    return pl.pallas_call(
        _peer_score_kernel,
        grid=(T // tm, P_HEADS // hps),
        in_specs=[pl.BlockSpec((tm, D), lambda i, j: (i, 0)),
                  pl.BlockSpec((1, D), lambda i, j: (0, 0)),
                  pl.BlockSpec((D, hps * P_QDIM), lambda i, j: (0, j)),
                  pl.BlockSpec((D, hps * P_QDIM), lambda i, j: (0, j)),
                  pl.BlockSpec(k3.shape, lambda i, j: (0, 0, 0))],
        out_specs=[pl.BlockSpec((tm, D), lambda i, j: (i, 0)),
                   pl.BlockSpec((hps, 2, P_NKEYS, tm), lambda i, j: (j, 0, 0, i))],
        out_shape=[jax.ShapeDtypeStruct((T, D), BF16),
                   jax.ShapeDtypeStruct((P_HEADS, 2, P_NKEYS, T), F32)],
        scratch_shapes=[pltpu.VMEM((tm, D), BF16), pltpu.VMEM((tm, D), BF16)],
        compiler_params=_cparams(("parallel", "arbitrary")),
        name="peer_scores",
    )(x, g.reshape(1, D), whi, wlo, k3)


def _peer_topk_kernel(st_ref, rank2_ref, n1_ref, e2_ref, coef_ref):
    def head(h, carry):
        rank2_ref[h], n1_ref[h], e2_ref[h], coef_ref[h] = _select_pairs(st_ref[h, 0], st_ref[h, 1])
        return carry

    lax.fori_loop(0, P_HEADS, head, 0)


def _peer_topk(st, tt=256):
    T = st.shape[-1]
    spec = pl.BlockSpec((P_HEADS, P_NKEYS, tt), lambda i: (0, 0, i))
    shape = lambda dt: jax.ShapeDtypeStruct((P_HEADS, P_NKEYS, T), dt)
    return pl.pallas_call(
        _peer_topk_kernel,
        grid=(T // tt,),
        in_specs=[pl.BlockSpec((P_HEADS, 2, P_NKEYS, tt), lambda i: (0, 0, 0, i))],
        out_specs=[spec, spec, spec, spec],
        out_shape=[shape(BF16), shape(F32), shape(BF16), shape(F32)],
        compiler_params=_cparams(("parallel",)),
        name="peer_topk",
    )(st)


TILE_ROWS = 32
PACKED_ROWS = 16
UP_ROWS = 2
DENSE_ROWS = 16


def _peer_dense_kernel(h_ref, u_ref, vt_ref, rank_ref, e2_ref, n1_ref, coef_ref, res_ref, fg_ref, o_ref,
                       bn_s, bcoef_s, acc_s, *, te, tt, final):
    j = pl.program_id(1)
    NK = P_NKEYS

    @pl.when(j == 0)
    def _():
        acc_s[...] = jnp.zeros(acc_s.shape, F32)

    n_rows = te // NK
    reps = TILE_ROWS // PACKED_ROWS
    for h in range(P_HEADS):
        for grp in range(n_rows // SUBLANES):
            a0 = pl.multiple_of(j * n_rows + grp * SUBLANES, SUBLANES)
            n8 = n1_ref[h, pl.ds(a0, SUBLANES), :]
            coef8 = coef_ref[h, pl.ds(a0, SUBLANES), :]
            for r in range(SUBLANES):
                al = grp * SUBLANES + r
                bn_s[h, al] = jnp.broadcast_to(n8[r:r + 1], (PACKED_ROWS, tt)).astype(BF16)
                bcoef_s[h, al] = jnp.broadcast_to(coef8[r:r + 1], (PACKED_ROWS, tt)).astype(BF16)

    acts = []
    for al in range(n_rows):
        if al % UP_ROWS == 0:
            hb = lax.dot_general(u_ref[al * NK:(al + UP_ROWS) * NK, :], h_ref[...], NT_DIMS,
                                 preferred_element_type=F32)
        for b0 in range(0, NK, TILE_ROWS):
            sub = slice(b0, b0 + TILE_ROWS)
            g = None
            for h in range(P_HEADS):
                pairs = jnp.concatenate([bn_s[h, al]] * reps, axis=0)
                coef = jnp.concatenate([bcoef_s[h, al]] * reps, axis=0)
                e2 = e2_ref[h, sub, :]
                t = jnp.where(rank_ref[h, sub, :] < pairs, e2, jnp.zeros_like(e2)) * coef
                g = t if g is None else g + t
            r0 = (al % UP_ROWS) * NK + b0
            x = hb[r0:r0 + TILE_ROWS, :]
            acts.append((x * (1.0 + lax.erf(x * (2.0 ** -0.5)))).astype(BF16) * g)
    act = jnp.concatenate(acts, axis=0)
    acc_s[...] += lax.dot_general(vt_ref[...], act, (((0,), (0,)), ((), ())),
                                  preferred_element_type=F32)

    @pl.when(j == pl.num_programs(1) - 1)
    def _():
        y = res_ref[...] + acc_s[...].T
        if final:
            y = _rms(y, fg_ref[...])
        o_ref[...] = y


def _peer_dense(hn, u, vt, sel, res, fgain, final, tt=512):
    T, D = res.shape
    E = u.shape[0]
    te = DENSE_ROWS * P_NKEYS
    n_tiles = E // te
    kern = functools.partial(_peer_dense_kernel, te=te, tt=tt, final=final)
    per_token = pl.BlockSpec((P_HEADS, P_NKEYS, tt), lambda i, j: (0, 0, i))
    return pl.pallas_call(
        kern,
        grid=(T // tt, n_tiles),
        in_specs=[pl.BlockSpec((tt, D), lambda i, j: (i, 0)),
                  pl.BlockSpec((te, D), lambda i, j: (j, 0)),
                  pl.BlockSpec((te, D), lambda i, j: (j, 0)),
                  per_token, per_token, per_token, per_token,
                  pl.BlockSpec((tt, D), lambda i, j: (i, 0)),
                  pl.BlockSpec((1, D), lambda i, j: (0, 0))],
        out_specs=pl.BlockSpec((tt, D), lambda i, j: (i, 0)),
        out_shape=jax.ShapeDtypeStruct((T, D), F32),
        scratch_shapes=[pltpu.VMEM((P_HEADS, DENSE_ROWS, PACKED_ROWS, tt), BF16),
                        pltpu.VMEM((P_HEADS, DENSE_ROWS, PACKED_ROWS, tt), BF16),
                        pltpu.VMEM((D, tt), F32)],
        compiler_params=_cparams(("parallel", "arbitrary")),
        name="peer_dense",
    )(hn, u, vt, *sel, res, fgain.reshape(1, D))


def _peer(x, g, wq, keys, u, v, fgain, final):
    hn, st = _peer_scores(x, g, wq, keys)
    rank2, n1, e2, coef = _peer_topk(st)
    return _peer_dense(hn, u.astype(BF16), v.astype(BF16), (rank2, e2, n1, coef), x, fgain, final)


def _block_diag(blocks):
    H, r, c = blocks.shape
    eye = jnp.eye(H, dtype=blocks.dtype)
    return (eye[:, None, :, None] * blocks[:, :, None, :]).reshape(H * r, H * c)


def _in_proj_weights(w_in):
    sizes = (A_HEADS * A_HEAD_DIM, A_LATENT, IDX_HEADS * IDX_DIM, IDX_DIM, IDX_HEADS,
             R_HEADS * R_DK, R_HEADS * R_DK, R_HEADS * R_DV, R_HEADS * R_DV)
    pts = np.cumsum(sizes)[:-1].tolist()
    qa, ckv, qi, ki, wi, rq, rk, rv, rg = jnp.split(w_in, pts, axis=1)
    pad = jnp.zeros((w_in.shape[0], LANES - IDX_DIM - IDX_HEADS), w_in.dtype)
    return (jnp.concatenate([rq, rk, rv, rg, qa, ckv], axis=1),
            jnp.concatenate([qi, ki, wi, pad], axis=1))


def kernel(x, mix_norm_e, w_in, kv_norm, w_uk, w_uv, w_o, mix_norm_o, conv_w1, conv_b1, conv_dw, conv_dw_b, conv_ln_g, conv_ln_b, conv_w2, conv_b2, ffn_norm, peer_wq, peer_keys, peer_u, peer_v, final_norm):
    B, L, D = x.shape
    T = B * L
    depth = ffn_norm.shape[0]
    xf = x.reshape(T, D)
    for layer in range(depth):
        j = layer // 2
        if layer % 2 == 0:
            proj, idx = _in_proj(xf, mix_norm_e[j], *_in_proj_weights(w_in[j]))
            proj3 = proj.reshape(B, L, proj.shape[1])
            idx3 = idx.reshape(B, L, idx.shape[1])
            wuk_bd = _block_diag(jnp.swapaxes(w_uk[j], 1, 2)).astype(BF16)
            wuv_bd = _block_diag(w_uv[j]).astype(BF16)
            a_out = _dsa(proj3, idx3, kv_norm[j], wuk_bd, wuv_bd)
            b_out = _retention(proj3)
            wo = w_o[j].astype(BF16)
            na = A_HEADS * A_HEAD_DIM
            xf = _mm_res([a_out.reshape(T, na), b_out.reshape(T, -1)], [wo[:na], wo[na:]], xf)
        else:
            a = _norm_glu(xf, mix_norm_o[j], conv_w1[j].astype(BF16), conv_b1[j])
            y = _conv_ln(a.reshape(B, L, D), conv_dw[j], conv_dw_b[j], conv_ln_g[j], conv_ln_b[j])
            xf = _mm_res([y.reshape(T, D)], [conv_w2[j].astype(BF16)], xf, bias=conv_b2[j])
        xf = _peer(xf, ffn_norm[layer], peer_wq[layer], peer_keys[layer], peer_u[layer], peer_v[layer],
                   final_norm, final=(layer == depth - 1))
    return xf.reshape(B, L, D)
```

```python
import functools

import jax
import jax.numpy as jnp
import numpy as np
from jax import lax
from jax.experimental import pallas as pl
from jax.experimental.pallas import tpu as pltpu

F32 = jnp.float32
BF16 = jnp.bfloat16
I32 = jnp.int32

EPS = 1e-6
A_HEADS, A_HEAD_DIM, A_LATENT = 8, 64, 128
IDX_HEADS, IDX_DIM, TOPK_MAX = 8, 32, 256
R_HEADS, R_DK, R_DV, R_CHUNK = 4, 128, 128, 128
CONV_WIDTH = 31
P_HEADS, P_NKEYS, P_QDIM, P_TOPK = 8, 128, 256, 16

LANES = 128
SUBLANES = 8
VMEM_LIMIT = 56 * 1024 * 1024
INT_MIN = -(2 ** 31)
NEG_BIG = -1e30
LOG2_E = 1.4426950408889634

NT_DIMS = (((1,), (1,)), ((), ()))


def _cparams(sem, flags=None):
    return pltpu.CompilerParams(dimension_semantics=sem, vmem_limit_bytes=VMEM_LIMIT, flags=flags)


def _aligned(x, m):
    if isinstance(x, int):
        assert x % m == 0
        return x
    return pl.multiple_of(x, m)


def _rms(x, g):
    return x * lax.rsqrt(jnp.mean(x * x, axis=-1, keepdims=True) + EPS) * g


def _split(x):
    hi = x.astype(BF16)
    return hi, (x - hi.astype(F32)).astype(BF16)


def _dot3(a_hi, a_lo, b_hi, b_lo):
    d = functools.partial(jnp.dot, preferred_element_type=F32)
    return d(a_hi, b_hi) + (d(a_hi, b_lo) + d(a_lo, b_hi))


def _in_proj_kernel(x_ref, g_ref, w_ref, whi_ref, wlo_ref, o_ref, oi_ref):
    h = _rms(x_ref[...], g_ref[...])
    hi, lo = _split(h)
    o_ref[...] = jnp.dot(hi, w_ref[...], preferred_element_type=F32)
    oi_ref[...] = _dot3(hi, lo, whi_ref[...], wlo_ref[...])


def _in_proj(x, g, w_main, w_idx, tm=512):
    T, D = x.shape
    N, NI = w_main.shape[1], w_idx.shape[1]
    whi, wlo = _split(w_idx)
    return pl.pallas_call(
        _in_proj_kernel,
        grid=(T // tm,),
        in_specs=[pl.BlockSpec((tm, D), lambda i: (i, 0)),
                  pl.BlockSpec((1, D), lambda i: (0, 0)),
                  pl.BlockSpec((D, N), lambda i: (0, 0)),
                  pl.BlockSpec((D, NI), lambda i: (0, 0)),
                  pl.BlockSpec((D, NI), lambda i: (0, 0))],
        out_specs=[pl.BlockSpec((tm, N), lambda i: (i, 0)),
                   pl.BlockSpec((tm, NI), lambda i: (i, 0))],
        out_shape=[jax.ShapeDtypeStruct((T, N), F32), jax.ShapeDtypeStruct((T, NI), F32)],
        compiler_params=_cparams(("parallel",)),
        name="in_proj",
    )(x, g.reshape(1, D), w_main.astype(BF16), whi, wlo)


def _norm_glu_kernel(x_ref, g_ref, wa_ref, wg_ref, ba_ref, bg_ref, o_ref, h_scr):
    @pl.when(pl.program_id(1) == 0)
    def _():
        h_scr[...] = _rms(x_ref[...], g_ref[...]).astype(BF16)

    h = h_scr[...]
    a = jnp.dot(h, wa_ref[...], preferred_element_type=F32) + ba_ref[...]
    gate = jnp.dot(h, wg_ref[...], preferred_element_type=F32) + bg_ref[...]
    o_ref[...] = a * jax.nn.sigmoid(gate)


def _norm_glu(x, g, w1, b1, tm=512, tn=1024):
    T, D = x.shape
    nj = D // tn
    b1 = b1.reshape(1, 2 * D)
    return pl.pallas_call(
        _norm_glu_kernel,
        grid=(T // tm, nj),
        in_specs=[pl.BlockSpec((tm, D), lambda i, j: (i, 0)),
                  pl.BlockSpec((1, D), lambda i, j: (0, 0)),
                  pl.BlockSpec((D, tn), lambda i, j: (0, j)),
                  pl.BlockSpec((D, tn), lambda i, j: (0, j + nj)),
                  pl.BlockSpec((1, tn), lambda i, j: (0, j)),
                  pl.BlockSpec((1, tn), lambda i, j: (0, j + nj))],
        out_specs=pl.BlockSpec((tm, tn), lambda i, j: (i, j)),
        out_shape=jax.ShapeDtypeStruct((T, D), F32),
        scratch_shapes=[pltpu.VMEM((tm, D), BF16)],
        compiler_params=_cparams(("parallel", "arbitrary")),
        name="norm_glu",
    )(x, g.reshape(1, D), w1, w1, b1, b1)


def _mm_res_kernel(*refs, n_lhs, has_bias):
    lhs = refs[:n_lhs]
    ws = refs[n_lhs:2 * n_lhs]
    rest = refs[2 * n_lhs:]
    if has_bias:
        b_ref, res_ref, o_ref = rest
        acc = res_ref[...] + b_ref[...]
    else:
        res_ref, o_ref = rest
        acc = res_ref[...]
    for a_ref, w_ref in zip(lhs, ws):
        acc = acc + jnp.dot(a_ref[...], w_ref[...], preferred_element_type=F32)
    o_ref[...] = acc


def _mm_res(lhs, ws, res, bias=None, tm=512):
    T, N = res.shape
    n = len(lhs)
    in_specs = [pl.BlockSpec((tm, a.shape[1]), lambda i: (i, 0)) for a in lhs]
    in_specs += [pl.BlockSpec(w.shape, lambda i: (0, 0)) for w in ws]
    args = list(lhs) + list(ws)
    if bias is not None:
        in_specs.append(pl.BlockSpec((1, N), lambda i: (0, 0)))
        args.append(bias.reshape(1, N))
    in_specs.append(pl.BlockSpec((tm, N), lambda i: (i, 0)))
    args.append(res)
    return pl.pallas_call(
        functools.partial(_mm_res_kernel, n_lhs=n, has_bias=bias is not None),
        grid=(T // tm,),
        in_specs=in_specs,
        out_specs=pl.BlockSpec((tm, N), lambda i: (i, 0)),
        out_shape=jax.ShapeDtypeStruct((T, N), F32),
        compiler_params=_cparams(("parallel",)),
        name="mm_res",
    )(*args)


def _dsa_kernel(qa_ref, qi_ref, misc_ref, ckv_ref, kit_ref, kvn_ref, wuk_ref, wuv_ref, o_ref,
                c_s, key_s, qst_s, qlat_s, wb_s, lg_s, bias_s, cmax_s, p_s, m_s, l_s, acc_s,
                *, Q, KC, topk, col_bits):
    qb = pl.program_id(1)
    H = A_HEADS
    NL = KC // LANES

    @pl.when(qb == 0)
    def _():
        c_s[...] = _rms(ckv_ref[0], kvn_ref[...]).astype(BF16)

    qi = qi_ref[0]
    wi = misc_ref[0][:, IDX_DIM:IDX_DIM + IDX_HEADS] * (IDX_HEADS ** -0.5)
    for h in range(IDX_HEADS):
        hi, lo = _split(qi[:, h * IDX_DIM:(h + 1) * IDX_DIM])
        qst_s[h * Q:(h + 1) * Q, :] = jnp.concatenate([hi, lo, hi], axis=1)
        wb_s[h] = jnp.broadcast_to(wi[:, h:h + 1], (Q, LANES))
    ql = jnp.dot(qa_ref[0].astype(BF16), wuk_ref[...], preferred_element_type=F32) * (A_HEAD_DIM ** -0.5 * LOG2_E)
    for h in range(H):
        qlat_s[h * Q:(h + 1) * Q, :] = ql[:, h * A_LATENT:(h + 1) * A_LATENT].astype(BF16)

    n_chunks = ((qb + 1) * Q + KC - 1) // KC
    q_pos = qb * Q + lax.broadcasted_iota(I32, (Q, LANES), 0)
    lane = lax.broadcasted_iota(I32, (Q, LANES), 1)

    RG = 4 * SUBLANES
    lane_rg = lax.broadcasted_iota(I32, (RG, LANES), 1)
    row_rg = lax.broadcasted_iota(I32, (RG, LANES), 0)

    def score_chunk(c, carry):
        off = pl.multiple_of(c * KC, KC)
        lg_s[...] = jnp.dot(qst_s[...], kit_ref[0, :, pl.ds(off, KC)], preferred_element_type=F32)
        for r0 in range(0, Q, RG):
            a = None
            for h in range(IDX_HEADS):
                w = jnp.concatenate([wb_s[h, r0:r0 + RG, :]] * NL, axis=1)
                t = jnp.maximum(lg_s[h * Q + r0:h * Q + r0 + RG, :], 0.0) * w
                a = t if a is None else a + t
            bits = pltpu.bitcast(a, I32)
            sgn = bits >> 31
            key = (bits ^ (sgn & 0x7FFFFFFF)) - sgn
            for l in range(NL):
                causal = off + l * LANES + lane_rg <= qb * Q + r0 + row_rg
                key_s[c * NL + l, r0:r0 + RG, :] = jnp.where(causal, key[:, l * LANES:(l + 1) * LANES], INT_MIN)
        return carry

    lax.fori_loop(0, n_chunks, score_chunk, 0)

    SR = min(Q, 16 * SUBLANES)
    lane_sr = lax.broadcasted_iota(I32, (SR, LANES), 1)

    def row_count(pred):
        cnts = []
        for r0 in range(0, Q, SR):
            def body(c, cnt, r0=r0):
                for l in range(NL):
                    blk = key_s[c * NL + l, r0:r0 + SR, :]
                    cnt = cnt + jnp.where(pred(blk, r0, c * KC + l * LANES), 1, 0)
                return cnt

            cnts.append(lax.fori_loop(0, n_chunks, body, jnp.zeros((SR, LANES), I32)))
        return jnp.sum(jnp.concatenate(cnts, axis=0).astype(F32), axis=1, keepdims=True)

    def bit_body(i, carry):
        tau, n_at = carry
        cand = tau | lax.shift_left(jnp.int32(1), 31 - i)
        cand_s = cand ^ INT_MIN
        tot = row_count(lambda blk, r0, col0: blk >= cand_s[r0:r0 + SR])
        take = tot >= topk
        return jnp.where(take, cand, tau), jnp.where(take, tot, n_at)

    zero = jnp.zeros((Q, LANES), I32)
    tau, n_ge = lax.fori_loop(0, 32, bit_body, (zero, jnp.zeros((Q, 1), F32)))
    thr = jnp.maximum(tau ^ INT_MIN, INT_MIN + 1)

    excess = n_ge > topk
    cmax_s[...] = jnp.full((Q, LANES), 2 ** 31 - 1, I32)

    @pl.when(jnp.max(jnp.where(excess, 1.0, 0.0)) > 0.0)
    def _():
        need = topk - row_count(lambda blk, r0, col0: blk > thr[r0:r0 + SR])

        def col_bit(i, y):
            cand = y | lax.shift_left(jnp.int32(1), col_bits - 1 - i)
            tot = row_count(lambda blk, r0, col0:
                            (blk == thr[r0:r0 + SR]) & (col0 + lane_sr < cand[r0:r0 + SR]))
            return jnp.where(tot < need, cand, y)

        y = lax.fori_loop(0, col_bits, col_bit, zero)
        cmax_s[...] = jnp.where(excess, y, 2 ** 31 - 1)

    cmax = cmax_s[...]
    m_s[...] = jnp.full(m_s.shape, NEG_BIG, F32)
    l_s[...] = jnp.zeros(l_s.shape, F32)
    acc_s[...] = jnp.zeros(acc_s.shape, F32)

    def att_chunk(c, carry):
        off = pl.multiple_of(c * KC, KC)
        ck = c_s[pl.ds(off, KC), :]
        lg_s[...] = lax.dot_general(qlat_s[...], ck, NT_DIMS, preferred_element_type=F32)
        for l in range(NL):
            blk = key_s[c * NL + l]
            sel = (blk > thr) | ((blk == thr) & (off + l * LANES + lane <= cmax))
            bias_s[l] = jnp.where(sel, 0.0, NEG_BIG)

        for h in range(H):
            for r0 in range(0, Q, RG):
                rows = slice(h * Q + r0, h * Q + r0 + RG)
                m_prev = m_s[rows, :]
                lg = lg_s[rows, :]
                lgs = [lg[:, l * LANES:(l + 1) * LANES] + bias_s[l, r0:r0 + RG, :] for l in range(NL)]
                mx = lgs[0]
                for l in range(1, NL):
                    mx = jnp.maximum(mx, lgs[l])
                m_new = jnp.maximum(m_prev, jnp.max(mx, axis=1, keepdims=True))
                alpha = jnp.exp2(m_prev - m_new)
                ps = [jnp.exp2(x - m_new) for x in lgs]
                p_s[rows, :] = jnp.concatenate(ps, axis=1).astype(BF16)
                psum = ps[0]
                for l in range(1, NL):
                    psum = psum + ps[l]
                l_s[rows, :] = alpha * l_s[rows, :] + psum
                m_s[rows, :] = m_new
                acc_s[rows, :] = acc_s[rows, :] * alpha
        acc_s[...] += jnp.dot(p_s[...], ck, preferred_element_type=F32)
        return carry

    lax.fori_loop(0, n_chunks, att_chunk, 0)

    o = acc_s[...] / jnp.sum(l_s[...], axis=1, keepdims=True)
    o_cat = jnp.concatenate([o[h * Q:(h + 1) * Q, :] for h in range(H)], axis=1).astype(BF16)
    o_ref[0] = jnp.dot(o_cat, wuv_ref[...], preferred_element_type=F32).astype(BF16)


def _dsa(proj3, idx3, kv_norm, wuk_bd, wuv_bd, Q=256, KC=512):
    B, L, _ = proj3.shape
    H = A_HEADS
    topk = min(TOPK_MAX, L // 4)
    qa_w, qi_w = A_HEADS * A_HEAD_DIM, IDX_HEADS * IDX_DIM
    qa_col = 4 * R_HEADS * R_DK
    k_hi, k_lo = _split(jnp.swapaxes(idx3[:, :, qi_w:qi_w + IDX_DIM], 1, 2))
    kit = jnp.concatenate([k_hi, k_hi, k_lo], axis=1)
    kern = functools.partial(_dsa_kernel, Q=Q, KC=KC, topk=topk, col_bits=max(1, (L - 1).bit_length()))
    return pl.pallas_call(
        kern,
        grid=(B, L // Q),
        in_specs=[pl.BlockSpec((1, Q, qa_w), lambda b, q: (b, q, qa_col // qa_w)),
                  pl.BlockSpec((1, Q, qi_w), lambda b, q: (b, q, 0)),
                  pl.BlockSpec((1, Q, LANES), lambda b, q: (b, q, qi_w // LANES)),
                  pl.BlockSpec((1, L, A_LATENT), lambda b, q: (b, 0, (qa_col + qa_w) // A_LATENT)),
                  pl.BlockSpec((1, 3 * IDX_DIM, L), lambda b, q: (b, 0, 0)),
                  pl.BlockSpec((1, A_LATENT), lambda b, q: (0, 0)),
                  pl.BlockSpec(wuk_bd.shape, lambda b, q: (0, 0)),
                  pl.BlockSpec(wuv_bd.shape, lambda b, q: (0, 0))],
        out_specs=pl.BlockSpec((1, Q, H * A_HEAD_DIM), lambda b, q: (b, q, 0)),
        out_shape=jax.ShapeDtypeStruct((B, L, H * A_HEAD_DIM), BF16),
        scratch_shapes=[pltpu.VMEM((L, A_LATENT), BF16),
                        pltpu.VMEM((L // LANES, Q, LANES), I32),
                        pltpu.VMEM((IDX_HEADS * Q, 3 * IDX_DIM), BF16),
                        pltpu.VMEM((H * Q, A_LATENT), BF16),
                        pltpu.VMEM((IDX_HEADS, Q, LANES), F32),
                        pltpu.VMEM((H * Q, KC), F32),
                        pltpu.VMEM((KC // LANES, Q, LANES), F32),
                        pltpu.VMEM((Q, LANES), I32),
                        pltpu.VMEM((H * Q, KC), BF16),
                        pltpu.VMEM((H * Q, LANES), F32),
                        pltpu.VMEM((H * Q, LANES), F32),
                        pltpu.VMEM((H * Q, A_LATENT), F32)],
        compiler_params=_cparams(("arbitrary", "arbitrary")),
        name="dsa",
    )(proj3, idx3, idx3, proj3, kit, kv_norm.reshape(1, A_LATENT), wuk_bd, wuv_bd)


def _ret_kernel(q_ref, k_ref, v_ref, g_ref, cos_ref, sin_ref, dec_ref, xi_ref, zeta_ref, gch_ref,
                o_ref, st_s, *, B):
    @pl.when(pl.program_id(0) == 0)
    def _():
        st_s[...] = jnp.zeros(st_s.shape, F32)

    C = R_CHUNK
    cos = cos_ref[...]
    sin = sin_ref[...]
    even = (lax.broadcasted_iota(I32, (C, R_DK), 1) & 1) == 0

    def rot(x):
        partner = jnp.where(even, pltpu.roll(x, R_DK - 1, 1), pltpu.roll(x, 1, 1))
        return x * cos + partner * sin

    for b in range(B):
        for h in range(R_HEADS):
            sl = slice(h * R_DK, (h + 1) * R_DK)
            q = rot(q_ref[b, :, sl])
            k = rot(k_ref[b, :, sl]) * (R_DK ** -0.5)
            vb = v_ref[b, :, sl].astype(BF16)
            qb = q.astype(BF16)
            s = lax.dot_general(qb, k.astype(BF16), NT_DIMS, preferred_element_type=F32) * dec_ref[h]
            inner = jnp.dot(s.astype(BF16), vb, preferred_element_type=F32)
            st = st_s[b, h]
            cross = jnp.dot(qb, st.astype(BF16), preferred_element_type=F32) * xi_ref[h]
            kz = (k * zeta_ref[h]).astype(BF16)
            st_s[b, h] = st * gch_ref[h] + jnp.dot(kz.T, vb, preferred_element_type=F32)
            out = inner + cross
            mu = jnp.mean(out, axis=-1, keepdims=True)
            d = out - mu
            var = jnp.mean(d * d, axis=-1, keepdims=True)
            y = d * lax.rsqrt(var + EPS)
            gate = g_ref[b, :, sl]
            o_ref[b, :, sl] = (gate * jax.nn.sigmoid(gate) * y).astype(BF16)


def _retention(proj3):
    B, L, _ = proj3.shape
    C, H = R_CHUNK, R_HEADS
    N = L // C
    d2 = R_DK // 2
    inv_freq = 1.0 / (10000.0 ** jnp.linspace(0.0, 1.0, d2, dtype=F32))
    ang = jnp.arange(L, dtype=F32)[:, None] * inv_freq[None, :]
    cos = jnp.repeat(jnp.cos(ang), 2, axis=1)
    sin = jnp.stack([-jnp.sin(ang), jnp.sin(ang)], axis=-1).reshape(L, R_DK)
    log_g = jnp.log1p(-jnp.exp2(-5.0 - jnp.arange(H, dtype=F32)))
    pos = jnp.arange(C, dtype=F32)
    diff = pos[:, None] - pos[None, :]
    decay = jnp.where(diff >= 0, jnp.exp(jnp.maximum(diff, 0.0)[None] * log_g[:, None, None]), 0.0)
    xi = jnp.broadcast_to(jnp.exp((pos + 1.0)[None, :] * log_g[:, None])[:, :, None], (H, C, R_DV))
    zeta = jnp.broadcast_to(jnp.exp((C - 1.0 - pos)[None, :] * log_g[:, None])[:, :, None], (H, C, R_DK))
    gch = jnp.broadcast_to(jnp.exp(C * log_g)[:, None, None], (H, 1, R_DV))
    W = H * R_DK
    colspec = lambda j: pl.BlockSpec((B, C, W), lambda n: (0, n, j))
    cst = lambda shape: pl.BlockSpec(shape, lambda n: (0,) * len(shape))
    return pl.pallas_call(
        functools.partial(_ret_kernel, B=B),
        grid=(N,),
        in_specs=[colspec(0), colspec(1), colspec(2), colspec(3),
                  pl.BlockSpec((C, R_DK), lambda n: (n, 0)),
                  pl.BlockSpec((C, R_DK), lambda n: (n, 0)),
                  cst((H, C, C)), cst((H, C, R_DV)), cst((H, C, R_DK)), cst((H, 1, R_DV))],
        out_specs=pl.BlockSpec((B, C, W), lambda n: (0, n, 0)),
        out_shape=jax.ShapeDtypeStruct((B, L, W), BF16),
        scratch_shapes=[pltpu.VMEM((B, H, R_DK, R_DV), F32)],
        compiler_params=_cparams(("arbitrary",)),
        name="retention",
    )(proj3, proj3, proj3, proj3, cos, sin, decay, xi, zeta, gch)


HALO = 32


CONV_ROWS = 4 * SUBLANES


def _conv_kernel(cur_ref, prev_ref, w_ref, b_ref, lg_ref, lb_ref, o_ref, ext_s, sh_s, wb_s, y_s, *, tl):
    i = pl.program_id(1)
    D = ext_s.shape[1]

    @pl.when((pl.program_id(0) == 0) & (i == 0))
    def _():
        for j in range(CONV_WIDTH):
            wb_s[j] = jnp.broadcast_to(w_ref[j:j + 1, :], (SUBLANES, D))

    ext_s[HALO:, :] = cur_ref[0]

    @pl.when(i == 0)
    def _():
        ext_s[:HALO, :] = jnp.zeros((HALO, D), F32)

    @pl.when(i > 0)
    def _():
        ext_s[:HALO, :] = prev_ref[0]

    base = HALO - (CONV_WIDTH - 1)
    offs = range(base, base + CONV_WIDTH)
    for r in range(1, SUBLANES):
        span = max(o for o in offs if o % SUBLANES == r) - r + tl
        sh_s[r, :span, :] = ext_s[r:r + span, :]

    def rows(t, carry):
        row0 = pl.multiple_of(t * CONV_ROWS, CONV_ROWS)
        n_sub = CONV_ROWS // SUBLANES
        y = [None] * n_sub
        for o in offs:
            r = o % SUBLANES
            start = pl.multiple_of(row0 + (o - r), SUBLANES)
            x = ext_s[pl.ds(start, CONV_ROWS), :] if r == 0 else sh_s[r, pl.ds(start, CONV_ROWS), :]
            w = wb_s[o - base]
            for k in range(n_sub):
                term = x[k * SUBLANES:(k + 1) * SUBLANES] * w
                y[k] = term if y[k] is None else y[k] + term
        y_s[pl.ds(row0, CONV_ROWS), :] = jnp.concatenate(y, axis=0)
        return carry

    lax.fori_loop(0, tl // CONV_ROWS, rows, 0)

    y = y_s[...] + b_ref[...]
    mu = jnp.mean(y, axis=-1, keepdims=True)
    d = y - mu
    var = jnp.mean(d * d, axis=-1, keepdims=True)
    z = d * lax.rsqrt(var + EPS) * lg_ref[...] + lb_ref[...]
    o_ref[0] = (z * jax.nn.sigmoid(z)).astype(BF16)


def _conv_ln(a3, w_dw, b_dw, ln_g, ln_b, tl=512):
    B, L, D = a3.shape
    r = tl // HALO
    row = lambda v: v.reshape(1, D)
    return pl.pallas_call(
        functools.partial(_conv_kernel, tl=tl),
        grid=(B, L // tl),
        in_specs=[pl.BlockSpec((1, tl, D), lambda b, i: (b, i, 0)),
                  pl.BlockSpec((1, HALO, D), lambda b, i: (b, jnp.maximum(i * r - 1, 0), 0)),
                  pl.BlockSpec((CONV_WIDTH, D), lambda b, i: (0, 0)),
                  pl.BlockSpec((1, D), lambda b, i: (0, 0)),
                  pl.BlockSpec((1, D), lambda b, i: (0, 0)),
                  pl.BlockSpec((1, D), lambda b, i: (0, 0))],
        out_specs=pl.BlockSpec((1, tl, D), lambda b, i: (b, i, 0)),
        out_shape=jax.ShapeDtypeStruct((B, L, D), BF16),
        scratch_shapes=[pltpu.VMEM((tl + HALO, D), F32),
                        pltpu.VMEM((SUBLANES, tl + HALO, D), F32),
                        pltpu.VMEM((CONV_WIDTH, SUBLANES, D), F32),
                        pltpu.VMEM((tl, D), F32)],
        compiler_params=_cparams(("arbitrary", "arbitrary")),
        name="conv_ln",
    )(a3, a3, w_dw, row(b_dw), row(ln_g), row(ln_b))


def _take_top(s, n):
    rows = []
    for _ in range(n):
        mx = jnp.max(s, axis=0, keepdims=True)
        rows.append(mx)
        s = jnp.where(s == mx, -jnp.inf, s)
    return rows, s


def _take_top_ranked(s, n):
    rows = []
    rank = jnp.full(s.shape, float(n), F32)
    for r in range(n):
        mx = jnp.max(s, axis=0, keepdims=True)
        rows.append(mx)
        hit = s == mx
        rank = jnp.where(hit, float(r), rank)
        s = jnp.where(hit, -jnp.inf, s)
    return rows, rank


def _select_pairs(s1, s2):
    n = P_TOPK
    tt = s1.shape[-1]
    r1, _ = _take_top(s1, n)
    r2, rank2 = _take_top_ranked(s2, n)
    v2 = jnp.concatenate(r2, axis=0)
    blocks = [r1[0] + v2]
    rows8 = lax.broadcasted_iota(I32, (SUBLANES, tt), 0)
    for i in range(1, SUBLANES):
        blocks.append(jnp.where(rows8 < n // (i + 1), r1[i] + v2[:SUBLANES], -jnp.inf))
    blocks.append(jnp.concatenate(r1[SUBLANES:], axis=0) + r2[0])
    top, _ = _take_top(jnp.concatenate(blocks, axis=0), n)
    kth = top[-1]
    z = None
    for r in top:
        e = jnp.exp(r - top[0])
        z = e if z is None else z + e
    hits = [jnp.where(blk >= kth, 1.0, 0.0) for blk in blocks]
    per_rank = [jnp.sum(hb, axis=0, keepdims=True) for hb in hits[:SUBLANES]]
    per_rank += [hits[SUBLANES][k:k + 1] for k in range(n - SUBLANES)]
    n1 = jnp.zeros(s1.shape, F32)
    for i in range(n):
        n1 = jnp.where(s1 == r1[i], per_rank[i], n1)
    e2 = jnp.exp(s2 - r2[0]).astype(BF16)
    coef = jnp.exp(s1 - r1[0]) * (0.5 / z)
    return rank2.astype(BF16), n1, e2, coef


def _peer_score_kernel(x_ref, g_ref, whi_ref, wlo_ref, k3_ref, h_ref, st_ref, hi_s, lo_s):
    @pl.when(pl.program_id(1) == 0)
    def _():
        hi, lo = _split(_rms(x_ref[...], g_ref[...]))
        hi_s[...] = hi
        lo_s[...] = lo
        h_ref[...] = hi

    q = _dot3(hi_s[...], lo_s[...], whi_ref[...], wlo_ref[...])
    q_hi, q_lo = _split(q)
    half = P_QDIM // 2
    for s in range(q.shape[1] // half):
        cols = slice(s * half, (s + 1) * half)
        q3 = jnp.concatenate([q_hi[:, cols], q_lo[:, cols], q_hi[:, cols]], axis=1)
        st_ref[s // 2, s % 2] = lax.dot_general(k3_ref[s % 2], q3, NT_DIMS, preferred_element_type=F32)


def _peer_scores(x, g, wq, keys, tm=512, hps=8):
    T, D = x.shape
    whi, wlo = _split(wq)
    k_hi, k_lo = _split(keys)
    k3 = jnp.concatenate([k_hi, k_hi, k_lo], axis=-1)
    return pl.pallas_call(
        _peer_score_kernel,
        grid=(T // tm, P_HEADS // hps),
        in_specs=[pl.BlockSpec((tm, D), lambda i, j: (i, 0)),
                  pl.BlockSpec((1, D), lambda i, j: (0, 0)),
                  pl.BlockSpec((D, hps * P_QDIM), lambda i, j: (0, j)),
                  pl.BlockSpec((D, hps * P_QDIM), lambda i, j: (0, j)),
                  pl.BlockSpec(k3.shape, lambda i, j: (0, 0, 0))],
        out_specs=[pl.BlockSpec((tm, D), lambda i, j: (i, 0)),
                   pl.BlockSpec((hps, 2, P_NKEYS, tm), lambda i, j: (j, 0, 0, i))],
        out_shape=[jax.ShapeDtypeStruct((T, D), BF16),
                   jax.ShapeDtypeStruct((P_HEADS, 2, P_NKEYS, T), F32)],
        scratch_shapes=[pltpu.VMEM((tm, D), BF16), pltpu.VMEM((tm, D), BF16)],
        compiler_params=_cparams(("parallel", "arbitrary")),
        name="peer_scores",
    )(x, g.reshape(1, D), whi, wlo, k3)


def _peer_topk_kernel(st_ref, rank2_ref, n1_ref, e2_ref, coef_ref):
    def head(h, carry):
        rank2_ref[h], n1_ref[h], e2_ref[h], coef_ref[h] = _select_pairs(st_ref[h, 0], st_ref[h, 1])
        return carry

    lax.fori_loop(0, P_HEADS, head, 0, unroll=2)


def _peer_topk(st, tt=256):
    T = st.shape[-1]
    spec = pl.BlockSpec((P_HEADS, P_NKEYS, tt), lambda i: (0, 0, i))
    shape = lambda dt: jax.ShapeDtypeStruct((P_HEADS, P_NKEYS, T), dt)
    return pl.pallas_call(
        _peer_topk_kernel,
        grid=(T // tt,),
        in_specs=[pl.BlockSpec((P_HEADS, 2, P_NKEYS, tt), lambda i: (0, 0, 0, i))],
        out_specs=[spec, spec, spec, spec],
        out_shape=[shape(BF16), shape(F32), shape(BF16), shape(F32)],
        compiler_params=_cparams(("parallel",)),
        name="peer_topk",
    )(st)


TILE_ROWS = 32
PACKED_ROWS = 16
UP_ROWS = 2
DENSE_ROWS = 16


def _peer_dense_kernel(h_ref, u_ref, vt_ref, rank_ref, e2_ref, n1_ref, coef_ref, res_ref, fg_ref, o_ref,
                       bn_s, bcoef_s, acc_s, *, te, tt, final):
    j = pl.program_id(1)
    NK = P_NKEYS

    @pl.when(j == 0)
    def _():
        acc_s[...] = jnp.zeros(acc_s.shape, F32)

    n_rows = te // NK
    reps = TILE_ROWS // PACKED_ROWS
    for h in range(P_HEADS):
        for grp in range(n_rows // SUBLANES):
            a0 = pl.multiple_of(j * n_rows + grp * SUBLANES, SUBLANES)
            n8 = n1_ref[h, pl.ds(a0, SUBLANES), :]
            coef8 = coef_ref[h, pl.ds(a0, SUBLANES), :]
            for r in range(SUBLANES):
                al = grp * SUBLANES + r
                bn_s[h, al] = jnp.broadcast_to(n8[r:r + 1], (PACKED_ROWS, tt)).astype(BF16)
                bcoef_s[h, al] = jnp.broadcast_to(coef8[r:r + 1], (PACKED_ROWS, tt)).astype(BF16)

    acts = []
    for al in range(n_rows):
        if al % UP_ROWS == 0:
            hb = lax.dot_general(u_ref[al * NK:(al + UP_ROWS) * NK, :], h_ref[...], NT_DIMS,
                                 preferred_element_type=F32)
        for b0 in range(0, NK, TILE_ROWS):
            sub = slice(b0, b0 + TILE_ROWS)
            g = None
            for h in range(P_HEADS):
                pairs = jnp.concatenate([bn_s[h, al]] * reps, axis=0)
                coef = jnp.concatenate([bcoef_s[h, al]] * reps, axis=0)
                e2 = e2_ref[h, sub, :]
                t = jnp.where(rank_ref[h, sub, :] < pairs, e2, jnp.zeros_like(e2)) * coef
                g = t if g is None else g + t
            r0 = (al % UP_ROWS) * NK + b0
            x = hb[r0:r0 + TILE_ROWS, :]
            acts.append((x * (1.0 + lax.erf(x * (2.0 ** -0.5)))).astype(BF16) * g)
    act = jnp.concatenate(acts, axis=0)
    acc_s[...] += lax.dot_general(vt_ref[...], act, (((0,), (0,)), ((), ())),
                                  preferred_element_type=F32)

    @pl.when(j == pl.num_programs(1) - 1)
    def _():
        y = res_ref[...] + acc_s[...].T
        if final:
            y = _rms(y, fg_ref[...])
        o_ref[...] = y


def _peer_dense(hn, u, vt, sel, res, fgain, final, tt=512):
    T, D = res.shape
    E = u.shape[0]
    te = DENSE_ROWS * P_NKEYS
    n_tiles = E // te
    kern = functools.partial(_peer_dense_kernel, te=te, tt=tt, final=final)
    per_token = pl.BlockSpec((P_HEADS, P_NKEYS, tt), lambda i, j: (0, 0, i))
    return pl.pallas_call(
        kern,
        grid=(T // tt, n_tiles),
        in_specs=[pl.BlockSpec((tt, D), lambda i, j: (i, 0)),
                  pl.BlockSpec((te, D), lambda i, j: (j, 0)),
                  pl.BlockSpec((te, D), lambda i, j: (j, 0)),
                  per_token, per_token, per_token, per_token,
                  pl.BlockSpec((tt, D), lambda i, j: (i, 0)),
                  pl.BlockSpec((1, D), lambda i, j: (0, 0))],
        out_specs=pl.BlockSpec((tt, D), lambda i, j: (i, 0)),
        out_shape=jax.ShapeDtypeStruct((T, D), F32),
        scratch_shapes=[pltpu.VMEM((P_HEADS, DENSE_ROWS, PACKED_ROWS, tt), BF16),
                        pltpu.VMEM((P_HEADS, DENSE_ROWS, PACKED_ROWS, tt), BF16),
                        pltpu.VMEM((D, tt), F32)],
        compiler_params=_cparams(("parallel", "arbitrary")),
        name="peer_dense",
    )(hn, u, vt, *sel, res, fgain.reshape(1, D))


def _peer(x, g, wq, keys, u, v, fgain, final):
    hn, st = _peer_scores(x, g, wq, keys)
    rank2, n1, e2, coef = _peer_topk(st)
    return _peer_dense(hn, u.astype(BF16), v.astype(BF16), (rank2, e2, n1, coef), x, fgain, final)


def _block_diag(blocks):
    H, r, c = blocks.shape
    eye = jnp.eye(H, dtype=blocks.dtype)
    return (eye[:, None, :, None] * blocks[:, :, None, :]).reshape(H * r, H * c)


def _in_proj_weights(w_in):
    sizes = (A_HEADS * A_HEAD_DIM, A_LATENT, IDX_HEADS * IDX_DIM, IDX_DIM, IDX_HEADS,
             R_HEADS * R_DK, R_HEADS * R_DK, R_HEADS * R_DV, R_HEADS * R_DV)
    pts = np.cumsum(sizes)[:-1].tolist()
    qa, ckv, qi, ki, wi, rq, rk, rv, rg = jnp.split(w_in, pts, axis=1)
    pad = jnp.zeros((w_in.shape[0], LANES - IDX_DIM - IDX_HEADS), w_in.dtype)
    return (jnp.concatenate([rq, rk, rv, rg, qa, ckv], axis=1),
            jnp.concatenate([qi, ki, wi, pad], axis=1))


def kernel(x, mix_norm_e, w_in, kv_norm, w_uk, w_uv, w_o, mix_norm_o, conv_w1, conv_b1, conv_dw, conv_dw_b, conv_ln_g, conv_ln_b, conv_w2, conv_b2, ffn_norm, peer_wq, peer_keys, peer_u, peer_v, final_norm):
    B, L, D = x.shape
    T = B * L
    depth = ffn_norm.shape[0]
    xf = x.reshape(T, D)
    for layer in range(depth):
        j = layer // 2
        if layer % 2 == 0:
            proj, idx = _in_proj(xf, mix_norm_e[j], *_in_proj_weights(w_in[j]))
            proj3 = proj.reshape(B, L, proj.shape[1])
            idx3 = idx.reshape(B, L, idx.shape[1])
            wuk_bd = _block_diag(jnp.swapaxes(w_uk[j], 1, 2)).astype(BF16)
            wuv_bd = _block_diag(w_uv[j]).astype(BF16)
            a_out = _dsa(proj3, idx3, kv_norm[j], wuk_bd, wuv_bd)
            b_out = _retention(proj3)
            wo = w_o[j].astype(BF16)
            na = A_HEADS * A_HEAD_DIM
            xf = _mm_res([a_out.reshape(T, na), b_out.reshape(T, -1)], [wo[:na], wo[na:]], xf)
        else:
            a = _norm_glu(xf, mix_norm_o[j], conv_w1[j].astype(BF16), conv_b1[j])
            y = _conv_ln(a.reshape(B, L, D), conv_dw[j], conv_dw_b[j], conv_ln_g[j], conv_ln_b[j])
            xf = _mm_res([y.reshape(T, D)], [conv_w2[j].astype(BF16)], xf, bias=conv_b2[j])
        xf = _peer(xf, ffn_norm[layer], peer_wq[layer], peer_keys[layer], peer_u[layer], peer_v[layer],
                   final_norm, final=(layer == depth - 1))
    return xf.reshape(B, L, D)
```

```python
import functools

import jax
import jax.numpy as jnp
import numpy as np
from jax import lax
from jax.experimental import pallas as pl
from jax.experimental.pallas import tpu as pltpu

F32 = jnp.float32
BF16 = jnp.bfloat16
I32 = jnp.int32

EPS = 1e-6
A_HEADS, A_HEAD_DIM, A_LATENT = 8, 64, 128
IDX_HEADS, IDX_DIM, TOPK_MAX = 8, 32, 256
R_HEADS, R_DK, R_DV, R_CHUNK = 4, 128, 128, 128
CONV_WIDTH = 31
P_HEADS, P_NKEYS, P_QDIM, P_TOPK = 8, 128, 256, 16

LANES = 128
SUBLANES = 8
VMEM_LIMIT = 56 * 1024 * 1024
INT_MIN = -(2 ** 31)
NEG_BIG = -1e30
LOG2_E = 1.4426950408889634

NT_DIMS = (((1,), (1,)), ((), ()))


def _cparams(sem, flags=None):
    return pltpu.CompilerParams(dimension_semantics=sem, vmem_limit_bytes=VMEM_LIMIT, flags=flags)


def _aligned(x, m):
    if isinstance(x, int):
        assert x % m == 0
        return x
    return pl.multiple_of(x, m)


def _rms(x, g):
    return x * lax.rsqrt(jnp.mean(x * x, axis=-1, keepdims=True) + EPS) * g


def _split(x):
    hi = x.astype(BF16)
    return hi, (x - hi.astype(F32)).astype(BF16)


def _dot3(a_hi, a_lo, b_hi, b_lo):
    d = functools.partial(jnp.dot, preferred_element_type=F32)
    return d(a_hi, b_hi) + (d(a_hi, b_lo) + d(a_lo, b_hi))


def _in_proj_kernel(x_ref, g_ref, w_ref, whi_ref, wlo_ref, o_ref, oi_ref):
    h = _rms(x_ref[...], g_ref[...])
    hi, lo = _split(h)
    o_ref[...] = jnp.dot(hi, w_ref[...], preferred_element_type=F32)
    oi_ref[...] = _dot3(hi, lo, whi_ref[...], wlo_ref[...])


def _in_proj(x, g, w_main, w_idx, tm=512):
    T, D = x.shape
    N, NI = w_main.shape[1], w_idx.shape[1]
    whi, wlo = _split(w_idx)
    return pl.pallas_call(
        _in_proj_kernel,
        grid=(T // tm,),
        in_specs=[pl.BlockSpec((tm, D), lambda i: (i, 0)),
                  pl.BlockSpec((1, D), lambda i: (0, 0)),
                  pl.BlockSpec((D, N), lambda i: (0, 0)),
                  pl.BlockSpec((D, NI), lambda i: (0, 0)),
                  pl.BlockSpec((D, NI), lambda i: (0, 0))],
        out_specs=[pl.BlockSpec((tm, N), lambda i: (i, 0)),
                   pl.BlockSpec((tm, NI), lambda i: (i, 0))],
        out_shape=[jax.ShapeDtypeStruct((T, N), F32), jax.ShapeDtypeStruct((T, NI), F32)],
        compiler_params=_cparams(("parallel",)),
        name="in_proj",
    )(x, g.reshape(1, D), w_main.astype(BF16), whi, wlo)


def _norm_glu_kernel(x_ref, g_ref, wa_ref, wg_ref, ba_ref, bg_ref, o_ref, h_scr):
    @pl.when(pl.program_id(1) == 0)
    def _():
        h_scr[...] = _rms(x_ref[...], g_ref[...]).astype(BF16)

    h = h_scr[...]
    a = jnp.dot(h, wa_ref[...], preferred_element_type=F32) + ba_ref[...]
    gate = jnp.dot(h, wg_ref[...], preferred_element_type=F32) + bg_ref[...]
    o_ref[...] = a * jax.nn.sigmoid(gate)


def _norm_glu(x, g, w1, b1, tm=512, tn=1024):
    T, D = x.shape
    nj = D // tn
    b1 = b1.reshape(1, 2 * D)
    return pl.pallas_call(
        _norm_glu_kernel,
        grid=(T // tm, nj),
        in_specs=[pl.BlockSpec((tm, D), lambda i, j: (i, 0)),
                  pl.BlockSpec((1, D), lambda i, j: (0, 0)),
                  pl.BlockSpec((D, tn), lambda i, j: (0, j)),
                  pl.BlockSpec((D, tn), lambda i, j: (0, j + nj)),
                  pl.BlockSpec((1, tn), lambda i, j: (0, j)),
                  pl.BlockSpec((1, tn), lambda i, j: (0, j + nj))],
        out_specs=pl.BlockSpec((tm, tn), lambda i, j: (i, j)),
        out_shape=jax.ShapeDtypeStruct((T, D), F32),
        scratch_shapes=[pltpu.VMEM((tm, D), BF16)],
        compiler_params=_cparams(("parallel", "arbitrary")),
        name="norm_glu",
    )(x, g.reshape(1, D), w1, w1, b1, b1)


def _mm_res_kernel(*refs, n_lhs, has_bias):
    lhs = refs[:n_lhs]
    ws = refs[n_lhs:2 * n_lhs]
    rest = refs[2 * n_lhs:]
    if has_bias:
        b_ref, res_ref, o_ref = rest
        acc = res_ref[...] + b_ref[...]
    else:
        res_ref, o_ref = rest
        acc = res_ref[...]
    for a_ref, w_ref in zip(lhs, ws):
        acc = acc + jnp.dot(a_ref[...], w_ref[...], preferred_element_type=F32)
    o_ref[...] = acc


def _mm_res(lhs, ws, res, bias=None, tm=512):
    T, N = res.shape
    n = len(lhs)
    in_specs = [pl.BlockSpec((tm, a.shape[1]), lambda i: (i, 0)) for a in lhs]
    in_specs += [pl.BlockSpec(w.shape, lambda i: (0, 0)) for w in ws]
    args = list(lhs) + list(ws)
    if bias is not None:
        in_specs.append(pl.BlockSpec((1, N), lambda i: (0, 0)))
        args.append(bias.reshape(1, N))
    in_specs.append(pl.BlockSpec((tm, N), lambda i: (i, 0)))
    args.append(res)
    return pl.pallas_call(
        functools.partial(_mm_res_kernel, n_lhs=n, has_bias=bias is not None),
        grid=(T // tm,),
        in_specs=in_specs,
        out_specs=pl.BlockSpec((tm, N), lambda i: (i, 0)),
        out_shape=jax.ShapeDtypeStruct((T, N), F32),
        compiler_params=_cparams(("parallel",)),
        name="mm_res",
    )(*args)


def _dsa_kernel(qa_ref, qi_ref, misc_ref, ckv_ref, kit_ref, kvn_ref, wuk_ref, wuv_ref, o_ref,
                c_s, key_s, qst_s, qlat_s, wb_s, lg_s, bias_s, cmax_s, p_s, m_s, l_s, acc_s,
                *, Q, KC, topk, col_bits):
    qb = pl.program_id(1)
    H = A_HEADS
    NL = KC // LANES

    @pl.when(qb == 0)
    def _():
        c_s[...] = _rms(ckv_ref[0], kvn_ref[...]).astype(BF16)

    qi = qi_ref[0]
    wi = misc_ref[0][:, IDX_DIM:IDX_DIM + IDX_HEADS] * (IDX_HEADS ** -0.5)
    for h in range(IDX_HEADS):
        hi, lo = _split(qi[:, h * IDX_DIM:(h + 1) * IDX_DIM])
        qst_s[h * Q:(h + 1) * Q, :] = jnp.concatenate([hi, lo, hi], axis=1)
        wb_s[h] = jnp.broadcast_to(wi[:, h:h + 1], (Q, LANES))
    ql = jnp.dot(qa_ref[0].astype(BF16), wuk_ref[...], preferred_element_type=F32) * (A_HEAD_DIM ** -0.5 * LOG2_E)
    for h in range(H):
        qlat_s[h * Q:(h + 1) * Q, :] = ql[:, h * A_LATENT:(h + 1) * A_LATENT].astype(BF16)

    n_chunks = ((qb + 1) * Q + KC - 1) // KC
    q_pos = qb * Q + lax.broadcasted_iota(I32, (Q, LANES), 0)
    lane = lax.broadcasted_iota(I32, (Q, LANES), 1)

    RG = 4 * SUBLANES
    lane_rg = lax.broadcasted_iota(I32, (RG, LANES), 1)
    row_rg = lax.broadcasted_iota(I32, (RG, LANES), 0)

    def score_chunk(c, carry):
        off = pl.multiple_of(c * KC, KC)
        lg_s[...] = jnp.dot(qst_s[...], kit_ref[0, :, pl.ds(off, KC)], preferred_element_type=F32)
        for r0 in range(0, Q, RG):
            a = None
            for h in range(IDX_HEADS):
                w = jnp.concatenate([wb_s[h, r0:r0 + RG, :]] * NL, axis=1)
                t = jnp.maximum(lg_s[h * Q + r0:h * Q + r0 + RG, :], 0.0) * w
                a = t if a is None else a + t
            bits = pltpu.bitcast(a, I32)
            sgn = bits >> 31
            key = (bits ^ (sgn & 0x7FFFFFFF)) - sgn
            for l in range(NL):
                causal = off + l * LANES + lane_rg <= qb * Q + r0 + row_rg
                key_s[c * NL + l, r0:r0 + RG, :] = jnp.where(causal, key[:, l * LANES:(l + 1) * LANES], INT_MIN)
        return carry

    lax.fori_loop(0, n_chunks, score_chunk, 0)

    SR = min(Q, 16 * SUBLANES)
    lane_sr = lax.broadcasted_iota(I32, (SR, LANES), 1)

    def row_count(pred):
        cnts = []
        for r0 in range(0, Q, SR):
            def body(c, cnt, r0=r0):
                for l in range(NL):
                    blk = key_s[c * NL + l, r0:r0 + SR, :]
                    cnt = cnt + jnp.where(pred(blk, r0, c * KC + l * LANES), 1, 0)
                return cnt

            cnts.append(lax.fori_loop(0, n_chunks, body, jnp.zeros((SR, LANES), I32)))
        return jnp.sum(jnp.concatenate(cnts, axis=0).astype(F32), axis=1, keepdims=True)

    def bit_cond(carry):
        i, _, _, pending = carry
        return (i < 32) & (pending > 0.0)

    def bit_body(carry):
        i, tau, n_at, _ = carry
        cand = tau | lax.shift_left(jnp.int32(1), 31 - i)
        cand_s = cand ^ INT_MIN
        tot = row_count(lambda blk, r0, col0: blk >= cand_s[r0:r0 + SR])
        take = tot >= topk
        n_new = jnp.where(take, tot, n_at)
        pending = jnp.max(jnp.where(n_new == topk, 0.0, 1.0))
        return i + 1, jnp.where(take, cand, tau), n_new, pending

    zero = jnp.zeros((Q, LANES), I32)
    _, tau, n_ge, _ = lax.while_loop(bit_cond, bit_body,
                                     (jnp.int32(0), zero, jnp.zeros((Q, 1), F32), jnp.float32(1.0)))
    thr = jnp.maximum(tau ^ INT_MIN, INT_MIN + 1)

    excess = n_ge > topk
    cmax_s[...] = jnp.full((Q, LANES), 2 ** 31 - 1, I32)

    @pl.when(jnp.max(jnp.where(excess, 1.0, 0.0)) > 0.0)
    def _():
        need = topk - row_count(lambda blk, r0, col0: blk > thr[r0:r0 + SR])

        def col_bit(i, y):
            cand = y | lax.shift_left(jnp.int32(1), col_bits - 1 - i)
            tot = row_count(lambda blk, r0, col0:
                            (blk == thr[r0:r0 + SR]) & (col0 + lane_sr < cand[r0:r0 + SR]))
            return jnp.where(tot < need, cand, y)

        y = lax.fori_loop(0, col_bits, col_bit, zero)
        cmax_s[...] = jnp.where(excess, y, 2 ** 31 - 1)

    cmax = cmax_s[...]
    m_s[...] = jnp.full(m_s.shape, NEG_BIG, F32)
    l_s[...] = jnp.zeros(l_s.shape, F32)
    acc_s[...] = jnp.zeros(acc_s.shape, F32)

    def att_chunk(c, carry):
        off = pl.multiple_of(c * KC, KC)
        ck = c_s[pl.ds(off, KC), :]
        lg_s[...] = lax.dot_general(qlat_s[...], ck, NT_DIMS, preferred_element_type=F32)
        for l in range(NL):
            blk = key_s[c * NL + l]
            sel = (blk > thr) | ((blk == thr) & (off + l * LANES + lane <= cmax))
            bias_s[l] = jnp.where(sel, 0.0, NEG_BIG)

        for h in range(H):
            for r0 in range(0, Q, RG):
                rows = slice(h * Q + r0, h * Q + r0 + RG)
                m_prev = m_s[rows, :]
                lg = lg_s[rows, :]
                lgs = [lg[:, l * LANES:(l + 1) * LANES] + bias_s[l, r0:r0 + RG, :] for l in range(NL)]
                mx = lgs[0]
                for l in range(1, NL):
                    mx = jnp.maximum(mx, lgs[l])
                m_new = jnp.maximum(m_prev, jnp.max(mx, axis=1, keepdims=True))
                alpha = jnp.exp2(m_prev - m_new)
                ps = [jnp.exp2(x - m_new) for x in lgs]
                p_s[rows, :] = jnp.concatenate(ps, axis=1).astype(BF16)
                psum = ps[0]
                for l in range(1, NL):
                    psum = psum + ps[l]
                l_s[rows, :] = alpha * l_s[rows, :] + psum
                m_s[rows, :] = m_new
                acc_s[rows, :] = acc_s[rows, :] * alpha
        acc_s[...] += jnp.dot(p_s[...], ck, preferred_element_type=F32)
        return carry

    lax.fori_loop(0, n_chunks, att_chunk, 0)

    o = acc_s[...] / jnp.sum(l_s[...], axis=1, keepdims=True)
    o_cat = jnp.concatenate([o[h * Q:(h + 1) * Q, :] for h in range(H)], axis=1).astype(BF16)
    o_ref[0] = jnp.dot(o_cat, wuv_ref[...], preferred_element_type=F32).astype(BF16)


def _dsa(proj3, idx3, kv_norm, wuk_bd, wuv_bd, Q=256, KC=512):
    B, L, _ = proj3.shape
    H = A_HEADS
    topk = min(TOPK_MAX, L // 4)
    qa_w, qi_w = A_HEADS * A_HEAD_DIM, IDX_HEADS * IDX_DIM
    qa_col = 4 * R_HEADS * R_DK
    k_hi, k_lo = _split(jnp.swapaxes(idx3[:, :, qi_w:qi_w + IDX_DIM], 1, 2))
    kit = jnp.concatenate([k_hi, k_hi, k_lo], axis=1)
    kern = functools.partial(_dsa_kernel, Q=Q, KC=KC, topk=topk, col_bits=max(1, (L - 1).bit_length()))
    return pl.pallas_call(
        kern,
        grid=(B, L // Q),
        in_specs=[pl.BlockSpec((1, Q, qa_w), lambda b, q: (b, q, qa_col // qa_w)),
                  pl.BlockSpec((1, Q, qi_w), lambda b, q: (b, q, 0)),
                  pl.BlockSpec((1, Q, LANES), lambda b, q: (b, q, qi_w // LANES)),
                  pl.BlockSpec((1, L, A_LATENT), lambda b, q: (b, 0, (qa_col + qa_w) // A_LATENT)),
                  pl.BlockSpec((1, 3 * IDX_DIM, L), lambda b, q: (b, 0, 0)),
                  pl.BlockSpec((1, A_LATENT), lambda b, q: (0, 0)),
                  pl.BlockSpec(wuk_bd.shape, lambda b, q: (0, 0)),
                  pl.BlockSpec(wuv_bd.shape, lambda b, q: (0, 0))],
        out_specs=pl.BlockSpec((1, Q, H * A_HEAD_DIM), lambda b, q: (b, q, 0)),
        out_shape=jax.ShapeDtypeStruct((B, L, H * A_HEAD_DIM), BF16),
        scratch_shapes=[pltpu.VMEM((L, A_LATENT), BF16),
                        pltpu.VMEM((L // LANES, Q, LANES), I32),
                        pltpu.VMEM((IDX_HEADS * Q, 3 * IDX_DIM), BF16),
                        pltpu.VMEM((H * Q, A_LATENT), BF16),
                        pltpu.VMEM((IDX_HEADS, Q, LANES), F32),
                        pltpu.VMEM((H * Q, KC), F32),
                        pltpu.VMEM((KC // LANES, Q, LANES), F32),
                        pltpu.VMEM((Q, LANES), I32),
                        pltpu.VMEM((H * Q, KC), BF16),
                        pltpu.VMEM((H * Q, LANES), F32),
                        pltpu.VMEM((H * Q, LANES), F32),
                        pltpu.VMEM((H * Q, A_LATENT), F32)],
        compiler_params=_cparams(("arbitrary", "arbitrary")),
        name="dsa",
    )(proj3, idx3, idx3, proj3, kit, kv_norm.reshape(1, A_LATENT), wuk_bd, wuv_bd)


def _ret_kernel(q_ref, k_ref, v_ref, g_ref, cos_ref, sin_ref, dec_ref, xi_ref, zeta_ref, gch_ref,
                o_ref, st_s, *, B):
    @pl.when(pl.program_id(0) == 0)
    def _():
        st_s[...] = jnp.zeros(st_s.shape, F32)

    C = R_CHUNK
    cos = cos_ref[...]
    sin = sin_ref[...]
    even = (lax.broadcasted_iota(I32, (C, R_DK), 1) & 1) == 0

    def rot(x):
        partner = jnp.where(even, pltpu.roll(x, R_DK - 1, 1), pltpu.roll(x, 1, 1))
        return x * cos + partner * sin

    for b in range(B):
        for h in range(R_HEADS):
            sl = slice(h * R_DK, (h + 1) * R_DK)
            q = rot(q_ref[b, :, sl])
            k = rot(k_ref[b, :, sl]) * (R_DK ** -0.5)
            vb = v_ref[b, :, sl].astype(BF16)
            qb = q.astype(BF16)
            s = lax.dot_general(qb, k.astype(BF16), NT_DIMS, preferred_element_type=F32) * dec_ref[h]
            inner = jnp.dot(s.astype(BF16), vb, preferred_element_type=F32)
            st = st_s[b, h]
            cross = jnp.dot(qb, st.astype(BF16), preferred_element_type=F32) * xi_ref[h]
            kz = (k * zeta_ref[h]).astype(BF16)
            st_s[b, h] = st * gch_ref[h] + jnp.dot(kz.T, vb, preferred_element_type=F32)
            out = inner + cross
            mu = jnp.mean(out, axis=-1, keepdims=True)
            d = out - mu
            var = jnp.mean(d * d, axis=-1, keepdims=True)
            y = d * lax.rsqrt(var + EPS)
            gate = g_ref[b, :, sl]
            o_ref[b, :, sl] = (gate * jax.nn.sigmoid(gate) * y).astype(BF16)


def _retention(proj3):
    B, L, _ = proj3.shape
    C, H = R_CHUNK, R_HEADS
    N = L // C
    d2 = R_DK // 2
    inv_freq = 1.0 / (10000.0 ** jnp.linspace(0.0, 1.0, d2, dtype=F32))
    ang = jnp.arange(L, dtype=F32)[:, None] * inv_freq[None, :]
    cos = jnp.repeat(jnp.cos(ang), 2, axis=1)
    sin = jnp.stack([-jnp.sin(ang), jnp.sin(ang)], axis=-1).reshape(L, R_DK)
    log_g = jnp.log1p(-jnp.exp2(-5.0 - jnp.arange(H, dtype=F32)))
    pos = jnp.arange(C, dtype=F32)
    diff = pos[:, None] - pos[None, :]
    decay = jnp.where(diff >= 0, jnp.exp(jnp.maximum(diff, 0.0)[None] * log_g[:, None, None]), 0.0)
    xi = jnp.broadcast_to(jnp.exp((pos + 1.0)[None, :] * log_g[:, None])[:, :, None], (H, C, R_DV))
    zeta = jnp.broadcast_to(jnp.exp((C - 1.0 - pos)[None, :] * log_g[:, None])[:, :, None], (H, C, R_DK))
    gch = jnp.broadcast_to(jnp.exp(C * log_g)[:, None, None], (H, 1, R_DV))
    W = H * R_DK
    colspec = lambda j: pl.BlockSpec((B, C, W), lambda n: (0, n, j))
    cst = lambda shape: pl.BlockSpec(shape, lambda n: (0,) * len(shape))
    return pl.pallas_call(
        functools.partial(_ret_kernel, B=B),
        grid=(N,),
        in_specs=[colspec(0), colspec(1), colspec(2), colspec(3),
                  pl.BlockSpec((C, R_DK), lambda n: (n, 0)),
                  pl.BlockSpec((C, R_DK), lambda n: (n, 0)),
                  cst((H, C, C)), cst((H, C, R_DV)), cst((H, C, R_DK)), cst((H, 1, R_DV))],
        out_specs=pl.BlockSpec((B, C, W), lambda n: (0, n, 0)),
        out_shape=jax.ShapeDtypeStruct((B, L, W), BF16),
        scratch_shapes=[pltpu.VMEM((B, H, R_DK, R_DV), F32)],
        compiler_params=_cparams(("arbitrary",)),
        name="retention",
    )(proj3, proj3, proj3, proj3, cos, sin, decay, xi, zeta, gch)


HALO = 32


CONV_ROWS = 4 * SUBLANES


def _conv_kernel(cur_ref, prev_ref, w_ref, b_ref, lg_ref, lb_ref, o_ref, ext_s, sh_s, wb_s, y_s, *, tl):
    i = pl.program_id(1)
    D = ext_s.shape[1]

    @pl.when((pl.program_id(0) == 0) & (i == 0))
    def _():
        for j in range(CONV_WIDTH):
            wb_s[j] = jnp.broadcast_to(w_ref[j:j + 1, :], (SUBLANES, D))

    ext_s[HALO:, :] = cur_ref[0]

    @pl.when(i == 0)
    def _():
        ext_s[:HALO, :] = jnp.zeros((HALO, D), F32)

    @pl.when(i > 0)
    def _():
        ext_s[:HALO, :] = prev_ref[0]

    base = HALO - (CONV_WIDTH - 1)
    offs = range(base, base + CONV_WIDTH)
    for r in range(1, SUBLANES):
        span = max(o for o in offs if o % SUBLANES == r) - r + tl
        sh_s[r, :span, :] = ext_s[r:r + span, :]

    def rows(t, carry):
        row0 = pl.multiple_of(t * CONV_ROWS, CONV_ROWS)
        n_sub = CONV_ROWS // SUBLANES
        y = [None] * n_sub
        for o in offs:
            r = o % SUBLANES
            start = pl.multiple_of(row0 + (o - r), SUBLANES)
            x = ext_s[pl.ds(start, CONV_ROWS), :] if r == 0 else sh_s[r, pl.ds(start, CONV_ROWS), :]
            w = wb_s[o - base]
            for k in range(n_sub):
                term = x[k * SUBLANES:(k + 1) * SUBLANES] * w
                y[k] = term if y[k] is None else y[k] + term
        y_s[pl.ds(row0, CONV_ROWS), :] = jnp.concatenate(y, axis=0)
        return carry

    lax.fori_loop(0, tl // CONV_ROWS, rows, 0)

    y = y_s[...] + b_ref[...]
    mu = jnp.mean(y, axis=-1, keepdims=True)
    d = y - mu
    var = jnp.mean(d * d, axis=-1, keepdims=True)
    z = d * lax.rsqrt(var + EPS) * lg_ref[...] + lb_ref[...]
    o_ref[0] = (z * jax.nn.sigmoid(z)).astype(BF16)


def _conv_ln(a3, w_dw, b_dw, ln_g, ln_b, tl=512):
    B, L, D = a3.shape
    r = tl // HALO
    row = lambda v: v.reshape(1, D)
    return pl.pallas_call(
        functools.partial(_conv_kernel, tl=tl),
        grid=(B, L // tl),
        in_specs=[pl.BlockSpec((1, tl, D), lambda b, i: (b, i, 0)),
                  pl.BlockSpec((1, HALO, D), lambda b, i: (b, jnp.maximum(i * r - 1, 0), 0)),
                  pl.BlockSpec((CONV_WIDTH, D), lambda b, i: (0, 0)),
                  pl.BlockSpec((1, D), lambda b, i: (0, 0)),
                  pl.BlockSpec((1, D), lambda b, i: (0, 0)),
                  pl.BlockSpec((1, D), lambda b, i: (0, 0))],
        out_specs=pl.BlockSpec((1, tl, D), lambda b, i: (b, i, 0)),
        out_shape=jax.ShapeDtypeStruct((B, L, D), BF16),
        scratch_shapes=[pltpu.VMEM((tl + HALO, D), F32),
                        pltpu.VMEM((SUBLANES, tl + HALO, D), F32),
                        pltpu.VMEM((CONV_WIDTH, SUBLANES, D), F32),
                        pltpu.VMEM((tl, D), F32)],
        compiler_params=_cparams(("arbitrary", "arbitrary")),
        name="conv_ln",
    )(a3, a3, w_dw, row(b_dw), row(ln_g), row(ln_b))


def _take_top(s, n):
    rows = []
    for _ in range(n):
        mx = jnp.max(s, axis=0, keepdims=True)
        rows.append(mx)
        s = jnp.where(s == mx, -jnp.inf, s)
    return rows, s


def _take_top_ranked(s, n):
    rows = []
    rank = jnp.full(s.shape, float(n), F32)
    for r in range(n):
        mx = jnp.max(s, axis=0, keepdims=True)
        rows.append(mx)
        hit = s == mx
        rank = jnp.where(hit, float(r), rank)
        s = jnp.where(hit, -jnp.inf, s)
    return rows, rank


def _select_pairs(s1, s2):
    n = P_TOPK
    tt = s1.shape[-1]
    r1, _ = _take_top(s1, n)
    r2, rank2 = _take_top_ranked(s2, n)
    v2 = jnp.concatenate(r2, axis=0)
    blocks = [r1[0] + v2]
    rows8 = lax.broadcasted_iota(I32, (SUBLANES, tt), 0)
    for i in range(1, SUBLANES):
        blocks.append(jnp.where(rows8 < n // (i + 1), r1[i] + v2[:SUBLANES], -jnp.inf))
    blocks.append(jnp.concatenate(r1[SUBLANES:], axis=0) + r2[0])
    top, _ = _take_top(jnp.concatenate(blocks, axis=0), n)
    kth = top[-1]
    z = None
    for r in top:
        e = jnp.exp(r - top[0])
        z = e if z is None else z + e
    hits = [jnp.where(blk >= kth, 1.0, 0.0) for blk in blocks]
    per_rank = [jnp.sum(hb, axis=0, keepdims=True) for hb in hits[:SUBLANES]]
    per_rank += [hits[SUBLANES][k:k + 1] for k in range(n - SUBLANES)]
    n1 = jnp.zeros(s1.shape, F32)
    for i in range(n):
        n1 = jnp.where(s1 == r1[i], per_rank[i], n1)
    e2 = jnp.exp(s2 - r2[0]).astype(BF16)
    coef = jnp.exp(s1 - r1[0]) * (0.5 / z)
    return rank2.astype(BF16), n1, e2, coef


def _peer_score_kernel(x_ref, g_ref, whi_ref, wlo_ref, k3_ref, h_ref, st_ref, hi_s, lo_s):
    @pl.when(pl.program_id(1) == 0)
    def _():
        hi, lo = _split(_rms(x_ref[...], g_ref[...]))
        hi_s[...] = hi
        lo_s[...] = lo
        h_ref[...] = hi

    q = _dot3(hi_s[...], lo_s[...], whi_ref[...], wlo_ref[...])
    q_hi, q_lo = _split(q)
    half = P_QDIM // 2
    for s in range(q.shape[1] // half):
        cols = slice(s * half, (s + 1) * half)
        q3 = jnp.concatenate([q_hi[:, cols], q_lo[:, cols], q_hi[:, cols]], axis=1)
        st_ref[s // 2, s % 2] = lax.dot_general(k3_ref[s % 2], q3, NT_DIMS, preferred_element_type=F32)


def _peer_scores(x, g, wq, keys, tm=512, hps=8):
    T, D = x.shape
    whi, wlo = _split(wq)
    k_hi, k_lo = _split(keys)
    k3 = jnp.concatenate([k_hi, k_hi, k_lo], axis=-1)
    return pl.pallas_call(
        _peer_score_kernel,
        grid=(T // tm, P_HEADS // hps),
        in_specs=[pl.BlockSpec((tm, D), lambda i, j: (i, 0)),
                  pl.BlockSpec((1, D), lambda i, j: (0, 0)),
                  pl.BlockSpec((D, hps * P_QDIM), lambda i, j: (0, j)),
                  pl.BlockSpec((D, hps * P_QDIM), lambda i, j: (0, j)),
                  pl.BlockSpec(k3.shape, lambda i, j: (0, 0, 0))],
        out_specs=[pl.BlockSpec((tm, D), lambda i, j: (i, 0)),
                   pl.BlockSpec((hps, 2, P_NKEYS, tm), lambda i, j: (j, 0, 0, i))],
        out_shape=[jax.ShapeDtypeStruct((T, D), BF16),
                   jax.ShapeDtypeStruct((P_HEADS, 2, P_NKEYS, T), F32)],
        scratch_shapes=[pltpu.VMEM((tm, D), BF16), pltpu.VMEM((tm, D), BF16)],
        compiler_params=_cparams(("parallel", "arbitrary")),
        name="peer_scores",
    )(x, g.reshape(1, D), whi, wlo, k3)


def _peer_topk_kernel(st_ref, rank2_ref, n1_ref, e2_ref, coef_ref):
    def head(h, carry):
        rank2_ref[h], n1_ref[h], e2_ref[h], coef_ref[h] = _select_pairs(st_ref[h, 0], st_ref[h, 1])
        return carry

    lax.fori_loop(0, P_HEADS, head, 0, unroll=2)


def _peer_topk(st, tt=256):
    T = st.shape[-1]
    spec = pl.BlockSpec((P_HEADS, P_NKEYS, tt), lambda i: (0, 0, i))
    shape = lambda dt: jax.ShapeDtypeStruct((P_HEADS, P_NKEYS, T), dt)
    return pl.pallas_call(
        _peer_topk_kernel,
        grid=(T // tt,),
        in_specs=[pl.BlockSpec((P_HEADS, 2, P_NKEYS, tt), lambda i: (0, 0, 0, i))],
        out_specs=[spec, spec, spec, spec],
        out_shape=[shape(BF16), shape(F32), shape(BF16), shape(F32)],
        compiler_params=_cparams(("parallel",)),
        name="peer_topk",
    )(st)


TILE_ROWS = 32
PACKED_ROWS = 16
UP_ROWS = 2
DENSE_ROWS = 16


def _peer_dense_kernel(h_ref, u_ref, vt_ref, rank_ref, e2_ref, n1_ref, coef_ref, res_ref, fg_ref, o_ref,
                       bn_s, bcoef_s, acc_s, *, te, tt, final):
    j = pl.program_id(1)
    NK = P_NKEYS

    @pl.when(j == 0)
    def _():
        acc_s[...] = jnp.zeros(acc_s.shape, F32)

    n_rows = te // NK
    reps = TILE_ROWS // PACKED_ROWS
    for h in range(P_HEADS):
        for grp in range(n_rows // SUBLANES):
            a0 = pl.multiple_of(j * n_rows + grp * SUBLANES, SUBLANES)
            n8 = n1_ref[h, pl.ds(a0, SUBLANES), :]
            coef8 = coef_ref[h, pl.ds(a0, SUBLANES), :]
            for r in range(SUBLANES):
                al = grp * SUBLANES + r
                bn_s[h, al] = jnp.broadcast_to(n8[r:r + 1], (PACKED_ROWS, tt)).astype(BF16)
                bcoef_s[h, al] = jnp.broadcast_to(coef8[r:r + 1], (PACKED_ROWS, tt)).astype(BF16)

    acts = []
    for al in range(n_rows):
        if al % UP_ROWS == 0:
            hb = lax.dot_general(u_ref[al * NK:(al + UP_ROWS) * NK, :], h_ref[...], NT_DIMS,
                                 preferred_element_type=F32)
        for b0 in range(0, NK, TILE_ROWS):
            sub = slice(b0, b0 + TILE_ROWS)
            g = None
            for h in range(P_HEADS):
                pairs = jnp.concatenate([bn_s[h, al]] * reps, axis=0)
                coef = jnp.concatenate([bcoef_s[h, al]] * reps, axis=0)
                e2 = e2_ref[h, sub, :]
                t = jnp.where(rank_ref[h, sub, :] < pairs, e2, jnp.zeros_like(e2)) * coef
                g = t if g is None else g + t
            r0 = (al % UP_ROWS) * NK + b0
            x = hb[r0:r0 + TILE_ROWS, :]
            acts.append((x * (1.0 + lax.erf(x * (2.0 ** -0.5)))).astype(BF16) * g)
    act = jnp.concatenate(acts, axis=0)
    acc_s[...] += lax.dot_general(vt_ref[...], act, (((0,), (0,)), ((), ())),
                                  preferred_element_type=F32)

    @pl.when(j == pl.num_programs(1) - 1)
    def _():
        y = res_ref[...] + acc_s[...].T
        if final:
            y = _rms(y, fg_ref[...])
        o_ref[...] = y


def _peer_dense(hn, u, vt, sel, res, fgain, final, tt=512):
    T, D = res.shape
    E = u.shape[0]
    te = DENSE_ROWS * P_NKEYS
    n_tiles = E // te
    kern = functools.partial(_peer_dense_kernel, te=te, tt=tt, final=final)
    per_token = pl.BlockSpec((P_HEADS, P_NKEYS, tt), lambda i, j: (0, 0, i))
    return pl.pallas_call(
        kern,
        grid=(T // tt, n_tiles),
        in_specs=[pl.BlockSpec((tt, D), lambda i, j: (i, 0)),
                  pl.BlockSpec((te, D), lambda i, j: (j, 0)),
                  pl.BlockSpec((te, D), lambda i, j: (j, 0)),
                  per_token, per_token, per_token, per_token,
                  pl.BlockSpec((tt, D), lambda i, j: (i, 0)),
                  pl.BlockSpec((1, D), lambda i, j: (0, 0))],
        out_specs=pl.BlockSpec((tt, D), lambda i, j: (i, 0)),
        out_shape=jax.ShapeDtypeStruct((T, D), F32),
        scratch_shapes=[pltpu.VMEM((P_HEADS, DENSE_ROWS, PACKED_ROWS, tt), BF16),
                        pltpu.VMEM((P_HEADS, DENSE_ROWS, PACKED_ROWS, tt), BF16),
                        pltpu.VMEM((D, tt), F32)],
        compiler_params=_cparams(("parallel", "arbitrary")),
        name="peer_dense",
    )(hn, u, vt, *sel, res, fgain.reshape(1, D))


def _peer(x, g, wq, keys, u, v, fgain, final):
    hn, st = _peer_scores(x, g, wq, keys)
    rank2, n1, e2, coef = _peer_topk(st)
    return _peer_dense(hn, u.astype(BF16), v.astype(BF16), (rank2, e2, n1, coef), x, fgain, final)


def _block_diag(blocks):
    H, r, c = blocks.shape
    eye = jnp.eye(H, dtype=blocks.dtype)
    return (eye[:, None, :, None] * blocks[:, :, None, :]).reshape(H * r, H * c)


def _in_proj_weights(w_in):
    sizes = (A_HEADS * A_HEAD_DIM, A_LATENT, IDX_HEADS * IDX_DIM, IDX_DIM, IDX_HEADS,
             R_HEADS * R_DK, R_HEADS * R_DK, R_HEADS * R_DV, R_HEADS * R_DV)
    pts = np.cumsum(sizes)[:-1].tolist()
    qa, ckv, qi, ki, wi, rq, rk, rv, rg = jnp.split(w_in, pts, axis=1)
    pad = jnp.zeros((w_in.shape[0], LANES - IDX_DIM - IDX_HEADS), w_in.dtype)
    return (jnp.concatenate([rq, rk, rv, rg, qa, ckv], axis=1),
            jnp.concatenate([qi, ki, wi, pad], axis=1))


def kernel(x, mix_norm_e, w_in, kv_norm, w_uk, w_uv, w_o, mix_norm_o, conv_w1, conv_b1, conv_dw, conv_dw_b, conv_ln_g, conv_ln_b, conv_w2, conv_b2, ffn_norm, peer_wq, peer_keys, peer_u, peer_v, final_norm):
    B, L, D = x.shape
    T = B * L
    depth = ffn_norm.shape[0]
    xf = x.reshape(T, D)
    for layer in range(depth):
        j = layer // 2
        if layer % 2 == 0:
            proj, idx = _in_proj(xf, mix_norm_e[j], *_in_proj_weights(w_in[j]))
            proj3 = proj.reshape(B, L, proj.shape[1])
            idx3 = idx.reshape(B, L, idx.shape[1])
            wuk_bd = _block_diag(jnp.swapaxes(w_uk[j], 1, 2)).astype(BF16)
            wuv_bd = _block_diag(w_uv[j]).astype(BF16)
            a_out = _dsa(proj3, idx3, kv_norm[j], wuk_bd, wuv_bd)
            b_out = _retention(proj3)
            wo = w_o[j].astype(BF16)
            na = A_HEADS * A_HEAD_DIM
            xf = _mm_res([a_out.reshape(T, na), b_out.reshape(T, -1)], [wo[:na], wo[na:]], xf)
        else:
            a = _norm_glu(xf, mix_norm_o[j], conv_w1[j].astype(BF16), conv_b1[j])
            y = _conv_ln(a.reshape(B, L, D), conv_dw[j], conv_dw_b[j], conv_ln_g[j], conv_ln_b[j])
            xf = _mm_res([y.reshape(T, D)], [conv_w2[j].astype(BF16)], xf, bias=conv_b2[j])
        xf = _peer(xf, ffn_norm[layer], peer_wq[layer], peer_keys[layer], peer_u[layer], peer_v[layer],
                   final_norm, final=(layer == depth - 1))
    return xf.reshape(B, L, D)
```
